```python
import jax
import jax.numpy as jnp
from jax import lax
import numpy as np

D_MODEL = 1024
BATCH = 8
SEQ = 4096
DEPTH = 4

N_META = 16
EXPAND = 2
D_INNER = EXPAND * D_MODEL
BLOCK = 128
ROPE_THETA = 10000.0
EPS = 1e-6
NEG_INF = -1e30

MLA_HEADS = 8
MLA_Q_RANK = 384
MLA_KV_RANK = 256
MLA_NOPE = 128
MLA_ROPE = 64
MLA_QK = MLA_NOPE + MLA_ROPE
MLA_V = 128
MLA_WIDTH = MLA_HEADS * MLA_V

SWA_HEADS = 8
SWA_KV_HEADS = 2
SWA_REP = SWA_HEADS // SWA_KV_HEADS
SWA_HEAD_DIM = 128
SWA_WINDOW = 128
SWA_WIDTH = SWA_HEADS * SWA_HEAD_DIM
SWA_KV_WIDTH = SWA_KV_HEADS * SWA_HEAD_DIM

LRU_WIDTH = D_INNER
LRU_BLOCKS = 16
LRU_BLOCK_DIM = LRU_WIDTH // LRU_BLOCKS
LRU_C = 8.0
CONV_WIDTH = 4
CONV_PAD_LEFT = 2
CONV_PAD_RIGHT = 1

EVEN_SPLITS = (MLA_Q_RANK, MLA_KV_RANK, MLA_ROPE, SWA_WIDTH, SWA_KV_WIDTH, SWA_KV_WIDTH, D_INNER)
EVEN_IN = MLA_Q_RANK + MLA_KV_RANK + MLA_ROPE + SWA_WIDTH + 2 * SWA_KV_WIDTH + D_INNER
ODD_SPLITS = (LRU_WIDTH, D_INNER)
ODD_IN = LRU_WIDTH + D_INNER
N_EVEN = (DEPTH + 1) // 2
N_ODD = DEPTH // 2

kernel_name = "hybrid_mla_swa_rglru_encoder"


def rms_norm(x, g):
    xf = x.astype(jnp.float32)
    y = xf * lax.rsqrt(jnp.mean(xf * xf, axis=-1, keepdims=True) + EPS)
    return (y * g.astype(jnp.float32)).astype(x.dtype)


def rope_tables(T, dim):
    inv = ROPE_THETA ** (-jnp.arange(0, dim, 2, dtype=jnp.float32) / dim)
    ang = jnp.arange(T, dtype=jnp.float32)[:, None] * inv[None, :]
    return jnp.cos(ang), jnp.sin(ang)


def apply_rope(x, cos, sin):
    x1, x2 = jnp.split(x, 2, axis=-1)
    c = cos.astype(x.dtype)
    s = sin.astype(x.dtype)
    return jnp.concatenate([x1 * c - x2 * s, x1 * s + x2 * c], axis=-1)


def split_cols(z, sizes):
    out = []
    start = 0
    for s in sizes:
        out.append(z[..., start:start + s])
        start += s
    return out


def blocked_dense_attention(q, k, v, scale):
    B, H, T, dq = q.shape
    dv = v.shape[-1]
    nb = (T - N_META) // BLOCK

    def attend(qb):
        s = jnp.einsum('bhqd,bhkd->bhqk', qb, k, preferred_element_type=jnp.float32) * scale
        p = jax.nn.softmax(s, axis=-1).astype(v.dtype)
        return jnp.einsum('bhqk,bhkd->bhqd', p, v)

    o_meta = attend(q[:, :, :N_META])
    qb = q[:, :, N_META:].reshape(B, H, nb, BLOCK, dq).transpose(2, 0, 1, 3, 4)
    o_real = lax.map(attend, qb).transpose(1, 2, 0, 3, 4).reshape(B, H, nb * BLOCK, dv)
    return jnp.concatenate([o_meta, o_real], axis=2)


def banded_window_attention(q, k, v, sink, scale):
    B, G, R, T, d = q.shape
    L = T - N_META
    nb = L // BLOCK
    k_m, v_m = k[:, :, :N_META], v[:, :, :N_META]
    k_r, v_r = k[:, :, N_META:], v[:, :, N_META:]
    sink_l = sink.astype(jnp.float32).reshape(G, R)

    def band(t):
        tp = jnp.pad(t, ((0, 0), (0, 0), (BLOCK, BLOCK), (0, 0))).reshape(B, G, nb + 2, BLOCK, d)
        return jnp.concatenate([tp[:, :, :-2], tp[:, :, 1:-1], tp[:, :, 2:]], axis=3)

    k_b, v_b = band(k_r), band(v_r)
    q_b = q[:, :, :, N_META:].reshape(B, G, R, nb, BLOCK, d)
    blk = jnp.arange(nb)[:, None, None]
    qpos = blk * BLOCK + jnp.arange(BLOCK)[None, :, None]
    kpos = blk * BLOCK - BLOCK + jnp.arange(3 * BLOCK)[None, None, :]
    valid = (jnp.abs(qpos - kpos) <= SWA_WINDOW) & (kpos >= 0) & (kpos < L)
    s_meta = jnp.einsum('bgrnqd,bgmd->bgrnqm', q_b, k_m, preferred_element_type=jnp.float32) * scale
    s_band = jnp.einsum('bgrnqd,bgnkd->bgrnqk', q_b, k_b, preferred_element_type=jnp.float32) * scale
    s_band = jnp.where(valid, s_band, NEG_INF)
    s_sink = jnp.broadcast_to(sink_l[None, :, :, None, None, None], (B, G, R, nb, BLOCK, 1))
    p = jax.nn.softmax(jnp.concatenate([s_sink, s_meta, s_band], axis=-1), axis=-1).astype(v.dtype)
    o_real = (jnp.einsum('bgrnqm,bgmd->bgrnqd', p[..., 1:1 + N_META], v_m)
              + jnp.einsum('bgrnqk,bgnkd->bgrnqd', p[..., 1 + N_META:], v_b))
    o_real = o_real.reshape(B, G, R, L, d)

    q_m = q[:, :, :, :N_META]
    k_f, v_f = k_r[:, :, :BLOCK], v_r[:, :, :BLOCK]
    valid_m = (N_META + jnp.arange(BLOCK)[None, :] - jnp.arange(N_META)[:, None]) <= SWA_WINDOW
    s_mm = jnp.einsum('bgrqd,bgmd->bgrqm', q_m, k_m, preferred_element_type=jnp.float32) * scale
    s_mf = jnp.einsum('bgrqd,bgkd->bgrqk', q_m, k_f, preferred_element_type=jnp.float32) * scale
    s_mf = jnp.where(valid_m, s_mf, NEG_INF)
    s_ms = jnp.broadcast_to(sink_l[None, :, :, None, None], (B, G, R, N_META, 1))
    p_m = jax.nn.softmax(jnp.concatenate([s_ms, s_mm, s_mf], axis=-1), axis=-1).astype(v.dtype)
    o_meta = (jnp.einsum('bgrqm,bgmd->bgrqd', p_m[..., 1:1 + N_META], v_m)
              + jnp.einsum('bgrqk,bgkd->bgrqd', p_m[..., 1 + N_META:], v_f))
    return jnp.concatenate([o_meta, o_real], axis=3)


def mla_mixer(c_q, c_kv, k_pe, g_q_lat, g_kv_lat, w_uq, w_ukv, g_qn, g_kn, cos, sin):
    B, T, _ = c_q.shape
    q = jnp.einsum('btr,rhd->bhtd', rms_norm(c_q, g_q_lat), w_uq)
    kv = jnp.einsum('btr,rhd->bhtd', rms_norm(c_kv, g_kv_lat), w_ukv)
    k_nope, v = kv[..., :MLA_NOPE], kv[..., MLA_NOPE:]
    k_pe = jnp.broadcast_to(k_pe[:, None], (B, MLA_HEADS, T, MLA_ROPE))
    k = jnp.concatenate([k_nope, k_pe], axis=-1)
    q = rms_norm(q, g_qn)
    k = rms_norm(k, g_kn)
    q = jnp.concatenate([q[..., :MLA_NOPE], apply_rope(q[..., MLA_NOPE:], cos, sin)], axis=-1)
    k = jnp.concatenate([k[..., :MLA_NOPE], apply_rope(k[..., MLA_NOPE:], cos, sin)], axis=-1)
    o = blocked_dense_attention(q, k, v, MLA_QK ** -0.5)
    return o.transpose(0, 2, 1, 3).reshape(B, T, MLA_WIDTH)


def swa_mixer(q_s, k_s, v_s, g_qn, g_kn, sink, cos, sin):
    B, T, _ = q_s.shape
    q = q_s.reshape(B, T, SWA_HEADS, SWA_HEAD_DIM).transpose(0, 2, 1, 3)
    k = k_s.reshape(B, T, SWA_KV_HEADS, SWA_HEAD_DIM).transpose(0, 2, 1, 3)
    v = v_s.reshape(B, T, SWA_KV_HEADS, SWA_HEAD_DIM).transpose(0, 2, 1, 3)
    q = apply_rope(rms_norm(q, g_qn), cos, sin)
    k = apply_rope(rms_norm(k, g_kn), cos, sin)
    q = q.reshape(B, SWA_KV_HEADS, SWA_REP, T, SWA_HEAD_DIM)
    o = banded_window_attention(q, k, v, sink, SWA_HEAD_DIM ** -0.5)
    return o.reshape(B, SWA_HEADS, T, SWA_HEAD_DIM).transpose(0, 2, 1, 3).reshape(B, T, SWA_WIDTH)


def even_layer(z, w_in, g_q_lat, g_kv_lat, w_uq, w_ukv, mla_g_qn, mla_g_kn,
               swa_g_qn, swa_g_kn, sink, w_out, rope_mla, rope_swa):
    zp = jnp.einsum('btd,de->bte', z, w_in)
    c_q, c_kv, k_pe, q_s, k_s, v_s, gate = split_cols(zp, EVEN_SPLITS)
    y_a = mla_mixer(c_q, c_kv, k_pe, g_q_lat, g_kv_lat, w_uq, w_ukv,
                    mla_g_qn, mla_g_kn, rope_mla[0], rope_mla[1])
    y_b = swa_mixer(q_s, k_s, v_s, swa_g_qn, swa_g_kn, sink, rope_swa[0], rope_swa[1])
    y = jnp.concatenate([y_a, y_b], axis=-1) * jax.nn.silu(gate)
    return jnp.einsum('bte,ed->btd', y, w_out)


def lru_direction(xc, w_a, b_a, w_x, b_x, lam, reverse):
    B, T, W = xc.shape
    xb = xc.reshape(B, T, LRU_BLOCKS, LRU_BLOCK_DIM)
    r = jax.nn.sigmoid((jnp.einsum('btnd,nde->btne', xb, w_a).reshape(B, T, W) + b_a).astype(jnp.float32))
    i = jax.nn.sigmoid((jnp.einsum('btnd,nde->btne', xb, w_x).reshape(B, T, W) + b_x).astype(jnp.float32))
    log_a = -LRU_C * r * jax.nn.softplus(-lam.astype(jnp.float32))
    a = jnp.exp(log_a)
    b = jnp.sqrt(-jnp.expm1(2.0 * log_a)) * i * xc.astype(jnp.float32)

    def step(h, ab):
        a_t, b_t = ab
        h = a_t * h + b_t
        return h, h

    _, hs = lax.scan(step, jnp.zeros((B, W), jnp.float32),
                     (a.swapaxes(0, 1), b.swapaxes(0, 1)), reverse=reverse)
    return hs.swapaxes(0, 1).astype(xc.dtype)


def odd_layer(z, w_in, conv_w, conv_b, w_a, b_a, w_x, b_x, lam, w_out):
    zp = jnp.einsum('btd,de->bte', z, w_in)
    u, gate = split_cols(zp, ODD_SPLITS)
    T = u.shape[1]
    up = jnp.pad(u, ((0, 0), (CONV_PAD_LEFT, CONV_PAD_RIGHT), (0, 0)))
    xc = conv_b
    for tap in range(CONV_WIDTH):
        xc = xc + up[:, tap:tap + T] * conv_w[tap]
    y = (lru_direction(xc, w_a[0], b_a[0], w_x[0], b_x[0], lam[0], reverse=False)
         + lru_direction(xc, w_a[1], b_a[1], w_x[1], b_x[1], lam[1], reverse=True))
    y = y * jax.nn.silu(gate)
    return jnp.einsum('bte,ed->btd', y, w_out)


def setup_inputs(seed: int = 0) -> dict:
    key = jax.random.key(seed)
    ks = jax.random.split(key, 24)
    f32 = jnp.float32

    def nrm(k, shape, scale):
        return jax.random.normal(k, shape, f32) * scale

    def gain(k, shape):
        return 1.0 + 0.01 * jax.random.normal(k, shape, f32)

    u = jax.random.uniform(ks[23], (N_ODD, 2, LRU_WIDTH), f32, 0.9, 0.999)
    a0 = u ** (1.0 / LRU_C)
    lru_lambda = jnp.log(a0) - jnp.log1p(-a0)

    return {
        "x": nrm(ks[0], (BATCH, SEQ, D_MODEL), 1.0),
        "meta_tokens": nrm(ks[1], (N_META, D_MODEL), 1.0),
        "norm_g": gain(ks[2], (DEPTH, D_MODEL)),
        "even_w_in": nrm(ks[3], (N_EVEN, D_MODEL, EVEN_IN), D_MODEL ** -0.5),
        "mla_g_q_lat": gain(ks[4], (N_EVEN, MLA_Q_RANK)),
        "mla_g_kv_lat": gain(ks[5], (N_EVEN, MLA_KV_RANK)),
        "mla_w_uq": nrm(ks[6], (N_EVEN, MLA_Q_RANK, MLA_HEADS, MLA_QK), MLA_Q_RANK ** -0.5),
        "mla_w_ukv": nrm(ks[7], (N_EVEN, MLA_KV_RANK, MLA_HEADS, MLA_NOPE + MLA_V), MLA_KV_RANK ** -0.5),
        "mla_g_qn": gain(ks[8], (N_EVEN, MLA_QK)),
        "mla_g_kn": gain(ks[9], (N_EVEN, MLA_QK)),
        "swa_g_qn": gain(ks[10], (N_EVEN, SWA_HEAD_DIM)),
        "swa_g_kn": gain(ks[11], (N_EVEN, SWA_HEAD_DIM)),
        "swa_sink": nrm(ks[12], (N_EVEN, SWA_HEADS), 0.5),
        "even_w_out": nrm(ks[13], (N_EVEN, D_INNER, D_MODEL), D_INNER ** -0.5),
        "odd_w_in": nrm(ks[14], (N_ODD, D_MODEL, ODD_IN), D_MODEL ** -0.5),
        "lru_conv_w": nrm(ks[15], (N_ODD, CONV_WIDTH, LRU_WIDTH), CONV_WIDTH ** -0.5),
        "lru_conv_b": nrm(ks[16], (N_ODD, LRU_WIDTH), 0.01),
        "lru_w_a": nrm(ks[17], (N_ODD, 2, LRU_BLOCKS, LRU_BLOCK_DIM, LRU_BLOCK_DIM), LRU_BLOCK_DIM ** -0.5),
        "lru_b_a": nrm(ks[18], (N_ODD, 2, LRU_WIDTH), 0.01),
        "lru_w_x": nrm(ks[19], (N_ODD, 2, LRU_BLOCKS, LRU_BLOCK_DIM, LRU_BLOCK_DIM), LRU_BLOCK_DIM ** -0.5),
        "lru_b_x": nrm(ks[20], (N_ODD, 2, LRU_WIDTH), 0.01),
        "lru_lambda": lru_lambda,
        "odd_w_out": nrm(ks[21], (N_ODD, D_INNER, D_MODEL), D_INNER ** -0.5),
    }


def reference(x, meta_tokens, norm_g, even_w_in, mla_g_q_lat, mla_g_kv_lat, mla_w_uq, mla_w_ukv,
              mla_g_qn, mla_g_kn, swa_g_qn, swa_g_kn, swa_sink, even_w_out,
              odd_w_in, lru_conv_w, lru_conv_b, lru_w_a, lru_b_a, lru_w_x, lru_b_x,
              lru_lambda, odd_w_out):
    B = x.shape[0]
    meta = jnp.broadcast_to(meta_tokens.astype(x.dtype)[None], (B, N_META, D_MODEL))
    h = jnp.concatenate([meta, x], axis=1)
    T = h.shape[1]
    rope_mla = rope_tables(T, MLA_ROPE)
    rope_swa = rope_tables(T, SWA_HEAD_DIM)
    for l in range(DEPTH):
        z = rms_norm(h, norm_g[l])
        j = l // 2
        if l % 2 == 0:
            y = even_layer(z, even_w_in[j], mla_g_q_lat[j], mla_g_kv_lat[j], mla_w_uq[j], mla_w_ukv[j],
                           mla_g_qn[j], mla_g_kn[j], swa_g_qn[j], swa_g_kn[j], swa_sink[j],
                           even_w_out[j], rope_mla, rope_swa)
        else:
            y = odd_layer(z, odd_w_in[j], lru_conv_w[j], lru_conv_b[j], lru_w_a[j], lru_b_a[j],
                          lru_w_x[j], lru_b_x[j], lru_lambda[j], odd_w_out[j])
        h = h + y
    return h[:, N_META:]
```

```python
import functools
import math

import jax
import jax.numpy as jnp
from jax import lax
from jax.experimental import pallas as pl
from jax.experimental.pallas import tpu as pltpu

F32 = jnp.float32
BF16 = jnp.bfloat16

D_MODEL = 1024
DEPTH = 4
N_META = 16
D_INNER = 2 * D_MODEL
ROPE_THETA = 10000.0
EPS = 1e-6
NEG_INF = -1e30

MLA_HEADS = 8
MLA_Q_RANK = 384
MLA_KV_RANK = 256
MLA_NOPE = 128
MLA_ROPE = 64
MLA_QK = MLA_NOPE + MLA_ROPE
MLA_V = 128
MLA_WIDTH = MLA_HEADS * MLA_V

SWA_HEADS = 8
SWA_KV_HEADS = 2
SWA_REP = SWA_HEADS // SWA_KV_HEADS
SWA_HEAD_DIM = 128
SWA_WINDOW = 128
SWA_WIDTH = SWA_HEADS * SWA_HEAD_DIM
SWA_KV_WIDTH = SWA_KV_HEADS * SWA_HEAD_DIM

LRU_WIDTH = D_INNER
LRU_BLOCKS = 16
LRU_BLOCK_DIM = LRU_WIDTH // LRU_BLOCKS
LRU_C = 8.0
CONV_WIDTH = 4

LANE = 128
SUBLANE = 8
MXU_DIM = 256
V7X_VMEM_BYTES = 64 * 1024 * 1024

BLOCK = 128
PAD_FRONT = BLOCK - N_META
MLA_QK_PAD = MXU_DIM
LAT_COLS = 768
SWA_COLS = SWA_WIDTH + 2 * SWA_KV_WIDTH
LOG2E = 1.4426950408889634


def _vmem_limit(nbytes):
    return int(min(V7X_VMEM_BYTES - (4 << 20), max(nbytes, 16 << 20)))


def _const_spec(shape):
    nd = len(shape)
    return pl.BlockSpec(shape, lambda *_: (0,) * nd, pipeline_mode=pl.Buffered(1))


def _rms_scale(x, n):
    return lax.rsqrt(jnp.sum(x * x, axis=-1, keepdims=True) * (1.0 / n) + EPS)


def _silu(g):
    return g * jax.nn.sigmoid(g)


def _nt_dot(a, b):
    return lax.dot_general(a, b, (((1,), (1,)), ((), ())), preferred_element_type=F32)


def _even_in_kernel(h_ref, gn_ref, w_ref, gql_ref, gkvl_ref, wuq_ref, wukv_ref, gqn_ref, gkn_ref,
                    sgq_ref, sgk_ref, cm_ref, sm_ref, cs_ref, ss_ref,
                    qm_ref, km_ref, vm_ref, qs_ref, ks_ref, vs_ref, gate_ref):
    x = h_ref[...]
    z = (x * _rms_scale(x, D_MODEL) * gn_ref[...]).astype(BF16)

    lat = jnp.dot(z, w_ref[:, 0:LAT_COLS], preferred_element_type=F32)
    cq = lat[:, 0:MLA_Q_RANK]
    ckv = lat[:, MLA_Q_RANK:MLA_Q_RANK + MLA_KV_RANK]
    kpe = lat[:, MLA_Q_RANK + MLA_KV_RANK:LAT_COLS]
    cqn = (cq * _rms_scale(cq, MLA_Q_RANK) * gql_ref[...]).astype(BF16)
    ckvn = (ckv * _rms_scale(ckv, MLA_KV_RANK) * gkvl_ref[...]).astype(BF16)

    cm = cm_ref[...]
    sm = sm_ref[...]

    def rope_mla(t):
        rot = pltpu.roll(t, MLA_ROPE // 2, 1) + pltpu.roll(t, LANE - MLA_ROPE // 2, 1)
        return t * cm + rot * sm

    q_scale = (MLA_QK ** -0.5) * LOG2E
    gqn = gqn_ref[...]
    q_all = jnp.dot(cqn, wuq_ref[...], preferred_element_type=F32)
    for hd in range(MLA_HEADS):
        qh = q_all[:, hd * MLA_QK_PAD:(hd + 1) * MLA_QK_PAD]
        qn = qh * _rms_scale(qh, MLA_QK) * gqn
        q_out = jnp.concatenate([qn[:, :MLA_NOPE], rope_mla(qn[:, MLA_NOPE:])], axis=-1) * q_scale
        qm_ref[0, hd] = q_out.astype(BF16)

    gkn = gkn_ref[...]
    kv = jnp.dot(ckvn, wukv_ref[...], preferred_element_type=F32)
    ss_pe = jnp.sum(kpe * kpe, axis=-1, keepdims=True)
    kr = rope_mla(kpe * gkn[:, MLA_NOPE:])
    for hd in range(MLA_HEADS):
        kn = kv[:, hd * MLA_QK_PAD:hd * MLA_QK_PAD + MLA_NOPE]
        rs = lax.rsqrt((jnp.sum(kn * kn, axis=-1, keepdims=True) + ss_pe) * (1.0 / MLA_QK) + EPS)
        k_out = jnp.concatenate([kn * rs * gkn[:, :MLA_NOPE], kr * rs], axis=-1)
        km_ref[0, hd] = k_out.astype(BF16)
        vm_ref[0, hd] = kv[:, hd * MLA_QK_PAD + MLA_NOPE:(hd + 1) * MLA_QK_PAD].astype(BF16)

    sw = jnp.dot(z, w_ref[:, LAT_COLS:LAT_COLS + SWA_COLS], preferred_element_type=F32)
    cs = cs_ref[...]
    ss = ss_ref[...]

    def norm_rope_swa(t, g):
        n = t * _rms_scale(t, SWA_HEAD_DIM) * g
        return n * cs + pltpu.roll(n, SWA_HEAD_DIM // 2, 1) * ss

    s_scale = (SWA_HEAD_DIM ** -0.5) * LOG2E
    for hd in range(SWA_HEADS):
        t = sw[:, hd * SWA_HEAD_DIM:(hd + 1) * SWA_HEAD_DIM]
        qs_ref[0, hd] = (norm_rope_swa(t, sgq_ref[...]) * s_scale).astype(BF16)
    for hd in range(SWA_KV_HEADS):
        t = sw[:, SWA_WIDTH + hd * SWA_HEAD_DIM:SWA_WIDTH + (hd + 1) * SWA_HEAD_DIM]
        ks_ref[0, hd] = norm_rope_swa(t, sgk_ref[...]).astype(BF16)
        vs_ref[0, hd] = sw[:, SWA_WIDTH + SWA_KV_WIDTH + hd * SWA_HEAD_DIM:
                           SWA_WIDTH + SWA_KV_WIDTH + (hd + 1) * SWA_HEAD_DIM].astype(BF16)

    gate_ref[0] = jnp.dot(z, w_ref[:, LAT_COLS + SWA_COLS:], preferred_element_type=F32)


def _even_in(h2d, gn, w_all, gql, gkvl, wuq, wukv, gqn, gkn, sgq, sgk, cm, sm, cs, ss, *, batch, tp, tm):
    nt = tp // tm
    row_spec = lambda cols: pl.BlockSpec((tm, cols), lambda b, i: (i, 0))
    head_out = lambda heads, cols: pl.BlockSpec((1, heads, tm, cols), lambda b, i: (b, 0, i, 0))
    n_w = w_all.shape[1]
    est = (w_all.size * 2 + (wuq.size + wukv.size) * 2 + 2 * tm * D_MODEL * 4
           + 2 * tm * 2 * (2 * MLA_HEADS * MLA_QK_PAD + MLA_WIDTH + SWA_WIDTH + 2 * SWA_KV_WIDTH)
           + 2 * tm * D_INNER * 4 + 3 * tm * n_w * 4)
    return pl.pallas_call(
        _even_in_kernel,
        name="even_in",
        grid=(batch, nt),
        in_specs=[
            pl.BlockSpec((tm, D_MODEL), lambda b, i: (i, b)),
            _const_spec((1, D_MODEL)),
            _const_spec(w_all.shape),
            _const_spec((1, MLA_Q_RANK)),
            _const_spec((1, MLA_KV_RANK)),
            _const_spec(wuq.shape),
            _const_spec(wukv.shape),
            _const_spec((1, MLA_QK_PAD)),
            _const_spec((1, MLA_QK_PAD)),
            _const_spec((1, SWA_HEAD_DIM)),
            _const_spec((1, SWA_HEAD_DIM)),
            row_spec(LANE), row_spec(LANE), row_spec(LANE), row_spec(LANE),
        ],
        out_specs=[
            head_out(MLA_HEADS, MLA_QK_PAD),
            head_out(MLA_HEADS, MLA_QK_PAD),
            head_out(MLA_HEADS, MLA_V),
            head_out(SWA_HEADS, SWA_HEAD_DIM),
            head_out(SWA_KV_HEADS, SWA_HEAD_DIM),
            head_out(SWA_KV_HEADS, SWA_HEAD_DIM),
            pl.BlockSpec((1, tm, D_INNER), lambda b, i: (b, i, 0)),
        ],
        out_shape=[
            jax.ShapeDtypeStruct((batch, MLA_HEADS, tp, MLA_QK_PAD), BF16),
            jax.ShapeDtypeStruct((batch, MLA_HEADS, tp, MLA_QK_PAD), BF16),
            jax.ShapeDtypeStruct((batch, MLA_HEADS, tp, MLA_V), BF16),
            jax.ShapeDtypeStruct((batch, SWA_HEADS, tp, SWA_HEAD_DIM), BF16),
            jax.ShapeDtypeStruct((batch, SWA_KV_HEADS, tp, SWA_HEAD_DIM), BF16),
            jax.ShapeDtypeStruct((batch, SWA_KV_HEADS, tp, SWA_HEAD_DIM), BF16),
            jax.ShapeDtypeStruct((batch, tp, D_INNER), F32),
        ],
        compiler_params=pltpu.CompilerParams(
            dimension_semantics=("arbitrary", "arbitrary"), vmem_limit_bytes=_vmem_limit(est)),
    )(h2d, gn, w_all, gql, gkvl, wuq, wukv, gqn, gkn, sgq, sgk, cm, sm, cs, ss)


def _mla_attn_kernel(q_ref, k_ref, v_ref, gate_ref, o_ref, *, tp, tkc):
    q = q_ref[0, 0]
    tq = q.shape[0]

    s = _nt_dot(q, k_ref[0, 0, 0:BLOCK, :])
    col = lax.broadcasted_iota(jnp.int32, (tq, BLOCK), 1)
    s = jnp.where(col >= PAD_FRONT, s, NEG_INF)
    m = jnp.max(s, axis=-1, keepdims=True)
    p = jnp.exp2(s - m)
    l = jnp.sum(p, axis=-1, keepdims=True)
    acc = jnp.dot(p.astype(BF16), v_ref[0, 0, 0:BLOCK, :], preferred_element_type=F32)

    def chunk(c, carry):
        m, l, acc = carry
        start = pl.multiple_of(BLOCK + c * tkc, BLOCK)
        s = _nt_dot(q, k_ref[0, 0, pl.ds(start, tkc), :])
        m_new = jnp.maximum(m, jnp.max(s, axis=-1, keepdims=True))
        alpha = jnp.exp2(m - m_new)
        p = jnp.exp2(s - m_new)
        l = alpha * l + jnp.sum(p, axis=-1, keepdims=True)
        acc = alpha * acc + jnp.dot(p.astype(BF16), v_ref[0, 0, pl.ds(start, tkc), :],
                                    preferred_element_type=F32)
        return m_new, l, acc

    m, l, acc = lax.fori_loop(0, (tp - BLOCK) // tkc, chunk, (m, l, acc))
    o_ref[0] = ((acc / l) * _silu(gate_ref[0])).astype(BF16)


def _mla_attn(qm, km, vm, gate, *, batch, tp, tq, tkc):
    nq = tp // tq
    est = 2 * tp * (MLA_QK_PAD + MLA_V) * 2 + 6 * tq * tkc * 4 + 4 * tq * MLA_QK_PAD * 2
    return pl.pallas_call(
        functools.partial(_mla_attn_kernel, tp=tp, tkc=tkc),
        name="mla_attn",
        grid=(batch, MLA_HEADS, nq),
        in_specs=[
            pl.BlockSpec((1, 1, tq, MLA_QK_PAD), lambda b, h, i: (b, h, i, 0)),
            pl.BlockSpec((1, 1, tp, MLA_QK_PAD), lambda b, h, i: (b, h, 0, 0)),
            pl.BlockSpec((1, 1, tp, MLA_V), lambda b, h, i: (b, h, 0, 0)),
            pl.BlockSpec((1, tq, MLA_V), lambda b, h, i: (b, i, h)),
        ],
        out_specs=pl.BlockSpec((1, tq, MLA_V), lambda b, h, i: (b, i, h)),
        out_shape=jax.ShapeDtypeStruct((batch, tp, MLA_WIDTH), BF16),
        compiler_params=pltpu.CompilerParams(
            dimension_semantics=("arbitrary", "arbitrary", "arbitrary"), vmem_limit_bytes=_vmem_limit(est)),
    )(qm, km, vm, gate)


def _swa_attn_kernel(q_ref, k_ref, v_ref, sink_ref, gate_ref, o_ref, *, tp):
    n = pl.program_id(2)
    rows = SWA_REP * BLOCK
    band = 3 * BLOCK
    q = q_ref[0].reshape(rows, SWA_HEAD_DIM)
    start = pl.multiple_of(jnp.clip((n - 1) * BLOCK, 0, tp - band), BLOCK)
    s_b = _nt_dot(q, k_ref[0, 0, pl.ds(start, band), :])
    s_m = _nt_dot(q, k_ref[0, 0, 0:BLOCK, :])

    rq = n * BLOCK + (lax.broadcasted_iota(jnp.int32, (rows, band), 0) & (BLOCK - 1))
    rk = start + lax.broadcasted_iota(jnp.int32, (rows, band), 1)
    s_b = jnp.where(jnp.abs(rq - rk) <= SWA_WINDOW, jnp.where(rk >= BLOCK, s_b, NEG_INF), NEG_INF)
    col_m = lax.broadcasted_iota(jnp.int32, (rows, BLOCK), 1)
    s_m = jnp.where(col_m >= PAD_FRONT, s_m, NEG_INF)

    sink = sink_ref[0]
    m = jnp.maximum(jnp.maximum(jnp.max(s_b, axis=-1, keepdims=True), jnp.max(s_m, axis=-1, keepdims=True)),
                    sink)
    p_b = jnp.exp2(s_b - m)
    p_m = jnp.exp2(s_m - m)
    l = jnp.exp2(sink - m) + jnp.sum(p_b, axis=-1, keepdims=True) + jnp.sum(p_m, axis=-1, keepdims=True)
    o = (jnp.dot(p_b.astype(BF16), v_ref[0, 0, pl.ds(start, band), :], preferred_element_type=F32)
         + jnp.dot(p_m.astype(BF16), v_ref[0, 0, 0:BLOCK, :], preferred_element_type=F32)) / l
    g = _silu(gate_ref[0])
    for r in range(SWA_REP):
        o_ref[0, :, r * SWA_HEAD_DIM:(r + 1) * SWA_HEAD_DIM] = (
            o[r * BLOCK:(r + 1) * BLOCK] * g[:, r * SWA_HEAD_DIM:(r + 1) * SWA_HEAD_DIM]).astype(BF16)


def _swa_attn(qs, ks, vs, sink_rows, gate, *, batch, tp):
    nb = tp // BLOCK
    gcols = SWA_REP * SWA_HEAD_DIM
    est = 4 * tp * SWA_HEAD_DIM * 2 + 12 * SWA_REP * BLOCK * 4 * BLOCK * 4
    return pl.pallas_call(
        functools.partial(_swa_attn_kernel, tp=tp),
        name="swa_attn",
        grid=(batch, SWA_KV_HEADS, nb),
        in_specs=[
            pl.BlockSpec((1, SWA_REP, BLOCK, SWA_HEAD_DIM), lambda b, g, n: (b, g, n, 0)),
            pl.BlockSpec((1, 1, tp, SWA_HEAD_DIM), lambda b, g, n: (b, g, 0, 0)),
            pl.BlockSpec((1, 1, tp, SWA_HEAD_DIM), lambda b, g, n: (b, g, 0, 0)),
            pl.BlockSpec((1, SWA_REP * BLOCK, 1), lambda b, g, n: (g, 0, 0)),
            pl.BlockSpec((1, BLOCK, gcols), lambda b, g, n: (b, n, MLA_WIDTH // gcols + g)),
        ],
        out_specs=pl.BlockSpec((1, BLOCK, gcols), lambda b, g, n: (b, n, g)),
        out_shape=jax.ShapeDtypeStruct((batch, tp, SWA_WIDTH), BF16),
        compiler_params=pltpu.CompilerParams(
            dimension_semantics=("arbitrary", "arbitrary", "arbitrary"), vmem_limit_bytes=_vmem_limit(est)),
    )(qs, ks, vs, sink_rows, gate)


def _even_out_kernel(ya_ref, yb_ref, w_ref, h_ref, o_ref, *, tm):
    i = pl.program_id(1)
    d = (jnp.dot(ya_ref[0], w_ref[0:MLA_WIDTH, :], preferred_element_type=F32)
         + jnp.dot(yb_ref[0], w_ref[MLA_WIDTH:, :], preferred_element_type=F32))
    row = i * tm + lax.broadcasted_iota(jnp.int32, (tm, D_MODEL), 0)
    o_ref[...] = h_ref[...] + jnp.where(row >= PAD_FRONT, d, 0.0)


def _even_out(ya, yb, w_out, h2d, *, batch, tp, tm):
    nt = tp // tm
    est = w_out.size * 2 + 4 * tm * D_MODEL * 2 + 6 * tm * D_MODEL * 4
    return pl.pallas_call(
        functools.partial(_even_out_kernel, tm=tm),
        name="even_out",
        grid=(batch, nt),
        in_specs=[
            pl.BlockSpec((1, tm, MLA_WIDTH), lambda b, i: (b, i, 0)),
            pl.BlockSpec((1, tm, SWA_WIDTH), lambda b, i: (b, i, 0)),
            _const_spec(w_out.shape),
            pl.BlockSpec((tm, D_MODEL), lambda b, i: (i, b)),
        ],
        out_specs=pl.BlockSpec((tm, D_MODEL), lambda b, i: (i, b)),
        out_shape=jax.ShapeDtypeStruct(h2d.shape, F32),
        compiler_params=pltpu.CompilerParams(
            dimension_semantics=("arbitrary", "arbitrary"), vmem_limit_bytes=_vmem_limit(est)),
    )(ya, yb, w_out, h2d)


def _lru_coeffs(xc, wg_ref, bg_ref, lam_ref, a_ref, b_ref, t0):
    tmr = xc.shape[0]
    t_idx = t0 + (lax.broadcasted_iota(jnp.int32, (tmr, LRU_BLOCK_DIM), 0) // SUBLANE)
    live = t_idx >= PAD_FRONT
    for n in range(LRU_BLOCKS):
        sl = slice(n * LRU_BLOCK_DIM, (n + 1) * LRU_BLOCK_DIM)
        xn = xc[:, sl]
        gg = jnp.dot(xn.astype(BF16), wg_ref[n], preferred_element_type=F32) + bg_ref[n]
        r = jax.nn.sigmoid(gg[:, :LRU_BLOCK_DIM])
        i = jax.nn.sigmoid(gg[:, LRU_BLOCK_DIM:])
        log_a = (-LRU_C) * r * jax.nn.softplus(-lam_ref[:, sl])
        a = jnp.exp(log_a)
        a_ref[:, sl] = a
        one_m_a2 = jnp.tanh(-log_a) * (a * a + 1.0)
        b_ref[:, sl] = jnp.where(live, jnp.sqrt(one_m_a2) * i * xn, 0.0)


def _odd_fwd_kernel(h_ref, halo_ref, gn_ref, w_ref, cw_ref, cb_ref, wg_ref, bg_ref, lam_ref,
                    xc_ref, gate_ref, hf_ref,
                    uprev_ref, hst_ref, a_ref, b_ref, *, tc, nt):
    j = pl.program_id(0)
    tmr = tc * SUBLANE
    halo_rows = SUBLANE
    prev_rows = 2 * SUBLANE

    @pl.when(j == 0)
    def _():
        uprev_ref[...] = jnp.zeros_like(uprev_ref)
        hst_ref[...] = jnp.zeros_like(hst_ref)

    x = jnp.concatenate([h_ref[...].reshape(tmr, D_MODEL), halo_ref[...].reshape(halo_rows, D_MODEL)], axis=0)
    z = (x * _rms_scale(x, D_MODEL) * gn_ref[...]).astype(BF16)
    u = jnp.dot(z, w_ref[:, 0:LRU_WIDTH], preferred_element_type=F32)
    gate_ref[...] = jnp.dot(z[0:tmr], w_ref[:, LRU_WIDTH:], preferred_element_type=F32).reshape(
        tc, SUBLANE, D_INNER)

    u_next = jnp.where(j == nt - 1, 0.0, u[tmr:])
    ue = jnp.concatenate([uprev_ref[...], u[0:tmr], u_next], axis=0)
    uprev_ref[...] = u[tmr - prev_rows:tmr]
    xc = cb_ref[...] + ue[0:tmr] * cw_ref[0:1, :]
    for tap in range(1, CONV_WIDTH):
        xc = xc + ue[tap * SUBLANE:tap * SUBLANE + tmr] * cw_ref[tap:tap + 1, :]
    xc_ref[...] = xc.reshape(tc, SUBLANE, LRU_WIDTH)

    _lru_coeffs(xc, wg_ref, bg_ref, lam_ref, a_ref, b_ref, j * tc)

    def step(t, hs):
        r = pl.multiple_of(t * SUBLANE, SUBLANE)
        hs = a_ref[pl.ds(r, SUBLANE), :] * hs + b_ref[pl.ds(r, SUBLANE), :]
        hf_ref[t] = hs
        return hs

    hst_ref[...] = lax.fori_loop(0, tc, step, hst_ref[...], unroll=8)


def _odd_fwd(h, gn, w_in, cw, cb, wg, bg, lam, *, batch, tp, tc):
    nt = tp // tc
    tmr = tc * batch
    blk = lambda cols: pl.BlockSpec((tc, batch, cols), lambda j: (j, 0, 0))
    est = (w_in.size * 2 + 2 * tmr * D_MODEL * 4 + 3 * 2 * tmr * LRU_WIDTH * 4
           + 2 * tmr * LRU_WIDTH * 4 + 5 * tmr * LRU_WIDTH * 4)
    return pl.pallas_call(
        functools.partial(_odd_fwd_kernel, tc=tc, nt=nt),
        name="odd_fwd",
        grid=(nt,),
        in_specs=[
            blk(D_MODEL),
            pl.BlockSpec((1, batch, D_MODEL), lambda j: (jnp.minimum((j + 1) * tc, tp - 1), 0, 0)),
            _const_spec((1, D_MODEL)),
            _const_spec(w_in.shape),
            _const_spec(cw.shape),
            _const_spec(cb.shape),
            _const_spec(wg.shape),
            _const_spec(bg.shape),
            _const_spec(lam.shape),
        ],
        out_specs=[blk(LRU_WIDTH), blk(D_INNER), blk(LRU_WIDTH)],
        out_shape=[jax.ShapeDtypeStruct((tp, batch, LRU_WIDTH), F32),
                   jax.ShapeDtypeStruct((tp, batch, D_INNER), F32),
                   jax.ShapeDtypeStruct((tp, batch, LRU_WIDTH), F32)],
        scratch_shapes=[pltpu.VMEM((2 * SUBLANE, LRU_WIDTH), F32),
                        pltpu.VMEM((SUBLANE, LRU_WIDTH), F32),
                        pltpu.VMEM((tmr, LRU_WIDTH), F32),
                        pltpu.VMEM((tmr, LRU_WIDTH), F32)],
        compiler_params=pltpu.CompilerParams(
            dimension_semantics=("arbitrary",), vmem_limit_bytes=_vmem_limit(est)),
    )(h, h, gn, w_in, cw, cb, wg, bg, lam)


def _odd_bwd_kernel(xc_ref, gate_ref, hf_ref, h_ref, wg_ref, bg_ref, lam_ref, w_ref, o_ref,
                    hst_ref, a_ref, b_ref, y_ref, *, tc, nt):
    j = pl.program_id(0)
    jj = nt - 1 - j
    tmr = tc * SUBLANE

    @pl.when(j == 0)
    def _():
        hst_ref[...] = jnp.zeros_like(hst_ref)

    xc = xc_ref[...].reshape(tmr, LRU_WIDTH)
    _lru_coeffs(xc, wg_ref, bg_ref, lam_ref, a_ref, b_ref, jj * tc)

    def step(s, hs):
        t = tc - 1 - s
        r = pl.multiple_of(t * SUBLANE, SUBLANE)
        hs = a_ref[pl.ds(r, SUBLANE), :] * hs + b_ref[pl.ds(r, SUBLANE), :]
        y_ref[pl.ds(r, SUBLANE), :] = hf_ref[t] + hs
        return hs

    hst_ref[...] = lax.fori_loop(0, tc, step, hst_ref[...], unroll=8)

    y = (y_ref[...] * _silu(gate_ref[...].reshape(tmr, D_INNER))).astype(BF16)
    d = jnp.dot(y, w_ref[...], preferred_element_type=F32)
    t_idx = jj * tc + (lax.broadcasted_iota(jnp.int32, (tmr, D_MODEL), 0) // SUBLANE)
    o_ref[...] = h_ref[...] + jnp.where(t_idx >= PAD_FRONT, d, 0.0).reshape(tc, SUBLANE, D_MODEL)


def _odd_bwd(xc, gate, hf, h, wg, bg, lam, w_out, *, batch, tp, tc):
    nt = tp // tc
    tmr = tc * batch
    blk = lambda cols: pl.BlockSpec((tc, batch, cols), lambda j: (nt - 1 - j, 0, 0))
    est = (w_out.size * 2 + 3 * 2 * tmr * LRU_WIDTH * 4 + 4 * tmr * D_MODEL * 4
           + 3 * tmr * LRU_WIDTH * 4 + 4 * tmr * LRU_WIDTH * 4)
    return pl.pallas_call(
        functools.partial(_odd_bwd_kernel, tc=tc, nt=nt),
        name="odd_bwd",
        grid=(nt,),
        in_specs=[
            blk(LRU_WIDTH), blk(D_INNER), blk(LRU_WIDTH), blk(D_MODEL),
            _const_spec(wg.shape),
            _const_spec(bg.shape),
            _const_spec(lam.shape),
            _const_spec(w_out.shape),
        ],
        out_specs=blk(D_MODEL),
        out_shape=jax.ShapeDtypeStruct((tp, batch, D_MODEL), F32),
        scratch_shapes=[pltpu.VMEM((SUBLANE, LRU_WIDTH), F32),
                        pltpu.VMEM((tmr, LRU_WIDTH), F32),
                        pltpu.VMEM((tmr, LRU_WIDTH), F32),
                        pltpu.VMEM((tmr, LRU_WIDTH), F32)],
        compiler_params=pltpu.CompilerParams(
            dimension_semantics=("arbitrary",), vmem_limit_bytes=_vmem_limit(est)),
    )(xc, gate, hf, h, wg, bg, lam, w_out)


def _rope_tables(tp):
    pos = (jnp.arange(tp, dtype=F32) - PAD_FRONT)[:, None]

    def table(dim):
        inv = ROPE_THETA ** (-jnp.arange(0, dim, 2, dtype=F32) / dim)
        ang = pos * inv[None, :]
        return jnp.cos(ang), jnp.sin(ang)

    c, s = table(MLA_ROPE)
    zeros = jnp.zeros((tp, LANE - MLA_ROPE), F32)
    cm = jnp.concatenate([c, c, zeros], axis=1)
    sm = jnp.concatenate([-s, s, zeros], axis=1)
    c, s = table(SWA_HEAD_DIM)
    cs = jnp.concatenate([c, c], axis=1)
    ss = jnp.concatenate([-s, s], axis=1)
    return cm, sm, cs, ss


def _prep_even(w_in, w_uq, w_ukv, g_qn, g_kn):
    lat_w = MLA_Q_RANK + MLA_KV_RANK + MLA_ROPE
    w_all = jnp.concatenate(
        [w_in[:, :lat_w], jnp.zeros((D_MODEL, LAT_COLS - lat_w), w_in.dtype), w_in[:, lat_w:]], axis=1).astype(BF16)
    wuq = jnp.pad(w_uq, ((0, 0), (0, 0), (0, MLA_QK_PAD - MLA_QK))).reshape(
        MLA_Q_RANK, MLA_HEADS * MLA_QK_PAD).astype(BF16)
    wukv = w_ukv.reshape(MLA_KV_RANK, MLA_HEADS * (MLA_NOPE + MLA_V)).astype(BF16)
    pad_g = lambda g: jnp.pad(g, (0, MLA_QK_PAD - MLA_QK)).reshape(1, MLA_QK_PAD)
    return w_all, wuq, wukv, pad_g(g_qn), pad_g(g_kn)


def _prep_gates(w_a, b_a, w_x, b_x):
    wg = jnp.concatenate([w_a, w_x], axis=-1).astype(BF16)
    bg = jnp.concatenate([b_a.reshape(LRU_BLOCKS, 1, LRU_BLOCK_DIM), b_x.reshape(LRU_BLOCKS, 1, LRU_BLOCK_DIM)],
                         axis=-1)
    return wg, bg


def _tile(tp, prefs):
    for t in prefs:
        if tp % t == 0:
            return t
    raise ValueError(f"no tile of {prefs} divides {tp}")


def kernel(x, meta_tokens, norm_g, even_w_in, mla_g_q_lat, mla_g_kv_lat, mla_w_uq, mla_w_ukv, mla_g_qn, mla_g_kn, swa_g_qn, swa_g_kn, swa_sink, even_w_out, odd_w_in, lru_conv_w, lru_conv_b, lru_w_a, lru_b_a, lru_w_x, lru_b_x, lru_lambda, odd_w_out):
    batch, seq, d = x.shape
    assert d == D_MODEL and batch == SUBLANE and seq % BLOCK == 0
    tp = BLOCK + seq
    tm = _tile(tp, (384, 128))
    tkc = _tile(seq, (512, 256, 128))
    tc = _tile(tp, (32, 16))

    meta = jnp.broadcast_to(meta_tokens.astype(x.dtype)[None], (batch, N_META, D_MODEL))
    h = jnp.concatenate([jnp.zeros((batch, PAD_FRONT, D_MODEL), x.dtype), meta, x], axis=1)
    h = h.transpose(1, 0, 2)
    cm, sm, cs, ss = _rope_tables(tp)

    for l in range(DEPTH):
        j = l // 2
        gn = norm_g[l].reshape(1, D_MODEL)
        if l % 2 == 0:
            w_all, wuq, wukv, gqn, gkn = _prep_even(even_w_in[j], mla_w_uq[j], mla_w_ukv[j], mla_g_qn[j],
                                                    mla_g_kn[j])
            h2d = h.reshape(tp, batch * D_MODEL)
            qm, km, vm, qs, ks, vs, gate = _even_in(
                h2d, gn, w_all, mla_g_q_lat[j].reshape(1, -1), mla_g_kv_lat[j].reshape(1, -1), wuq, wukv,
                gqn, gkn, swa_g_qn[j].reshape(1, -1), swa_g_kn[j].reshape(1, -1), cm, sm, cs, ss,
                batch=batch, tp=tp, tm=tm)
            ya = _mla_attn(qm, km, vm, gate, batch=batch, tp=tp, tq=tm, tkc=tkc)
            sink_rows = jnp.repeat(swa_sink[j].astype(F32) * LOG2E, BLOCK).reshape(
                SWA_KV_HEADS, SWA_REP * BLOCK, 1)
            yb = _swa_attn(qs, ks, vs, sink_rows, gate, batch=batch, tp=tp)
            h2d = _even_out(ya, yb, even_w_out[j].astype(BF16), h2d, batch=batch, tp=tp, tm=tm)
            h = h2d.reshape(tp, batch, D_MODEL)
        else:
            wgf, bgf = _prep_gates(lru_w_a[j, 0], lru_b_a[j, 0], lru_w_x[j, 0], lru_b_x[j, 0])
            wgb, bgb = _prep_gates(lru_w_a[j, 1], lru_b_a[j, 1], lru_w_x[j, 1], lru_b_x[j, 1])
            xc, gate, hf = _odd_fwd(h, gn, odd_w_in[j].astype(BF16), lru_conv_w[j], lru_conv_b[j].reshape(1, -1),
                                    wgf, bgf, lru_lambda[j, 0].reshape(1, -1), batch=batch, tp=tp, tc=tc)
            h = _odd_bwd(xc, gate, hf, h, wgb, bgb, lru_lambda[j, 1].reshape(1, -1), odd_w_out[j].astype(BF16),
                         batch=batch, tp=tp, tc=tc)
    return h[BLOCK:].transpose(1, 0, 2)
```

```python
import functools
import math

import jax
import jax.numpy as jnp
from jax import lax
from jax.experimental import pallas as pl
from jax.experimental.pallas import tpu as pltpu

F32 = jnp.float32
BF16 = jnp.bfloat16

D_MODEL = 1024
DEPTH = 4
N_META = 16
D_INNER = 2 * D_MODEL
ROPE_THETA = 10000.0
EPS = 1e-6
NEG_INF = -1e30

MLA_HEADS = 8
MLA_Q_RANK = 384
MLA_KV_RANK = 256
MLA_NOPE = 128
MLA_ROPE = 64
MLA_QK = MLA_NOPE + MLA_ROPE
MLA_V = 128
MLA_WIDTH = MLA_HEADS * MLA_V

SWA_HEADS = 8
SWA_KV_HEADS = 2
SWA_REP = SWA_HEADS // SWA_KV_HEADS
SWA_HEAD_DIM = 128
SWA_WINDOW = 128
SWA_WIDTH = SWA_HEADS * SWA_HEAD_DIM
SWA_KV_WIDTH = SWA_KV_HEADS * SWA_HEAD_DIM

LRU_WIDTH = D_INNER
LRU_BLOCKS = 16
LRU_BLOCK_DIM = LRU_WIDTH // LRU_BLOCKS
LRU_C = 8.0
CONV_WIDTH = 4

LANE = 128
SUBLANE = 8
MXU_DIM = 256
V7X_VMEM_BYTES = 64 * 1024 * 1024

BLOCK = 128
PAD_FRONT = BLOCK - N_META
MLA_QK_PAD = MXU_DIM
LAT_COLS = 768
SWA_COLS = SWA_WIDTH + 2 * SWA_KV_WIDTH
LOG2E = 1.4426950408889634


def _vmem_limit(nbytes):
    return int(min(V7X_VMEM_BYTES - (4 << 20), max(nbytes, 16 << 20)))


def _const_spec(shape):
    nd = len(shape)
    return pl.BlockSpec(shape, lambda *_: (0,) * nd, pipeline_mode=pl.Buffered(1))


def _rms_scale(x, n):
    return lax.rsqrt(jnp.sum(x * x, axis=-1, keepdims=True) * (1.0 / n) + EPS)


def _silu(g):
    return g * jax.nn.sigmoid(g)


def _nt_dot(a, b):
    return lax.dot_general(a, b, (((1,), (1,)), ((), ())), preferred_element_type=F32)


def _even_in_kernel(h_ref, gn_ref, w_ref, gql_ref, gkvl_ref, wuq_ref, wukv_ref, gqn_ref, gkn_ref,
                    sgq_ref, sgk_ref, cm_ref, sm_ref, cs_ref, ss_ref,
                    qm_ref, km_ref, vm_ref, qs_ref, ks_ref, vs_ref, gate_ref):
    x = h_ref[...]
    z = (x * _rms_scale(x, D_MODEL) * gn_ref[...]).astype(BF16)

    lat = jnp.dot(z, w_ref[:, 0:LAT_COLS], preferred_element_type=F32)
    cq = lat[:, 0:MLA_Q_RANK]
    ckv = lat[:, MLA_Q_RANK:MLA_Q_RANK + MLA_KV_RANK]
    kpe = lat[:, MLA_Q_RANK + MLA_KV_RANK:LAT_COLS]
    cqn = (cq * _rms_scale(cq, MLA_Q_RANK) * gql_ref[...]).astype(BF16)
    ckvn = (ckv * _rms_scale(ckv, MLA_KV_RANK) * gkvl_ref[...]).astype(BF16)

    cm = cm_ref[...]
    sm = sm_ref[...]

    def rope_mla(t):
        rot = pltpu.roll(t, MLA_ROPE // 2, 1) + pltpu.roll(t, LANE - MLA_ROPE // 2, 1)
        return t * cm + rot * sm

    q_scale = (MLA_QK ** -0.5) * LOG2E
    gqn = gqn_ref[...]
    q_all = jnp.dot(cqn, wuq_ref[...], preferred_element_type=F32)
    for hd in range(MLA_HEADS):
        qh = q_all[:, hd * MLA_QK_PAD:(hd + 1) * MLA_QK_PAD]
        qn = qh * _rms_scale(qh, MLA_QK) * gqn
        q_out = jnp.concatenate([qn[:, :MLA_NOPE], rope_mla(qn[:, MLA_NOPE:])], axis=-1) * q_scale
        qm_ref[0, hd] = q_out.astype(BF16)

    gkn = gkn_ref[...]
    kv = jnp.dot(ckvn, wukv_ref[...], preferred_element_type=F32)
    ss_pe = jnp.sum(kpe * kpe, axis=-1, keepdims=True)
    kr = rope_mla(kpe * gkn[:, MLA_NOPE:])
    for hd in range(MLA_HEADS):
        kn = kv[:, hd * MLA_QK_PAD:hd * MLA_QK_PAD + MLA_NOPE]
        rs = lax.rsqrt((jnp.sum(kn * kn, axis=-1, keepdims=True) + ss_pe) * (1.0 / MLA_QK) + EPS)
        k_out = jnp.concatenate([kn * rs * gkn[:, :MLA_NOPE], kr * rs], axis=-1)
        km_ref[0, hd] = k_out.astype(BF16)
        vm_ref[0, hd] = kv[:, hd * MLA_QK_PAD + MLA_NOPE:(hd + 1) * MLA_QK_PAD].astype(BF16)

    sw = jnp.dot(z, w_ref[:, LAT_COLS:LAT_COLS + SWA_COLS], preferred_element_type=F32)
    cs = cs_ref[...]
    ss = ss_ref[...]

    def norm_rope_swa(t, g):
        n = t * _rms_scale(t, SWA_HEAD_DIM) * g
        return n * cs + pltpu.roll(n, SWA_HEAD_DIM // 2, 1) * ss

    s_scale = (SWA_HEAD_DIM ** -0.5) * LOG2E
    for hd in range(SWA_HEADS):
        t = sw[:, hd * SWA_HEAD_DIM:(hd + 1) * SWA_HEAD_DIM]
        qs_ref[0, hd] = (norm_rope_swa(t, sgq_ref[...]) * s_scale).astype(BF16)
    for hd in range(SWA_KV_HEADS):
        t = sw[:, SWA_WIDTH + hd * SWA_HEAD_DIM:SWA_WIDTH + (hd + 1) * SWA_HEAD_DIM]
        ks_ref[0, hd] = norm_rope_swa(t, sgk_ref[...]).astype(BF16)
        vs_ref[0, hd] = sw[:, SWA_WIDTH + SWA_KV_WIDTH + hd * SWA_HEAD_DIM:
                           SWA_WIDTH + SWA_KV_WIDTH + (hd + 1) * SWA_HEAD_DIM].astype(BF16)

    gate_ref[0] = jnp.dot(z, w_ref[:, LAT_COLS + SWA_COLS:], preferred_element_type=F32)


def _even_in(h2d, gn, w_all, gql, gkvl, wuq, wukv, gqn, gkn, sgq, sgk, cm, sm, cs, ss, *, batch, tp, tm):
    nt = tp // tm
    row_spec = lambda cols: pl.BlockSpec((tm, cols), lambda b, i: (i, 0))
    head_out = lambda heads, cols: pl.BlockSpec((1, heads, tm, cols), lambda b, i: (b, 0, i, 0))
    n_w = w_all.shape[1]
    est = (w_all.size * 2 + (wuq.size + wukv.size) * 2 + 2 * tm * D_MODEL * 4
           + 2 * tm * 2 * (2 * MLA_HEADS * MLA_QK_PAD + MLA_WIDTH + SWA_WIDTH + 2 * SWA_KV_WIDTH)
           + 2 * tm * D_INNER * 4 + 3 * tm * n_w * 4)
    return pl.pallas_call(
        _even_in_kernel,
        name="even_in",
        grid=(batch, nt),
        in_specs=[
            pl.BlockSpec((tm, D_MODEL), lambda b, i: (i, b)),
            _const_spec((1, D_MODEL)),
            _const_spec(w_all.shape),
            _const_spec((1, MLA_Q_RANK)),
            _const_spec((1, MLA_KV_RANK)),
            _const_spec(wuq.shape),
            _const_spec(wukv.shape),
            _const_spec((1, MLA_QK_PAD)),
            _const_spec((1, MLA_QK_PAD)),
            _const_spec((1, SWA_HEAD_DIM)),
            _const_spec((1, SWA_HEAD_DIM)),
            row_spec(LANE), row_spec(LANE), row_spec(LANE), row_spec(LANE),
        ],
        out_specs=[
            head_out(MLA_HEADS, MLA_QK_PAD),
            head_out(MLA_HEADS, MLA_QK_PAD),
            head_out(MLA_HEADS, MLA_V),
            head_out(SWA_HEADS, SWA_HEAD_DIM),
            head_out(SWA_KV_HEADS, SWA_HEAD_DIM),
            head_out(SWA_KV_HEADS, SWA_HEAD_DIM),
            pl.BlockSpec((1, tm, D_INNER), lambda b, i: (b, i, 0)),
        ],
        out_shape=[
            jax.ShapeDtypeStruct((batch, MLA_HEADS, tp, MLA_QK_PAD), BF16),
            jax.ShapeDtypeStruct((batch, MLA_HEADS, tp, MLA_QK_PAD), BF16),
            jax.ShapeDtypeStruct((batch, MLA_HEADS, tp, MLA_V), BF16),
            jax.ShapeDtypeStruct((batch, SWA_HEADS, tp, SWA_HEAD_DIM), BF16),
            jax.ShapeDtypeStruct((batch, SWA_KV_HEADS, tp, SWA_HEAD_DIM), BF16),
            jax.ShapeDtypeStruct((batch, SWA_KV_HEADS, tp, SWA_HEAD_DIM), BF16),
            jax.ShapeDtypeStruct((batch, tp, D_INNER), F32),
        ],
        compiler_params=pltpu.CompilerParams(
            dimension_semantics=("arbitrary", "arbitrary"), vmem_limit_bytes=_vmem_limit(est)),
    )(h2d, gn, w_all, gql, gkvl, wuq, wukv, gqn, gkn, sgq, sgk, cm, sm, cs, ss)


def _mla_attn_kernel(q_ref, k_ref, v_ref, gate_ref, o_ref, *, tp, tkc):
    q = q_ref[0, 0]
    tq = q.shape[0]

    s = _nt_dot(q, k_ref[0, 0, 0:BLOCK, :])
    col = lax.broadcasted_iota(jnp.int32, (tq, BLOCK), 1)
    s = jnp.where(col >= PAD_FRONT, s, NEG_INF)
    m = jnp.max(s, axis=-1, keepdims=True)
    p = jnp.exp2(s - m)
    l = jnp.sum(p, axis=-1, keepdims=True)
    acc = jnp.dot(p.astype(BF16), v_ref[0, 0, 0:BLOCK, :], preferred_element_type=F32)

    n_chunks = (tp - BLOCK) // tkc

    def scores(c):
        start = pl.multiple_of(BLOCK + c * tkc, BLOCK)
        return _nt_dot(q, k_ref[0, 0, pl.ds(start, tkc), :])

    def softmax_step(s, m, l):
        m_new = jnp.maximum(m, jnp.max(s, axis=-1, keepdims=True))
        alpha = jnp.exp2(m - m_new)
        p = jnp.exp2(s - m_new)
        l = alpha * l + jnp.sum(p, axis=-1, keepdims=True)
        return p.astype(BF16), alpha, m_new, l

    def pv_step(c, p, alpha, acc):
        start = pl.multiple_of(BLOCK + c * tkc, BLOCK)
        return alpha * acc + jnp.dot(p, v_ref[0, 0, pl.ds(start, tkc), :], preferred_element_type=F32)

    def chunk(c, carry):
        s, p_prev, alpha_prev, m, l, acc = carry
        s_next = scores(c + 1)
        p, alpha, m, l = softmax_step(s, m, l)
        acc = pv_step(c - 1, p_prev, alpha_prev, acc)
        return s_next, p, alpha, m, l, acc

    s1 = scores(1)
    p0, alpha0, m, l = softmax_step(scores(0), m, l)
    carry = (s1, p0, alpha0, m, l, acc)
    for c in range(1, n_chunks - 1):
        carry = chunk(c, carry)
    s, p_prev, alpha_prev, m, l, acc = carry
    p, alpha, m, l = softmax_step(s, m, l)
    acc = pv_step(n_chunks - 2, p_prev, alpha_prev, acc)
    acc = pv_step(n_chunks - 1, p, alpha, acc)
    o_ref[0] = ((acc / l) * _silu(gate_ref[0])).astype(BF16)


def _mla_attn(qm, km, vm, gate, *, batch, tp, tq, tkc):
    nq = tp // tq
    est = 2 * tp * (MLA_QK_PAD + MLA_V) * 2 + 6 * tq * tkc * 4 + 4 * tq * MLA_QK_PAD * 2
    return pl.pallas_call(
        functools.partial(_mla_attn_kernel, tp=tp, tkc=tkc),
        name="mla_attn",
        grid=(batch, MLA_HEADS, nq),
        in_specs=[
            pl.BlockSpec((1, 1, tq, MLA_QK_PAD), lambda b, h, i: (b, h, i, 0)),
            pl.BlockSpec((1, 1, tp, MLA_QK_PAD), lambda b, h, i: (b, h, 0, 0)),
            pl.BlockSpec((1, 1, tp, MLA_V), lambda b, h, i: (b, h, 0, 0)),
            pl.BlockSpec((1, tq, MLA_V), lambda b, h, i: (b, i, h)),
        ],
        out_specs=pl.BlockSpec((1, tq, MLA_V), lambda b, h, i: (b, i, h)),
        out_shape=jax.ShapeDtypeStruct((batch, tp, MLA_WIDTH), BF16),
        compiler_params=pltpu.CompilerParams(
            dimension_semantics=("arbitrary", "arbitrary", "arbitrary"), vmem_limit_bytes=_vmem_limit(est)),
    )(qm, km, vm, gate)


def _swa_attn_kernel(q_ref, k_ref, v_ref, sink_ref, gate_ref, o_ref, *, tp):
    n = pl.program_id(2)
    rows = SWA_REP * BLOCK
    band = 3 * BLOCK
    q = q_ref[0].reshape(rows, SWA_HEAD_DIM)
    start = pl.multiple_of(jnp.clip((n - 1) * BLOCK, 0, tp - band), BLOCK)
    s_b = _nt_dot(q, k_ref[0, 0, pl.ds(start, band), :])
    s_m = _nt_dot(q, k_ref[0, 0, 0:BLOCK, :])

    rq = n * BLOCK + (lax.broadcasted_iota(jnp.int32, (rows, band), 0) & (BLOCK - 1))
    rk = start + lax.broadcasted_iota(jnp.int32, (rows, band), 1)
    s_b = jnp.where(jnp.abs(rq - rk) <= SWA_WINDOW, jnp.where(rk >= BLOCK, s_b, NEG_INF), NEG_INF)
    col_m = lax.broadcasted_iota(jnp.int32, (rows, BLOCK), 1)
    s_m = jnp.where(col_m >= PAD_FRONT, s_m, NEG_INF)

    sink = sink_ref[0]
    m = jnp.maximum(jnp.maximum(jnp.max(s_b, axis=-1, keepdims=True), jnp.max(s_m, axis=-1, keepdims=True)),
                    sink)
    p_b = jnp.exp2(s_b - m)
    p_m = jnp.exp2(s_m - m)
    l = jnp.exp2(sink - m) + jnp.sum(p_b, axis=-1, keepdims=True) + jnp.sum(p_m, axis=-1, keepdims=True)
    o = (jnp.dot(p_b.astype(BF16), v_ref[0, 0, pl.ds(start, band), :], preferred_element_type=F32)
         + jnp.dot(p_m.astype(BF16), v_ref[0, 0, 0:BLOCK, :], preferred_element_type=F32)) / l
    g = _silu(gate_ref[0])
    for r in range(SWA_REP):
        o_ref[0, :, r * SWA_HEAD_DIM:(r + 1) * SWA_HEAD_DIM] = (
            o[r * BLOCK:(r + 1) * BLOCK] * g[:, r * SWA_HEAD_DIM:(r + 1) * SWA_HEAD_DIM]).astype(BF16)


def _swa_attn(qs, ks, vs, sink_rows, gate, *, batch, tp):
    nb = tp // BLOCK
    gcols = SWA_REP * SWA_HEAD_DIM
    est = 4 * tp * SWA_HEAD_DIM * 2 + 12 * SWA_REP * BLOCK * 4 * BLOCK * 4
    return pl.pallas_call(
        functools.partial(_swa_attn_kernel, tp=tp),
        name="swa_attn",
        grid=(batch, SWA_KV_HEADS, nb),
        in_specs=[
            pl.BlockSpec((1, SWA_REP, BLOCK, SWA_HEAD_DIM), lambda b, g, n: (b, g, n, 0)),
            pl.BlockSpec((1, 1, tp, SWA_HEAD_DIM), lambda b, g, n: (b, g, 0, 0)),
            pl.BlockSpec((1, 1, tp, SWA_HEAD_DIM), lambda b, g, n: (b, g, 0, 0)),
            pl.BlockSpec((1, SWA_REP * BLOCK, 1), lambda b, g, n: (g, 0, 0)),
            pl.BlockSpec((1, BLOCK, gcols), lambda b, g, n: (b, n, MLA_WIDTH // gcols + g)),
        ],
        out_specs=pl.BlockSpec((1, BLOCK, gcols), lambda b, g, n: (b, n, g)),
        out_shape=jax.ShapeDtypeStruct((batch, tp, SWA_WIDTH), BF16),
        compiler_params=pltpu.CompilerParams(
            dimension_semantics=("arbitrary", "arbitrary", "arbitrary"), vmem_limit_bytes=_vmem_limit(est)),
    )(qs, ks, vs, sink_rows, gate)


def _even_out_kernel(ya_ref, yb_ref, w_ref, h_ref, o_ref, *, tm):
    i = pl.program_id(1)
    d = (jnp.dot(ya_ref[0], w_ref[0:MLA_WIDTH, :], preferred_element_type=F32)
         + jnp.dot(yb_ref[0], w_ref[MLA_WIDTH:, :], preferred_element_type=F32))
    row = i * tm + lax.broadcasted_iota(jnp.int32, (tm, D_MODEL), 0)
    o_ref[...] = h_ref[...] + jnp.where(row >= PAD_FRONT, d, 0.0)


def _even_out(ya, yb, w_out, h2d, *, batch, tp, tm):
    nt = tp // tm
    est = w_out.size * 2 + 4 * tm * D_MODEL * 2 + 6 * tm * D_MODEL * 4
    return pl.pallas_call(
        functools.partial(_even_out_kernel, tm=tm),
        name="even_out",
        grid=(batch, nt),
        in_specs=[
            pl.BlockSpec((1, tm, MLA_WIDTH), lambda b, i: (b, i, 0)),
            pl.BlockSpec((1, tm, SWA_WIDTH), lambda b, i: (b, i, 0)),
            _const_spec(w_out.shape),
            pl.BlockSpec((tm, D_MODEL), lambda b, i: (i, b)),
        ],
        out_specs=pl.BlockSpec((tm, D_MODEL), lambda b, i: (i, b)),
        out_shape=jax.ShapeDtypeStruct(h2d.shape, F32),
        compiler_params=pltpu.CompilerParams(
            dimension_semantics=("arbitrary", "arbitrary"), vmem_limit_bytes=_vmem_limit(est)),
    )(ya, yb, w_out, h2d)


def _lru_coeffs(xc, wg_ref, bg_ref, lam_ref, a_ref, b_ref, t0):
    tmr = xc.shape[0]
    t_idx = t0 + (lax.broadcasted_iota(jnp.int32, (tmr, LRU_BLOCK_DIM), 0) // SUBLANE)
    live = t_idx >= PAD_FRONT
    for n in range(LRU_BLOCKS):
        sl = slice(n * LRU_BLOCK_DIM, (n + 1) * LRU_BLOCK_DIM)
        xn = xc[:, sl]
        gg = jnp.dot(xn.astype(BF16), wg_ref[n], preferred_element_type=F32) + bg_ref[n]
        r = jax.nn.sigmoid(gg[:, :LRU_BLOCK_DIM])
        i = jax.nn.sigmoid(gg[:, LRU_BLOCK_DIM:])
        log_a = (-LRU_C) * r * jax.nn.softplus(-lam_ref[:, sl])
        a = jnp.exp(log_a)
        a_ref[:, sl] = a
        one_m_a2 = jnp.tanh(-log_a) * (a * a + 1.0)
        b_ref[:, sl] = jnp.where(live, jnp.sqrt(one_m_a2) * i * xn, 0.0)


def _odd_fwd_kernel(h_ref, halo_ref, gn_ref, w_ref, cw_ref, cb_ref, wg_ref, bg_ref, lam_ref,
                    xc_ref, gate_ref, hf_ref,
                    uprev_ref, hst_ref, a_ref, b_ref, *, tc, nt):
    j = pl.program_id(0)
    tmr = tc * SUBLANE
    halo_rows = SUBLANE
    prev_rows = 2 * SUBLANE

    @pl.when(j == 0)
    def _():
        uprev_ref[...] = jnp.zeros_like(uprev_ref)
        hst_ref[...] = jnp.zeros_like(hst_ref)

    x = jnp.concatenate([h_ref[...].reshape(tmr, D_MODEL), halo_ref[...].reshape(halo_rows, D_MODEL)], axis=0)
    z = (x * _rms_scale(x, D_MODEL) * gn_ref[...]).astype(BF16)
    u = jnp.dot(z, w_ref[:, 0:LRU_WIDTH], preferred_element_type=F32)
    gate_ref[...] = jnp.dot(z[0:tmr], w_ref[:, LRU_WIDTH:], preferred_element_type=F32).reshape(
        tc, SUBLANE, D_INNER)

    u_next = jnp.where(j == nt - 1, 0.0, u[tmr:])
    ue = jnp.concatenate([uprev_ref[...], u[0:tmr], u_next], axis=0)
    uprev_ref[...] = u[tmr - prev_rows:tmr]
    xc = cb_ref[...] + ue[0:tmr] * cw_ref[0:1, :]
    for tap in range(1, CONV_WIDTH):
        xc = xc + ue[tap * SUBLANE:tap * SUBLANE + tmr] * cw_ref[tap:tap + 1, :]
    xc_ref[...] = xc.reshape(tc, SUBLANE, LRU_WIDTH)

    _lru_coeffs(xc, wg_ref, bg_ref, lam_ref, a_ref, b_ref, j * tc)

    def step(t, hs):
        r = pl.multiple_of(t * SUBLANE, SUBLANE)
        hs = a_ref[pl.ds(r, SUBLANE), :] * hs + b_ref[pl.ds(r, SUBLANE), :]
        hf_ref[t] = hs
        return hs

    hst_ref[...] = lax.fori_loop(0, tc, step, hst_ref[...], unroll=8)


def _odd_fwd(h, gn, w_in, cw, cb, wg, bg, lam, *, batch, tp, tc):
    nt = tp // tc
    tmr = tc * batch
    blk = lambda cols: pl.BlockSpec((tc, batch, cols), lambda j: (j, 0, 0))
    est = (w_in.size * 2 + 2 * tmr * D_MODEL * 4 + 3 * 2 * tmr * LRU_WIDTH * 4
           + 2 * tmr * LRU_WIDTH * 4 + 5 * tmr * LRU_WIDTH * 4)
    return pl.pallas_call(
        functools.partial(_odd_fwd_kernel, tc=tc, nt=nt),
        name="odd_fwd",
        grid=(nt,),
        in_specs=[
            blk(D_MODEL),
            pl.BlockSpec((1, batch, D_MODEL), lambda j: (jnp.minimum((j + 1) * tc, tp - 1), 0, 0)),
            _const_spec((1, D_MODEL)),
            _const_spec(w_in.shape),
            _const_spec(cw.shape),
            _const_spec(cb.shape),
            _const_spec(wg.shape),
            _const_spec(bg.shape),
            _const_spec(lam.shape),
        ],
        out_specs=[blk(LRU_WIDTH), blk(D_INNER), blk(LRU_WIDTH)],
        out_shape=[jax.ShapeDtypeStruct((tp, batch, LRU_WIDTH), F32),
                   jax.ShapeDtypeStruct((tp, batch, D_INNER), F32),
                   jax.ShapeDtypeStruct((tp, batch, LRU_WIDTH), F32)],
        scratch_shapes=[pltpu.VMEM((2 * SUBLANE, LRU_WIDTH), F32),
                        pltpu.VMEM((SUBLANE, LRU_WIDTH), F32),
                        pltpu.VMEM((tmr, LRU_WIDTH), F32),
                        pltpu.VMEM((tmr, LRU_WIDTH), F32)],
        compiler_params=pltpu.CompilerParams(
            dimension_semantics=("arbitrary",), vmem_limit_bytes=_vmem_limit(est)),
    )(h, h, gn, w_in, cw, cb, wg, bg, lam)


def _odd_bwd_kernel(xc_ref, gate_ref, hf_ref, h_ref, wg_ref, bg_ref, lam_ref, w_ref, o_ref,
                    hst_ref, a_ref, b_ref, y_ref, *, tc, nt):
    j = pl.program_id(0)
    jj = nt - 1 - j
    tmr = tc * SUBLANE

    @pl.when(j == 0)
    def _():
        hst_ref[...] = jnp.zeros_like(hst_ref)

    xc = xc_ref[...].reshape(tmr, LRU_WIDTH)
    _lru_coeffs(xc, wg_ref, bg_ref, lam_ref, a_ref, b_ref, jj * tc)

    def step(s, hs):
        t = tc - 1 - s
        r = pl.multiple_of(t * SUBLANE, SUBLANE)
        hs = a_ref[pl.ds(r, SUBLANE), :] * hs + b_ref[pl.ds(r, SUBLANE), :]
        y_ref[pl.ds(r, SUBLANE), :] = hf_ref[t] + hs
        return hs

    hst_ref[...] = lax.fori_loop(0, tc, step, hst_ref[...], unroll=8)

    y = (y_ref[...] * _silu(gate_ref[...].reshape(tmr, D_INNER))).astype(BF16)
    d = jnp.dot(y, w_ref[...], preferred_element_type=F32)
    t_idx = jj * tc + (lax.broadcasted_iota(jnp.int32, (tmr, D_MODEL), 0) // SUBLANE)
    o_ref[...] = h_ref[...] + jnp.where(t_idx >= PAD_FRONT, d, 0.0).reshape(tc, SUBLANE, D_MODEL)


def _odd_bwd(xc, gate, hf, h, wg, bg, lam, w_out, *, batch, tp, tc):
    nt = tp // tc
    tmr = tc * batch
    blk = lambda cols: pl.BlockSpec((tc, batch, cols), lambda j: (nt - 1 - j, 0, 0))
    est = (w_out.size * 2 + 3 * 2 * tmr * LRU_WIDTH * 4 + 4 * tmr * D_MODEL * 4
           + 3 * tmr * LRU_WIDTH * 4 + 4 * tmr * LRU_WIDTH * 4)
    return pl.pallas_call(
        functools.partial(_odd_bwd_kernel, tc=tc, nt=nt),
        name="odd_bwd",
        grid=(nt,),
        in_specs=[
            blk(LRU_WIDTH), blk(D_INNER), blk(LRU_WIDTH), blk(D_MODEL),
            _const_spec(wg.shape),
            _const_spec(bg.shape),
            _const_spec(lam.shape),
            _const_spec(w_out.shape),
        ],
        out_specs=blk(D_MODEL),
        out_shape=jax.ShapeDtypeStruct((tp, batch, D_MODEL), F32),
        scratch_shapes=[pltpu.VMEM((SUBLANE, LRU_WIDTH), F32),
                        pltpu.VMEM((tmr, LRU_WIDTH), F32),
                        pltpu.VMEM((tmr, LRU_WIDTH), F32),
                        pltpu.VMEM((tmr, LRU_WIDTH), F32)],
        compiler_params=pltpu.CompilerParams(
            dimension_semantics=("arbitrary",), vmem_limit_bytes=_vmem_limit(est)),
    )(xc, gate, hf, h, wg, bg, lam, w_out)


def _rope_tables(tp):
    pos = (jnp.arange(tp, dtype=F32) - PAD_FRONT)[:, None]

    def table(dim):
        inv = ROPE_THETA ** (-jnp.arange(0, dim, 2, dtype=F32) / dim)
        ang = pos * inv[None, :]
        return jnp.cos(ang), jnp.sin(ang)

    c, s = table(MLA_ROPE)
    zeros = jnp.zeros((tp, LANE - MLA_ROPE), F32)
    cm = jnp.concatenate([c, c, zeros], axis=1)
    sm = jnp.concatenate([-s, s, zeros], axis=1)
    c, s = table(SWA_HEAD_DIM)
    cs = jnp.concatenate([c, c], axis=1)
    ss = jnp.concatenate([-s, s], axis=1)
    return cm, sm, cs, ss


def _prep_even(w_in, w_uq, w_ukv, g_qn, g_kn):
    lat_w = MLA_Q_RANK + MLA_KV_RANK + MLA_ROPE
    w_all = jnp.concatenate(
        [w_in[:, :lat_w], jnp.zeros((D_MODEL, LAT_COLS - lat_w), w_in.dtype), w_in[:, lat_w:]], axis=1).astype(BF16)
    wuq = jnp.pad(w_uq, ((0, 0), (0, 0), (0, MLA_QK_PAD - MLA_QK))).reshape(
        MLA_Q_RANK, MLA_HEADS * MLA_QK_PAD).astype(BF16)
    wukv = w_ukv.reshape(MLA_KV_RANK, MLA_HEADS * (MLA_NOPE + MLA_V)).astype(BF16)
    pad_g = lambda g: jnp.pad(g, (0, MLA_QK_PAD - MLA_QK)).reshape(1, MLA_QK_PAD)
    return w_all, wuq, wukv, pad_g(g_qn), pad_g(g_kn)


def _prep_gates(w_a, b_a, w_x, b_x):
    wg = jnp.concatenate([w_a, w_x], axis=-1).astype(BF16)
    bg = jnp.concatenate([b_a.reshape(LRU_BLOCKS, 1, LRU_BLOCK_DIM), b_x.reshape(LRU_BLOCKS, 1, LRU_BLOCK_DIM)],
                         axis=-1)
    return wg, bg


def _tile(tp, prefs):
    for t in prefs:
        if tp % t == 0:
            return t
    raise ValueError(f"no tile of {prefs} divides {tp}")


def kernel(x, meta_tokens, norm_g, even_w_in, mla_g_q_lat, mla_g_kv_lat, mla_w_uq, mla_w_ukv, mla_g_qn, mla_g_kn, swa_g_qn, swa_g_kn, swa_sink, even_w_out, odd_w_in, lru_conv_w, lru_conv_b, lru_w_a, lru_b_a, lru_w_x, lru_b_x, lru_lambda, odd_w_out):
    batch, seq, d = x.shape
    assert d == D_MODEL and batch == SUBLANE and seq % BLOCK == 0
    tp = BLOCK + seq
    tm = _tile(tp, (384, 128))
    tkc = _tile(seq, (512, 256, 128))
    tc = _tile(tp, (32, 16))

    meta = jnp.broadcast_to(meta_tokens.astype(x.dtype)[None], (batch, N_META, D_MODEL))
    h = jnp.concatenate([jnp.zeros((batch, PAD_FRONT, D_MODEL), x.dtype), meta, x], axis=1)
    h = h.transpose(1, 0, 2)
    cm, sm, cs, ss = _rope_tables(tp)

    for l in range(DEPTH):
        j = l // 2
        gn = norm_g[l].reshape(1, D_MODEL)
        if l % 2 == 0:
            w_all, wuq, wukv, gqn, gkn = _prep_even(even_w_in[j], mla_w_uq[j], mla_w_ukv[j], mla_g_qn[j],
                                                    mla_g_kn[j])
            h2d = h.reshape(tp, batch * D_MODEL)
            qm, km, vm, qs, ks, vs, gate = _even_in(
                h2d, gn, w_all, mla_g_q_lat[j].reshape(1, -1), mla_g_kv_lat[j].reshape(1, -1), wuq, wukv,
                gqn, gkn, swa_g_qn[j].reshape(1, -1), swa_g_kn[j].reshape(1, -1), cm, sm, cs, ss,
                batch=batch, tp=tp, tm=tm)
            ya = _mla_attn(qm, km, vm, gate, batch=batch, tp=tp, tq=tm, tkc=tkc)
            sink_rows = jnp.repeat(swa_sink[j].astype(F32) * LOG2E, BLOCK).reshape(
                SWA_KV_HEADS, SWA_REP * BLOCK, 1)
            yb = _swa_attn(qs, ks, vs, sink_rows, gate, batch=batch, tp=tp)
            h2d = _even_out(ya, yb, even_w_out[j].astype(BF16), h2d, batch=batch, tp=tp, tm=tm)
            h = h2d.reshape(tp, batch, D_MODEL)
        else:
            wgf, bgf = _prep_gates(lru_w_a[j, 0], lru_b_a[j, 0], lru_w_x[j, 0], lru_b_x[j, 0])
            wgb, bgb = _prep_gates(lru_w_a[j, 1], lru_b_a[j, 1], lru_w_x[j, 1], lru_b_x[j, 1])
            xc, gate, hf = _odd_fwd(h, gn, odd_w_in[j].astype(BF16), lru_conv_w[j], lru_conv_b[j].reshape(1, -1),
                                    wgf, bgf, lru_lambda[j, 0].reshape(1, -1), batch=batch, tp=tp, tc=tc)
            h = _odd_bwd(xc, gate, hf, h, wgb, bgb, lru_lambda[j, 1].reshape(1, -1), odd_w_out[j].astype(BF16),
                         batch=batch, tp=tp, tc=tc)
    return h[BLOCK:].transpose(1, 0, 2)
```

```python
import functools
import math

import jax
import jax.numpy as jnp
from jax import lax
from jax.experimental import pallas as pl
from jax.experimental.pallas import tpu as pltpu

F32 = jnp.float32
BF16 = jnp.bfloat16

D_MODEL = 1024
DEPTH = 4
N_META = 16
D_INNER = 2 * D_MODEL
ROPE_THETA = 10000.0
EPS = 1e-6
NEG_INF = -1e30

MLA_HEADS = 8
MLA_Q_RANK = 384
MLA_KV_RANK = 256
MLA_NOPE = 128
MLA_ROPE = 64
MLA_QK = MLA_NOPE + MLA_ROPE
MLA_V = 128
MLA_WIDTH = MLA_HEADS * MLA_V

SWA_HEADS = 8
SWA_KV_HEADS = 2
SWA_REP = SWA_HEADS // SWA_KV_HEADS
SWA_HEAD_DIM = 128
SWA_WINDOW = 128
SWA_WIDTH = SWA_HEADS * SWA_HEAD_DIM
SWA_KV_WIDTH = SWA_KV_HEADS * SWA_HEAD_DIM

LRU_WIDTH = D_INNER
LRU_BLOCKS = 16
LRU_BLOCK_DIM = LRU_WIDTH // LRU_BLOCKS
LRU_C = 8.0
CONV_WIDTH = 4

LANE = 128
SUBLANE = 8
MXU_DIM = 256
V7X_VMEM_BYTES = 64 * 1024 * 1024

BLOCK = 128
PAD_FRONT = BLOCK - N_META
MLA_QK_PAD = MXU_DIM
LAT_COLS = 768
SWA_COLS = SWA_WIDTH + 2 * SWA_KV_WIDTH
LOG2E = 1.4426950408889634


def _vmem_limit(nbytes):
    return int(min(V7X_VMEM_BYTES - (4 << 20), max(nbytes, 16 << 20)))


def _const_spec(shape):
    nd = len(shape)
    return pl.BlockSpec(shape, lambda *_: (0,) * nd, pipeline_mode=pl.Buffered(1))


def _rms_scale(x, n):
    return lax.rsqrt(jnp.sum(x * x, axis=-1, keepdims=True) * (1.0 / n) + EPS)


def _silu(g):
    gh = 0.5 * g
    return gh * (1.0 + jnp.tanh(gh))


def _nt_dot(a, b):
    return lax.dot_general(a, b, (((1,), (1,)), ((), ())), preferred_element_type=F32)


def _even_in_kernel(h_ref, gn_ref, w_ref, gql_ref, gkvl_ref, wuq_ref, wukv_ref, gqn_ref, gkn_ref,
                    sgq_ref, sgk_ref, cm_ref, sm_ref, cs_ref, ss_ref,
                    qm_ref, km_ref, vm_ref, qs_ref, ks_ref, vs_ref, gate_ref):
    x = h_ref[...]
    z = (x * _rms_scale(x, D_MODEL) * gn_ref[...]).astype(BF16)

    lat = jnp.dot(z, w_ref[:, 0:LAT_COLS], preferred_element_type=F32)
    cq = lat[:, 0:MLA_Q_RANK]
    ckv = lat[:, MLA_Q_RANK:MLA_Q_RANK + MLA_KV_RANK]
    kpe = lat[:, MLA_Q_RANK + MLA_KV_RANK:LAT_COLS]
    cqn = (cq * _rms_scale(cq, MLA_Q_RANK) * gql_ref[...]).astype(BF16)
    ckvn = (ckv * _rms_scale(ckv, MLA_KV_RANK) * gkvl_ref[...]).astype(BF16)

    cm = cm_ref[...]
    sm = sm_ref[...]

    def rope_mla(t):
        rot = pltpu.roll(t, MLA_ROPE // 2, 1) + pltpu.roll(t, LANE - MLA_ROPE // 2, 1)
        return t * cm + rot * sm

    q_scale = (MLA_QK ** -0.5) * LOG2E
    gqn = gqn_ref[...]
    q_all = jnp.dot(cqn, wuq_ref[...], preferred_element_type=F32)
    for hd in range(MLA_HEADS):
        qh = q_all[:, hd * MLA_QK_PAD:(hd + 1) * MLA_QK_PAD]
        qn = qh * _rms_scale(qh, MLA_QK) * gqn
        q_out = jnp.concatenate([qn[:, :MLA_NOPE], rope_mla(qn[:, MLA_NOPE:])], axis=-1) * q_scale
        qm_ref[0, hd] = q_out.astype(BF16)

    gkn = gkn_ref[...]
    kv = jnp.dot(ckvn, wukv_ref[...], preferred_element_type=F32)
    ss_pe = jnp.sum(kpe * kpe, axis=-1, keepdims=True)
    kr = rope_mla(kpe * gkn[:, MLA_NOPE:])
    for hd in range(MLA_HEADS):
        kn = kv[:, hd * MLA_QK_PAD:hd * MLA_QK_PAD + MLA_NOPE]
        rs = lax.rsqrt((jnp.sum(kn * kn, axis=-1, keepdims=True) + ss_pe) * (1.0 / MLA_QK) + EPS)
        k_out = jnp.concatenate([kn * rs * gkn[:, :MLA_NOPE], kr * rs], axis=-1)
        km_ref[0, hd] = k_out.astype(BF16)
        vm_ref[0, hd] = kv[:, hd * MLA_QK_PAD + MLA_NOPE:(hd + 1) * MLA_QK_PAD].astype(BF16)

    sw = jnp.dot(z, w_ref[:, LAT_COLS:LAT_COLS + SWA_COLS], preferred_element_type=F32)
    cs = cs_ref[...]
    ss = ss_ref[...]

    def norm_rope_swa(t, g):
        n = t * _rms_scale(t, SWA_HEAD_DIM) * g
        return n * cs + pltpu.roll(n, SWA_HEAD_DIM // 2, 1) * ss

    s_scale = (SWA_HEAD_DIM ** -0.5) * LOG2E
    for hd in range(SWA_HEADS):
        t = sw[:, hd * SWA_HEAD_DIM:(hd + 1) * SWA_HEAD_DIM]
        qs_ref[0, hd] = (norm_rope_swa(t, sgq_ref[...]) * s_scale).astype(BF16)
    for hd in range(SWA_KV_HEADS):
        t = sw[:, SWA_WIDTH + hd * SWA_HEAD_DIM:SWA_WIDTH + (hd + 1) * SWA_HEAD_DIM]
        ks_ref[0, hd] = norm_rope_swa(t, sgk_ref[...]).astype(BF16)
        vs_ref[0, hd] = sw[:, SWA_WIDTH + SWA_KV_WIDTH + hd * SWA_HEAD_DIM:
                           SWA_WIDTH + SWA_KV_WIDTH + (hd + 1) * SWA_HEAD_DIM].astype(BF16)

    gate_ref[0] = jnp.dot(z, w_ref[:, LAT_COLS + SWA_COLS:], preferred_element_type=F32).astype(BF16)


def _h_tile_spec(h, tm):
    if h.ndim == 3:
        return pl.BlockSpec((None, tm, D_MODEL), lambda b, i: (b, i, 0))
    return pl.BlockSpec((tm, D_MODEL), lambda b, i: (i, b))


def _even_in(h2d, gn, w_all, gql, gkvl, wuq, wukv, gqn, gkn, sgq, sgk, cm, sm, cs, ss, *, batch, tp, tm):
    nt = tp // tm
    row_spec = lambda cols: pl.BlockSpec((tm, cols), lambda b, i: (i, 0))
    head_out = lambda heads, cols: pl.BlockSpec((1, heads, tm, cols), lambda b, i: (b, 0, i, 0))
    n_w = w_all.shape[1]
    est = (w_all.size * 2 + (wuq.size + wukv.size) * 2 + 2 * tm * D_MODEL * 4
           + 2 * tm * 2 * (2 * MLA_HEADS * MLA_QK_PAD + MLA_WIDTH + SWA_WIDTH + 2 * SWA_KV_WIDTH)
           + 2 * tm * D_INNER * 4 + 3 * tm * n_w * 4)
    return pl.pallas_call(
        _even_in_kernel,
        name="even_in",
        grid=(batch, nt),
        in_specs=[
            _h_tile_spec(h2d, tm),
            _const_spec((1, D_MODEL)),
            _const_spec(w_all.shape),
            _const_spec((1, MLA_Q_RANK)),
            _const_spec((1, MLA_KV_RANK)),
            _const_spec(wuq.shape),
            _const_spec(wukv.shape),
            _const_spec((1, MLA_QK_PAD)),
            _const_spec((1, MLA_QK_PAD)),
            _const_spec((1, SWA_HEAD_DIM)),
            _const_spec((1, SWA_HEAD_DIM)),
            row_spec(LANE), row_spec(LANE), row_spec(LANE), row_spec(LANE),
        ],
        out_specs=[
            head_out(MLA_HEADS, MLA_QK_PAD),
            head_out(MLA_HEADS, MLA_QK_PAD),
            head_out(MLA_HEADS, MLA_V),
            head_out(SWA_HEADS, SWA_HEAD_DIM),
            head_out(SWA_KV_HEADS, SWA_HEAD_DIM),
            head_out(SWA_KV_HEADS, SWA_HEAD_DIM),
            pl.BlockSpec((1, tm, D_INNER), lambda b, i: (b, i, 0)),
        ],
        out_shape=[
            jax.ShapeDtypeStruct((batch, MLA_HEADS, tp, MLA_QK_PAD), BF16),
            jax.ShapeDtypeStruct((batch, MLA_HEADS, tp, MLA_QK_PAD), BF16),
            jax.ShapeDtypeStruct((batch, MLA_HEADS, tp, MLA_V), BF16),
            jax.ShapeDtypeStruct((batch, SWA_HEADS, tp, SWA_HEAD_DIM), BF16),
            jax.ShapeDtypeStruct((batch, SWA_KV_HEADS, tp, SWA_HEAD_DIM), BF16),
            jax.ShapeDtypeStruct((batch, SWA_KV_HEADS, tp, SWA_HEAD_DIM), BF16),
            jax.ShapeDtypeStruct((batch, tp, D_INNER), BF16),
        ],
        compiler_params=pltpu.CompilerParams(
            dimension_semantics=("arbitrary", "arbitrary"), vmem_limit_bytes=_vmem_limit(est)),
    )(h2d, gn, w_all, gql, gkvl, wuq, wukv, gqn, gkn, sgq, sgk, cm, sm, cs, ss)


def _mla_attn_kernel(q_ref, k_ref, v_ref, gate_ref, o_ref, vx_ref, *, tp, tkc):
    q = q_ref[0, 0]
    tq = q.shape[0]

    @pl.when(pl.program_id(2) == 0)
    def _():
        lane = lax.broadcasted_iota(jnp.int32, (tp, MXU_DIM - MLA_V), 1)
        vx_ref[:, 0:MLA_V] = v_ref[0, 0]
        vx_ref[:, MLA_V:] = jnp.where(lane == 0, 1.0, 0.0).astype(BF16)

    s = _nt_dot(q, k_ref[0, 0, 0:BLOCK, :])
    col = lax.broadcasted_iota(jnp.int32, (tq, BLOCK), 1)
    s = jnp.where(col >= PAD_FRONT, s, NEG_INF)
    m = jnp.max(s, axis=-1, keepdims=True)
    p = jnp.exp2(s - m)
    acc = jnp.dot(p.astype(BF16), vx_ref[0:BLOCK, :], preferred_element_type=F32)

    n_chunks = (tp - BLOCK) // tkc

    def scores(c):
        start = pl.multiple_of(BLOCK + c * tkc, BLOCK)
        return _nt_dot(q, k_ref[0, 0, pl.ds(start, tkc), :])

    def softmax_step(s, m):
        m_new = jnp.maximum(m, jnp.max(s, axis=-1, keepdims=True))
        alpha = jnp.exp2(m - m_new)
        p = jnp.exp2(s - m_new)
        return p.astype(BF16), alpha, m_new

    def pv_step(c, p, alpha, acc):
        start = pl.multiple_of(BLOCK + c * tkc, BLOCK)
        return alpha * acc + jnp.dot(p, vx_ref[pl.ds(start, tkc), :], preferred_element_type=F32)

    def chunk(c, carry):
        s, p_prev, alpha_prev, m, acc = carry
        s_next = scores(c + 1)
        p, alpha, m = softmax_step(s, m)
        acc = pv_step(c - 1, p_prev, alpha_prev, acc)
        return s_next, p, alpha, m, acc

    s1 = scores(1)
    p0, alpha0, m = softmax_step(scores(0), m)
    carry = (s1, p0, alpha0, m, acc)
    for c in range(1, n_chunks - 1):
        carry = chunk(c, carry)
    s, p_prev, alpha_prev, m, acc = carry
    p, alpha, m = softmax_step(s, m)
    acc = pv_step(n_chunks - 2, p_prev, alpha_prev, acc)
    acc = pv_step(n_chunks - 1, p, alpha, acc)
    o = acc[:, 0:MLA_V] / acc[:, MLA_V:MLA_V + 1]
    o_ref[0] = (o * _silu(gate_ref[0].astype(F32))).astype(BF16)


def _mla_attn(qm, km, vm, gate, *, batch, tp, tq, tkc):
    nq = tp // tq
    est = (2 * tp * (MLA_QK_PAD + MLA_V) * 2 + tp * MXU_DIM * 2 + 6 * tq * tkc * 4
           + 4 * tq * MLA_QK_PAD * 2)
    return pl.pallas_call(
        functools.partial(_mla_attn_kernel, tp=tp, tkc=tkc),
        name="mla_attn",
        grid=(batch, MLA_HEADS, nq),
        in_specs=[
            pl.BlockSpec((1, 1, tq, MLA_QK_PAD), lambda b, h, i: (b, h, i, 0)),
            pl.BlockSpec((1, 1, tp, MLA_QK_PAD), lambda b, h, i: (b, h, 0, 0)),
            pl.BlockSpec((1, 1, tp, MLA_V), lambda b, h, i: (b, h, 0, 0)),
            pl.BlockSpec((1, tq, MLA_V), lambda b, h, i: (b, i, h)),
        ],
        out_specs=pl.BlockSpec((1, tq, MLA_V), lambda b, h, i: (b, i, h)),
        out_shape=jax.ShapeDtypeStruct((batch, tp, MLA_WIDTH), BF16),
        scratch_shapes=[pltpu.VMEM((tp, MXU_DIM), BF16)],
        compiler_params=pltpu.CompilerParams(
            dimension_semantics=("arbitrary", "arbitrary", "arbitrary"), vmem_limit_bytes=_vmem_limit(est)),
    )(qm, km, vm, gate)


def _swa_attn_kernel(q_ref, k_ref, v_ref, sink_ref, gate_ref, o_ref, *, tp):
    n = pl.program_id(2)
    rows = SWA_REP * BLOCK
    band = 3 * BLOCK
    q = q_ref[0].reshape(rows, SWA_HEAD_DIM)
    start = pl.multiple_of(jnp.clip((n - 1) * BLOCK, 0, tp - band), BLOCK)
    s_b = _nt_dot(q, k_ref[0, 0, pl.ds(start, band), :])
    s_m = _nt_dot(q, k_ref[0, 0, 0:BLOCK, :])

    rq = n * BLOCK + (lax.broadcasted_iota(jnp.int32, (rows, band), 0) & (BLOCK - 1))
    rk = start + lax.broadcasted_iota(jnp.int32, (rows, band), 1)
    s_b = jnp.where(jnp.abs(rq - rk) <= SWA_WINDOW, jnp.where(rk >= BLOCK, s_b, NEG_INF), NEG_INF)
    col_m = lax.broadcasted_iota(jnp.int32, (rows, BLOCK), 1)
    s_m = jnp.where(col_m >= PAD_FRONT, s_m, NEG_INF)

    sink = sink_ref[0]
    m = jnp.maximum(jnp.maximum(jnp.max(s_b, axis=-1, keepdims=True), jnp.max(s_m, axis=-1, keepdims=True)),
                    sink)
    p_b = jnp.exp2(s_b - m)
    p_m = jnp.exp2(s_m - m)
    l = jnp.exp2(sink - m) + jnp.sum(p_b, axis=-1, keepdims=True) + jnp.sum(p_m, axis=-1, keepdims=True)
    o = (jnp.dot(p_b.astype(BF16), v_ref[0, 0, pl.ds(start, band), :], preferred_element_type=F32)
         + jnp.dot(p_m.astype(BF16), v_ref[0, 0, 0:BLOCK, :], preferred_element_type=F32)) / l
    g = _silu(gate_ref[0].astype(F32))
    for r in range(SWA_REP):
        o_ref[0, :, r * SWA_HEAD_DIM:(r + 1) * SWA_HEAD_DIM] = (
            o[r * BLOCK:(r + 1) * BLOCK] * g[:, r * SWA_HEAD_DIM:(r + 1) * SWA_HEAD_DIM]).astype(BF16)


def _swa_attn(qs, ks, vs, sink_rows, gate, *, batch, tp):
    nb = tp // BLOCK
    gcols = SWA_REP * SWA_HEAD_DIM
    est = 4 * tp * SWA_HEAD_DIM * 2 + 12 * SWA_REP * BLOCK * 4 * BLOCK * 4
    return pl.pallas_call(
        functools.partial(_swa_attn_kernel, tp=tp),
        name="swa_attn",
        grid=(batch, SWA_KV_HEADS, nb),
        in_specs=[
            pl.BlockSpec((1, SWA_REP, BLOCK, SWA_HEAD_DIM), lambda b, g, n: (b, g, n, 0)),
            pl.BlockSpec((1, 1, tp, SWA_HEAD_DIM), lambda b, g, n: (b, g, 0, 0)),
            pl.BlockSpec((1, 1, tp, SWA_HEAD_DIM), lambda b, g, n: (b, g, 0, 0)),
            pl.BlockSpec((1, SWA_REP * BLOCK, 1), lambda b, g, n: (g, 0, 0)),
            pl.BlockSpec((1, BLOCK, gcols), lambda b, g, n: (b, n, MLA_WIDTH // gcols + g)),
        ],
        out_specs=pl.BlockSpec((1, BLOCK, gcols), lambda b, g, n: (b, n, g)),
        out_shape=jax.ShapeDtypeStruct((batch, tp, SWA_WIDTH), BF16),
        compiler_params=pltpu.CompilerParams(
            dimension_semantics=("arbitrary", "arbitrary", "arbitrary"), vmem_limit_bytes=_vmem_limit(est)),
    )(qs, ks, vs, sink_rows, gate)


def _even_out_kernel(ya_ref, yb_ref, w_ref, h_ref, o_ref, *, tm):
    i = pl.program_id(1)
    d = (jnp.dot(ya_ref[0], w_ref[0:MLA_WIDTH, :], preferred_element_type=F32)
         + jnp.dot(yb_ref[0], w_ref[MLA_WIDTH:, :], preferred_element_type=F32))
    row = i * tm + lax.broadcasted_iota(jnp.int32, (tm, D_MODEL), 0)
    o_ref[...] = h_ref[...] + jnp.where(row >= PAD_FRONT, d, 0.0)


def _even_out(ya, yb, w_out, h2d, *, batch, tp, tm):
    nt = tp // tm
    est = w_out.size * 2 + 4 * tm * D_MODEL * 2 + 6 * tm * D_MODEL * 4
    return pl.pallas_call(
        functools.partial(_even_out_kernel, tm=tm),
        name="even_out",
        grid=(batch, nt),
        in_specs=[
            pl.BlockSpec((1, tm, MLA_WIDTH), lambda b, i: (b, i, 0)),
            pl.BlockSpec((1, tm, SWA_WIDTH), lambda b, i: (b, i, 0)),
            _const_spec(w_out.shape),
            _h_tile_spec(h2d, tm),
        ],
        out_specs=pl.BlockSpec((tm, D_MODEL), lambda b, i: (i, b)),
        out_shape=jax.ShapeDtypeStruct((tp, batch * D_MODEL), F32),
        compiler_params=pltpu.CompilerParams(
            dimension_semantics=("arbitrary", "arbitrary"), vmem_limit_bytes=_vmem_limit(est)),
    )(ya, yb, w_out, h2d)


def _lru_coeffs(xc, xcb, wg_ref, bg_ref, lam_ref, a_ref, b_ref, t0=None):
    tmr = xc.shape[0]
    if t0 is not None:
        t_idx = t0 + (lax.broadcasted_iota(jnp.int32, (tmr, LRU_BLOCK_DIM), 0) // SUBLANE)
        live = t_idx >= PAD_FRONT
    for n in range(LRU_BLOCKS):
        sl = slice(n * LRU_BLOCK_DIM, (n + 1) * LRU_BLOCK_DIM)
        xn = xc[:, sl]
        gh = jnp.dot(xcb[:, sl], wg_ref[n], preferred_element_type=F32) + bg_ref[n]
        tr = jnp.tanh(gh[:, :LRU_BLOCK_DIM])
        ti = jnp.tanh(gh[:, LRU_BLOCK_DIM:])
        ch = (-0.5 * LRU_C) * jax.nn.softplus(-lam_ref[:, sl])
        log_a = tr * ch + ch
        a = jnp.exp(log_a)
        a_ref[:, sl] = a
        one_m_a2 = jnp.tanh(log_a) * (-1.0 - a * a)
        b = jnp.sqrt(one_m_a2) * (0.5 * ti + 0.5) * xn
        b_ref[:, sl] = b if t0 is None else jnp.where(live, b, 0.0)


def _odd_fwd_kernel(h_ref, halo_ref, gn_ref, w_ref, cw_ref, cb_ref, wg_ref, bg_ref, lam_ref,
                    xc_ref, gate_ref, hf_ref,
                    uprev_ref, hst_ref, a_ref, b_ref, *, tc, nt):
    j = pl.program_id(0)
    tmr = tc * SUBLANE
    halo_rows = SUBLANE
    prev_rows = 2 * SUBLANE

    @pl.when(j == 0)
    def _():
        uprev_ref[...] = jnp.zeros_like(uprev_ref)
        hst_ref[...] = jnp.zeros_like(hst_ref)

    x = jnp.concatenate([h_ref[...].reshape(tmr, D_MODEL), halo_ref[...].reshape(halo_rows, D_MODEL)], axis=0)
    z = (x * _rms_scale(x, D_MODEL) * gn_ref[...]).astype(BF16)
    u = jnp.dot(z, w_ref[:, 0:LRU_WIDTH], preferred_element_type=F32)
    gate_ref[...] = jnp.dot(z[0:tmr], w_ref[:, LRU_WIDTH:], preferred_element_type=F32).astype(BF16)

    u_next = jnp.where(j == nt - 1, 0.0, u[tmr:])
    ue = jnp.concatenate([uprev_ref[...], u[0:tmr], u_next], axis=0)
    uprev_ref[...] = u[tmr - prev_rows:tmr]
    xc = cb_ref[...] + ue[0:tmr] * cw_ref[0:1, :]
    for tap in range(1, CONV_WIDTH):
        xc = xc + ue[tap * SUBLANE:tap * SUBLANE + tmr] * cw_ref[tap:tap + 1, :]
    xcb = xc.astype(BF16)
    xc_ref[...] = xcb

    _lru_coeffs(xc, xcb, wg_ref, bg_ref, lam_ref, a_ref, b_ref, j * tc)

    def step(t, hs):
        r = pl.multiple_of(t * SUBLANE, SUBLANE)
        hs = a_ref[pl.ds(r, SUBLANE), :] * hs + b_ref[pl.ds(r, SUBLANE), :]
        b_ref[pl.ds(r, SUBLANE), :] = hs
        return hs

    hst_ref[...] = lax.fori_loop(0, tc, step, hst_ref[...], unroll=8)
    hf_ref[...] = b_ref[...].astype(BF16)


def _odd_fwd(h, gn, w_in, cw, cb, wg, bg, lam, *, batch, tp, tc):
    nt = tp // tc
    tmr = tc * batch
    blk = lambda cols: pl.BlockSpec((tc, batch, cols), lambda j: (j, 0, 0))
    row_blk = pl.BlockSpec((tmr, LRU_WIDTH), lambda j: (j, 0))
    est = (w_in.size * 2 + 2 * tmr * D_MODEL * 4 + 3 * 2 * tmr * LRU_WIDTH * 2
           + 2 * tmr * LRU_WIDTH * 4 + 5 * tmr * LRU_WIDTH * 4)
    return pl.pallas_call(
        functools.partial(_odd_fwd_kernel, tc=tc, nt=nt),
        name="odd_fwd",
        grid=(nt,),
        in_specs=[
            blk(D_MODEL),
            pl.BlockSpec((1, batch, D_MODEL), lambda j: (jnp.minimum((j + 1) * tc, tp - 1), 0, 0)),
            _const_spec((1, D_MODEL)),
            _const_spec(w_in.shape),
            _const_spec(cw.shape),
            _const_spec(cb.shape),
            _const_spec(wg.shape),
            _const_spec(bg.shape),
            _const_spec(lam.shape),
        ],
        out_specs=[row_blk, row_blk, row_blk],
        out_shape=[jax.ShapeDtypeStruct((tp * batch, LRU_WIDTH), BF16)] * 3,
        scratch_shapes=[pltpu.VMEM((2 * SUBLANE, LRU_WIDTH), F32),
                        pltpu.VMEM((SUBLANE, LRU_WIDTH), F32),
                        pltpu.VMEM((tmr, LRU_WIDTH), F32),
                        pltpu.VMEM((tmr, LRU_WIDTH), F32)],
        compiler_params=pltpu.CompilerParams(
            dimension_semantics=("arbitrary",), vmem_limit_bytes=_vmem_limit(est)),
    )(h, h, gn, w_in, cw, cb, wg, bg, lam)


def _odd_bwd_kernel(xc_ref, gate_ref, hf_ref, h_ref, wg_ref, bg_ref, lam_ref, w_ref, o_ref,
                    hst_ref, a_ref, b_ref, st_ref, *, tc, nt, final):
    j = pl.program_id(0)
    jj = nt - 1 - j
    tmr = tc * SUBLANE

    @pl.when(j == 0)
    def _():
        hst_ref[...] = jnp.zeros_like(hst_ref)

    xcb = xc_ref[...]
    _lru_coeffs(xcb.astype(F32), xcb, wg_ref, bg_ref, lam_ref, a_ref, b_ref)

    def step(s, hs):
        t = tc - 1 - s
        r = pl.multiple_of(t * SUBLANE, SUBLANE)
        hs = a_ref[pl.ds(r, SUBLANE), :] * hs + b_ref[pl.ds(r, SUBLANE), :]
        b_ref[pl.ds(r, SUBLANE), :] = hs
        return hs

    hst_ref[...] = lax.fori_loop(0, tc, step, hst_ref[...], unroll=8)

    y = ((hf_ref[...].astype(F32) + b_ref[...]) * _silu(gate_ref[...].astype(F32))).astype(BF16)
    d = jnp.dot(y, w_ref[...], preferred_element_type=F32)
    if not final:
        t_idx = jj * tc + (lax.broadcasted_iota(jnp.int32, (tmr, D_MODEL), 0) // SUBLANE)
        o_ref[...] = h_ref[...] + jnp.where(t_idx >= PAD_FRONT, d, 0.0).reshape(tc, SUBLANE, D_MODEL)
    else:
        @pl.when(jj >= BLOCK // tc)
        def _():
            res = h_ref[...].reshape(tmr, D_MODEL) + d
            for c in range(D_MODEL // LANE):
                st_ref[c] = res[:, c * LANE:(c + 1) * LANE]
                for b in range(SUBLANE):
                    o_ref[b, :, c * LANE:(c + 1) * LANE] = st_ref[c, pl.ds(b, tc, stride=SUBLANE), :]


def _odd_bwd(xc, gate, hf, h, wg, bg, lam, w_out, *, batch, tp, tc, final):
    nt = tp // tc
    tmr = tc * batch
    blk = lambda cols: pl.BlockSpec((tc, batch, cols), lambda j: (nt - 1 - j, 0, 0))
    if final:
        first = BLOCK // tc
        out_spec = pl.BlockSpec((batch, tc, D_MODEL), lambda j: (0, jnp.maximum(nt - 1 - j - first, 0), 0))
        out_shape = jax.ShapeDtypeStruct((batch, tp - BLOCK, D_MODEL), F32)
    else:
        out_spec = blk(D_MODEL)
        out_shape = jax.ShapeDtypeStruct((tp, batch, D_MODEL), F32)
    row_blk = pl.BlockSpec((tmr, LRU_WIDTH), lambda j: (nt - 1 - j, 0))
    est = (w_out.size * 2 + 3 * 2 * tmr * LRU_WIDTH * 2 + 4 * tmr * D_MODEL * 4
           + 2 * tmr * LRU_WIDTH * 4 + 5 * tmr * LRU_WIDTH * 4)
    return pl.pallas_call(
        functools.partial(_odd_bwd_kernel, tc=tc, nt=nt, final=final),
        name="odd_bwd",
        grid=(nt,),
        in_specs=[
            row_blk, row_blk, row_blk, blk(D_MODEL),
            _const_spec(wg.shape),
            _const_spec(bg.shape),
            _const_spec(lam.shape),
            _const_spec(w_out.shape),
        ],
        out_specs=out_spec,
        out_shape=out_shape,
        scratch_shapes=[pltpu.VMEM((SUBLANE, LRU_WIDTH), F32),
                        pltpu.VMEM((tmr, LRU_WIDTH), F32),
                        pltpu.VMEM((tmr, LRU_WIDTH), F32),
                        pltpu.VMEM((D_MODEL // LANE, tmr, LANE), F32)],
        compiler_params=pltpu.CompilerParams(
            dimension_semantics=("arbitrary",), vmem_limit_bytes=_vmem_limit(est)),
    )(xc, gate, hf, h, wg, bg, lam, w_out)


def _rope_tables(tp):
    pos = (jnp.arange(tp, dtype=F32) - PAD_FRONT)[:, None]

    def table(dim):
        inv = ROPE_THETA ** (-jnp.arange(0, dim, 2, dtype=F32) / dim)
        ang = pos * inv[None, :]
        return jnp.cos(ang), jnp.sin(ang)

    c, s = table(MLA_ROPE)
    zeros = jnp.zeros((tp, LANE - MLA_ROPE), F32)
    cm = jnp.concatenate([c, c, zeros], axis=1)
    sm = jnp.concatenate([-s, s, zeros], axis=1)
    c, s = table(SWA_HEAD_DIM)
    cs = jnp.concatenate([c, c], axis=1)
    ss = jnp.concatenate([-s, s], axis=1)
    return cm, sm, cs, ss


def _prep_even(w_in, w_uq, w_ukv, g_qn, g_kn):
    lat_w = MLA_Q_RANK + MLA_KV_RANK + MLA_ROPE
    w_all = jnp.concatenate(
        [w_in[:, :lat_w], jnp.zeros((D_MODEL, LAT_COLS - lat_w), w_in.dtype), w_in[:, lat_w:]], axis=1).astype(BF16)
    wuq = jnp.pad(w_uq, ((0, 0), (0, 0), (0, MLA_QK_PAD - MLA_QK))).reshape(
        MLA_Q_RANK, MLA_HEADS * MLA_QK_PAD).astype(BF16)
    wukv = w_ukv.reshape(MLA_KV_RANK, MLA_HEADS * (MLA_NOPE + MLA_V)).astype(BF16)
    pad_g = lambda g: jnp.pad(g, (0, MLA_QK_PAD - MLA_QK)).reshape(1, MLA_QK_PAD)
    return w_all, wuq, wukv, pad_g(g_qn), pad_g(g_kn)


def _prep_gates(w_a, b_a, w_x, b_x):
    wg = (0.5 * jnp.concatenate([w_a, w_x], axis=-1)).astype(BF16)
    bg = 0.5 * jnp.concatenate([b_a.reshape(LRU_BLOCKS, 1, LRU_BLOCK_DIM),
                                b_x.reshape(LRU_BLOCKS, 1, LRU_BLOCK_DIM)], axis=-1)
    return wg, bg


def _tile(tp, prefs):
    for t in prefs:
        if tp % t == 0:
            return t
    raise ValueError(f"no tile of {prefs} divides {tp}")


def kernel(x, meta_tokens, norm_g, even_w_in, mla_g_q_lat, mla_g_kv_lat, mla_w_uq, mla_w_ukv, mla_g_qn, mla_g_kn, swa_g_qn, swa_g_kn, swa_sink, even_w_out, odd_w_in, lru_conv_w, lru_conv_b, lru_w_a, lru_b_a, lru_w_x, lru_b_x, lru_lambda, odd_w_out):
    batch, seq, d = x.shape
    assert d == D_MODEL and batch == SUBLANE and seq % BLOCK == 0
    tp = BLOCK + seq
    tm = _tile(tp, (384, 128))
    tq = _tile(tp, (1056, 384, 128))
    tkc = _tile(seq, (512, 256, 128))
    assert seq // tkc >= 2, "the MLA key-chunk pipeline needs at least two chunks"
    tc = _tile(tp, (64, 32, 16))

    meta = jnp.broadcast_to(meta_tokens.astype(x.dtype)[None], (batch, N_META, D_MODEL))
    h = jnp.concatenate([jnp.zeros((batch, PAD_FRONT, D_MODEL), x.dtype), meta, x], axis=1)
    cm, sm, cs, ss = _rope_tables(tp)

    for l in range(DEPTH):
        j = l // 2
        gn = norm_g[l].reshape(1, D_MODEL)
        if l % 2 == 0:
            w_all, wuq, wukv, gqn, gkn = _prep_even(even_w_in[j], mla_w_uq[j], mla_w_ukv[j], mla_g_qn[j],
                                                    mla_g_kn[j])
            h2d = h if l == 0 else h.reshape(tp, batch * D_MODEL)
            qm, km, vm, qs, ks, vs, gate = _even_in(
                h2d, gn, w_all, mla_g_q_lat[j].reshape(1, -1), mla_g_kv_lat[j].reshape(1, -1), wuq, wukv,
                gqn, gkn, swa_g_qn[j].reshape(1, -1), swa_g_kn[j].reshape(1, -1), cm, sm, cs, ss,
                batch=batch, tp=tp, tm=tm)
            ya = _mla_attn(qm, km, vm, gate, batch=batch, tp=tp, tq=tq, tkc=tkc)
            sink_rows = jnp.repeat(swa_sink[j].astype(F32) * LOG2E, BLOCK).reshape(
                SWA_KV_HEADS, SWA_REP * BLOCK, 1)
            yb = _swa_attn(qs, ks, vs, sink_rows, gate, batch=batch, tp=tp)
            h2d = _even_out(ya, yb, even_w_out[j].astype(BF16), h2d, batch=batch, tp=tp, tm=tm)
            h = h2d.reshape(tp, batch, D_MODEL)
        else:
            wgf, bgf = _prep_gates(lru_w_a[j, 0], lru_b_a[j, 0], lru_w_x[j, 0], lru_b_x[j, 0])
            wgb, bgb = _prep_gates(lru_w_a[j, 1], lru_b_a[j, 1], lru_w_x[j, 1], lru_b_x[j, 1])
            xc, gate, hf = _odd_fwd(h, gn, odd_w_in[j].astype(BF16), lru_conv_w[j], lru_conv_b[j].reshape(1, -1),
                                    wgf, bgf, lru_lambda[j, 0].reshape(1, -1), batch=batch, tp=tp, tc=tc)
            h = _odd_bwd(xc, gate, hf, h, wgb, bgb, lru_lambda[j, 1].reshape(1, -1), odd_w_out[j].astype(BF16),
                         batch=batch, tp=tp, tc=tc, final=(l == DEPTH - 1))
    return h
```

```python
import functools
import math

import jax
import jax.numpy as jnp
from jax import lax
from jax.experimental import pallas as pl
from jax.experimental.pallas import tpu as pltpu

F32 = jnp.float32
BF16 = jnp.bfloat16

D_MODEL = 1024
DEPTH = 4
N_META = 16
D_INNER = 2 * D_MODEL
ROPE_THETA = 10000.0
EPS = 1e-6
NEG_INF = -1e30

MLA_HEADS = 8
MLA_Q_RANK = 384
MLA_KV_RANK = 256
MLA_NOPE = 128
MLA_ROPE = 64
MLA_QK = MLA_NOPE + MLA_ROPE
MLA_V = 128
MLA_WIDTH = MLA_HEADS * MLA_V

SWA_HEADS = 8
SWA_KV_HEADS = 2
SWA_REP = SWA_HEADS // SWA_KV_HEADS
SWA_HEAD_DIM = 128
SWA_WINDOW = 128
SWA_WIDTH = SWA_HEADS * SWA_HEAD_DIM
SWA_KV_WIDTH = SWA_KV_HEADS * SWA_HEAD_DIM

LRU_WIDTH = D_INNER
LRU_BLOCKS = 16
LRU_BLOCK_DIM = LRU_WIDTH // LRU_BLOCKS
LRU_C = 8.0
CONV_WIDTH = 4

LANE = 128
SUBLANE = 8
MXU_DIM = 256
V7X_VMEM_BYTES = 64 * 1024 * 1024

BLOCK = 128
PAD_FRONT = BLOCK - N_META
MLA_QK_PAD = MXU_DIM
LAT_COLS = 768
SWA_COLS = SWA_WIDTH + 2 * SWA_KV_WIDTH
LOG2E = 1.4426950408889634


def _vmem_limit(nbytes):
    return int(min(V7X_VMEM_BYTES - (4 << 20), max(nbytes, 16 << 20)))


def _const_spec(shape):
    nd = len(shape)
    return pl.BlockSpec(shape, lambda *_: (0,) * nd, pipeline_mode=pl.Buffered(1))


def _rms_scale(x, n):
    return lax.rsqrt(jnp.sum(x * x, axis=-1, keepdims=True) * (1.0 / n) + EPS)


def _silu(g):
    gh = 0.5 * g
    return gh * (1.0 + jnp.tanh(gh))


def _nt_dot(a, b):
    return lax.dot_general(a, b, (((1,), (1,)), ((), ())), preferred_element_type=F32)


def _even_in_kernel(h_ref, gn_ref, w_ref, gql_ref, gkvl_ref, wuq_ref, wukv_ref, gqn_ref, gkn_ref,
                    sgq_ref, sgk_ref, cm_ref, sm_ref, cs_ref, ss_ref,
                    qm_ref, km_ref, vm_ref, qs_ref, ks_ref, vs_ref, gate_ref, zp_a_ref, zp_b_ref):
    n = pl.program_id(0)

    @pl.when(n == 0)
    def _():
        zp_b_ref[...] = jnp.zeros_like(zp_b_ref)

    @pl.when(n % 2 == 0)
    def _():
        _even_in_step(h_ref, gn_ref, w_ref, gql_ref, gkvl_ref, wuq_ref, wukv_ref, gqn_ref, gkn_ref,
                      sgq_ref, sgk_ref, cm_ref, sm_ref, cs_ref, ss_ref,
                      qm_ref, km_ref, vm_ref, qs_ref, ks_ref, vs_ref, gate_ref, zp_a_ref, zp_b_ref)

    @pl.when(n % 2 == 1)
    def _():
        _even_in_step(h_ref, gn_ref, w_ref, gql_ref, gkvl_ref, wuq_ref, wukv_ref, gqn_ref, gkn_ref,
                      sgq_ref, sgk_ref, cm_ref, sm_ref, cs_ref, ss_ref,
                      qm_ref, km_ref, vm_ref, qs_ref, ks_ref, vs_ref, gate_ref, zp_b_ref, zp_a_ref)


def _even_in_step(h_ref, gn_ref, w_ref, gql_ref, gkvl_ref, wuq_ref, wukv_ref, gqn_ref, gkn_ref,
                  sgq_ref, sgk_ref, cm_ref, sm_ref, cs_ref, ss_ref,
                  qm_ref, km_ref, vm_ref, qs_ref, ks_ref, vs_ref, gate_ref, zp_new_ref, zp_old_ref):
    x = h_ref[...]
    z = (x * _rms_scale(x, D_MODEL) * gn_ref[...]).astype(BF16)
    zp_new_ref[...] = jnp.dot(z, w_ref[:, 0:LAT_COLS + SWA_COLS], preferred_element_type=F32)
    gate_ref[0] = jnp.dot(z, w_ref[:, LAT_COLS + SWA_COLS:], preferred_element_type=F32).astype(BF16)

    cq = zp_old_ref[:, 0:MLA_Q_RANK]
    ckv = zp_old_ref[:, MLA_Q_RANK:MLA_Q_RANK + MLA_KV_RANK]
    kpe = zp_old_ref[:, MLA_Q_RANK + MLA_KV_RANK:LAT_COLS]
    cqn = (cq * _rms_scale(cq, MLA_Q_RANK) * gql_ref[...]).astype(BF16)
    ckvn = (ckv * _rms_scale(ckv, MLA_KV_RANK) * gkvl_ref[...]).astype(BF16)

    cm = cm_ref[...]
    sm = sm_ref[...]

    def rope_mla(t):
        rot = pltpu.roll(t, MLA_ROPE // 2, 1) + pltpu.roll(t, LANE - MLA_ROPE // 2, 1)
        return t * cm + rot * sm

    q_scale = (MLA_QK ** -0.5) * LOG2E
    gqn = gqn_ref[...]
    q_all = jnp.dot(cqn, wuq_ref[...], preferred_element_type=F32)
    for hd in range(MLA_HEADS):
        qh = q_all[:, hd * MLA_QK_PAD:(hd + 1) * MLA_QK_PAD]
        qn = qh * _rms_scale(qh, MLA_QK) * gqn
        q_out = jnp.concatenate([qn[:, :MLA_NOPE], rope_mla(qn[:, MLA_NOPE:])], axis=-1) * q_scale
        qm_ref[0, hd] = q_out.astype(BF16)

    gkn = gkn_ref[...]
    kv = jnp.dot(ckvn, wukv_ref[...], preferred_element_type=F32)
    ss_pe = jnp.sum(kpe * kpe, axis=-1, keepdims=True)
    kr = rope_mla(kpe * gkn[:, MLA_NOPE:])
    for hd in range(MLA_HEADS):
        kn = kv[:, hd * MLA_QK_PAD:hd * MLA_QK_PAD + MLA_NOPE]
        rs = lax.rsqrt((jnp.sum(kn * kn, axis=-1, keepdims=True) + ss_pe) * (1.0 / MLA_QK) + EPS)
        k_out = jnp.concatenate([kn * rs * gkn[:, :MLA_NOPE], kr * rs], axis=-1)
        km_ref[0, hd] = k_out.astype(BF16)
        vm_ref[0, hd] = kv[:, hd * MLA_QK_PAD + MLA_NOPE:(hd + 1) * MLA_QK_PAD].astype(BF16)

    sw = zp_old_ref.at[:, LAT_COLS:LAT_COLS + SWA_COLS]
    cs = cs_ref[...]
    ss = ss_ref[...]

    def norm_rope_swa(t, g):
        n = t * _rms_scale(t, SWA_HEAD_DIM) * g
        return n * cs + pltpu.roll(n, SWA_HEAD_DIM // 2, 1) * ss

    s_scale = (SWA_HEAD_DIM ** -0.5) * LOG2E
    for hd in range(SWA_HEADS):
        t = sw[:, hd * SWA_HEAD_DIM:(hd + 1) * SWA_HEAD_DIM]
        qs_ref[0, hd] = (norm_rope_swa(t, sgq_ref[...]) * s_scale).astype(BF16)
    for hd in range(SWA_KV_HEADS):
        t = sw[:, SWA_WIDTH + hd * SWA_HEAD_DIM:SWA_WIDTH + (hd + 1) * SWA_HEAD_DIM]
        ks_ref[0, hd] = norm_rope_swa(t, sgk_ref[...]).astype(BF16)
        vs_ref[0, hd] = sw[:, SWA_WIDTH + SWA_KV_WIDTH + hd * SWA_HEAD_DIM:
                           SWA_WIDTH + SWA_KV_WIDTH + (hd + 1) * SWA_HEAD_DIM].astype(BF16)


def _h_tile_spec(h, tm, tile_of=lambda b, i: (b, i)):
    if h.ndim == 3:
        return pl.BlockSpec((None, tm, D_MODEL), lambda *g: (*tile_of(*g), 0))
    return pl.BlockSpec((tm, D_MODEL), lambda *g: tile_of(*g)[::-1])


def _even_in(h2d, gn, w_all, gql, gkvl, wuq, wukv, gqn, gkn, sgq, sgk, cm, sm, cs, ss, *, batch, tp, tm):
    nt = tp // tm
    n_tiles = batch * nt
    new_tile = lambda n: divmod(jnp.minimum(n, n_tiles - 1), nt)
    old_tile = lambda n: divmod(jnp.maximum(n - 1, 0), nt)
    row_spec = lambda cols: pl.BlockSpec((tm, cols), lambda n: (old_tile(n)[1], 0))
    head_out = lambda heads, cols: pl.BlockSpec(
        (1, heads, tm, cols), lambda n: (old_tile(n)[0], 0, old_tile(n)[1], 0))
    n_w = w_all.shape[1]
    est = (w_all.size * 2 + (wuq.size + wukv.size) * 2 + 2 * tm * D_MODEL * 4
           + 2 * tm * 2 * (2 * MLA_HEADS * MLA_QK_PAD + MLA_WIDTH + SWA_WIDTH + 2 * SWA_KV_WIDTH)
           + 2 * tm * D_INNER * 2 + 2 * tm * (LAT_COLS + SWA_COLS) * 4 + 3 * tm * n_w * 4)
    return pl.pallas_call(
        _even_in_kernel,
        name="even_in",
        grid=(n_tiles + 1,),
        in_specs=[
            _h_tile_spec(h2d, tm, new_tile),
            _const_spec((1, D_MODEL)),
            _const_spec(w_all.shape),
            _const_spec((1, MLA_Q_RANK)),
            _const_spec((1, MLA_KV_RANK)),
            _const_spec(wuq.shape),
            _const_spec(wukv.shape),
            _const_spec((1, MLA_QK_PAD)),
            _const_spec((1, MLA_QK_PAD)),
            _const_spec((1, SWA_HEAD_DIM)),
            _const_spec((1, SWA_HEAD_DIM)),
            row_spec(LANE), row_spec(LANE), row_spec(LANE), row_spec(LANE),
        ],
        out_specs=[
            head_out(MLA_HEADS, MLA_QK_PAD),
            head_out(MLA_HEADS, MLA_QK_PAD),
            head_out(MLA_HEADS, MLA_V),
            head_out(SWA_HEADS, SWA_HEAD_DIM),
            head_out(SWA_KV_HEADS, SWA_HEAD_DIM),
            head_out(SWA_KV_HEADS, SWA_HEAD_DIM),
            pl.BlockSpec((1, tm, D_INNER), lambda n: (*new_tile(n), 0)),
        ],
        out_shape=[
            jax.ShapeDtypeStruct((batch, MLA_HEADS, tp, MLA_QK_PAD), BF16),
            jax.ShapeDtypeStruct((batch, MLA_HEADS, tp, MLA_QK_PAD), BF16),
            jax.ShapeDtypeStruct((batch, MLA_HEADS, tp, MLA_V), BF16),
            jax.ShapeDtypeStruct((batch, SWA_HEADS, tp, SWA_HEAD_DIM), BF16),
            jax.ShapeDtypeStruct((batch, SWA_KV_HEADS, tp, SWA_HEAD_DIM), BF16),
            jax.ShapeDtypeStruct((batch, SWA_KV_HEADS, tp, SWA_HEAD_DIM), BF16),
            jax.ShapeDtypeStruct((batch, tp, D_INNER), BF16),
        ],
        scratch_shapes=[pltpu.VMEM((tm, LAT_COLS + SWA_COLS), F32),
                        pltpu.VMEM((tm, LAT_COLS + SWA_COLS), F32)],
        compiler_params=pltpu.CompilerParams(
            dimension_semantics=("arbitrary",), vmem_limit_bytes=_vmem_limit(est)),
    )(h2d, gn, w_all, gql, gkvl, wuq, wukv, gqn, gkn, sgq, sgk, cm, sm, cs, ss)


def _mla_attn_kernel(q_ref, k_ref, v_ref, gate_ref, o_ref, vx_ref, *, tp, tkc):
    q = q_ref[0, 0]
    tq = q.shape[0]

    @pl.when(pl.program_id(2) == 0)
    def _():
        lane = lax.broadcasted_iota(jnp.int32, (tp, MXU_DIM - MLA_V), 1)
        vx_ref[:, 0:MLA_V] = v_ref[0, 0]
        vx_ref[:, MLA_V:] = jnp.where(lane == 0, 1.0, 0.0).astype(BF16)

    s = _nt_dot(q, k_ref[0, 0, 0:BLOCK, :])
    col = lax.broadcasted_iota(jnp.int32, (tq, BLOCK), 1)
    s = jnp.where(col >= PAD_FRONT, s, NEG_INF)
    m = jnp.max(s, axis=-1, keepdims=True)
    p = jnp.exp2(s - m)
    acc = jnp.dot(p.astype(BF16), vx_ref[0:BLOCK, :], preferred_element_type=F32)

    n_chunks = (tp - BLOCK) // tkc

    def scores(c):
        start = pl.multiple_of(BLOCK + c * tkc, BLOCK)
        return _nt_dot(q, k_ref[0, 0, pl.ds(start, tkc), :])

    def softmax_step(s, m):
        m_new = jnp.maximum(m, jnp.max(s, axis=-1, keepdims=True))
        alpha = jnp.exp2(m - m_new)
        p = jnp.exp2(s - m_new)
        return p.astype(BF16), alpha, m_new

    def pv_step(c, p, alpha, acc):
        start = pl.multiple_of(BLOCK + c * tkc, BLOCK)
        return alpha * acc + jnp.dot(p, vx_ref[pl.ds(start, tkc), :], preferred_element_type=F32)

    def chunk(c, carry):
        s, p_prev, alpha_prev, m, acc = carry
        s_next = scores(c + 1)
        p, alpha, m = softmax_step(s, m)
        acc = pv_step(c - 1, p_prev, alpha_prev, acc)
        return s_next, p, alpha, m, acc

    s1 = scores(1)
    p0, alpha0, m = softmax_step(scores(0), m)
    carry = (s1, p0, alpha0, m, acc)
    for c in range(1, n_chunks - 1):
        carry = chunk(c, carry)
    s, p_prev, alpha_prev, m, acc = carry
    p, alpha, m = softmax_step(s, m)
    acc = pv_step(n_chunks - 2, p_prev, alpha_prev, acc)
    acc = pv_step(n_chunks - 1, p, alpha, acc)
    o = acc[:, 0:MLA_V] / acc[:, MLA_V:MLA_V + 1]
    o_ref[0] = (o * _silu(gate_ref[0].astype(F32))).astype(BF16)


def _mla_attn(qm, km, vm, gate, *, batch, tp, tq, tkc):
    nq = tp // tq
    est = (2 * tp * (MLA_QK_PAD + MLA_V) * 2 + tp * MXU_DIM * 2 + 6 * tq * tkc * 4
           + 4 * tq * MLA_QK_PAD * 2)
    return pl.pallas_call(
        functools.partial(_mla_attn_kernel, tp=tp, tkc=tkc),
        name="mla_attn",
        grid=(batch, MLA_HEADS, nq),
        in_specs=[
            pl.BlockSpec((1, 1, tq, MLA_QK_PAD), lambda b, h, i: (b, h, i, 0)),
            pl.BlockSpec((1, 1, tp, MLA_QK_PAD), lambda b, h, i: (b, h, 0, 0)),
            pl.BlockSpec((1, 1, tp, MLA_V), lambda b, h, i: (b, h, 0, 0)),
            pl.BlockSpec((1, tq, MLA_V), lambda b, h, i: (b, i, h)),
        ],
        out_specs=pl.BlockSpec((1, tq, MLA_V), lambda b, h, i: (b, i, h)),
        out_shape=jax.ShapeDtypeStruct((batch, tp, MLA_WIDTH), BF16),
        scratch_shapes=[pltpu.VMEM((tp, MXU_DIM), BF16)],
        compiler_params=pltpu.CompilerParams(
            dimension_semantics=("arbitrary", "arbitrary", "arbitrary"), vmem_limit_bytes=_vmem_limit(est)),
    )(qm, km, vm, gate)


def _swa_masks(tp):
    band = 3 * BLOCK
    nb = tp // BLOCK
    r = (jnp.arange(SWA_REP * BLOCK, dtype=jnp.int32) % BLOCK)[:, None]
    c = jnp.arange(band, dtype=jnp.int32)[None, :]

    def mask(n):
        start = min(max((n - 1) * BLOCK, 0), tp - band)
        rq, rk = n * BLOCK + r, start + c
        return jnp.where((jnp.abs(rq - rk) <= SWA_WINDOW) & (rk >= BLOCK), 0.0, NEG_INF).astype(F32)

    band_bias = jnp.stack([mask(2), mask(0), mask(1), mask(nb - 1)])
    meta_bias = jnp.where(jnp.arange(BLOCK) >= PAD_FRONT, 0.0, NEG_INF).astype(F32)
    return band_bias, jnp.broadcast_to(meta_bias[None, :], (SWA_REP * BLOCK, BLOCK))


def _swa_attn_kernel(q_ref, k_ref, v_ref, sink_ref, gate_ref, bias_ref, mbias_ref, o_ref, vx_ref, *, tp, nblk):
    step = pl.program_id(2)
    rows = SWA_REP * BLOCK
    band = 3 * BLOCK

    @pl.when(step == 0)
    def _():
        lane = lax.broadcasted_iota(jnp.int32, (tp, MXU_DIM - SWA_HEAD_DIM), 1)
        vx_ref[:, 0:SWA_HEAD_DIM] = v_ref[0, 0]
        vx_ref[:, SWA_HEAD_DIM:] = jnp.where(lane == 0, 1.0, 0.0).astype(BF16)

    sink = sink_ref[0]
    g = _silu(gate_ref[0].astype(F32))
    starts, scores = [], []
    for u in range(nblk):
        n = step * nblk + u
        q = q_ref[0, :, u * BLOCK:(u + 1) * BLOCK, :].reshape(rows, SWA_HEAD_DIM)
        start = pl.multiple_of(jnp.clip((n - 1) * BLOCK, 0, tp - band), BLOCK)
        kind = jnp.where(n == 0, 1, jnp.where(n == 1, 2, jnp.where(n == tp // BLOCK - 1, 3, 0)))
        s_b = _nt_dot(q, k_ref[0, 0, pl.ds(start, band), :]) + bias_ref[kind]
        s_m = _nt_dot(q, k_ref[0, 0, 0:BLOCK, :]) + mbias_ref[...]
        starts.append(start)
        scores.append((s_b, s_m))

    for u in range(nblk):
        s_b, s_m = scores[u]
        mx = jnp.maximum(jnp.maximum(s_b[:, 0:BLOCK], s_b[:, BLOCK:2 * BLOCK]),
                         jnp.maximum(s_b[:, 2 * BLOCK:], s_m))
        m = jnp.maximum(jnp.max(mx, axis=-1, keepdims=True), sink)
        p_b = jnp.exp2(s_b - m).astype(BF16)
        p_m = jnp.exp2(s_m - m).astype(BF16)
        acc = (jnp.dot(p_b, vx_ref[pl.ds(starts[u], band), :], preferred_element_type=F32)
               + jnp.dot(p_m, vx_ref[0:BLOCK, :], preferred_element_type=F32))
        l = jnp.exp2(sink - m) + acc[:, SWA_HEAD_DIM:SWA_HEAD_DIM + 1]
        o = acc[:, 0:SWA_HEAD_DIM] / l
        for r in range(SWA_REP):
            o_ref[0, u * BLOCK:(u + 1) * BLOCK, r * SWA_HEAD_DIM:(r + 1) * SWA_HEAD_DIM] = (
                o[r * BLOCK:(r + 1) * BLOCK]
                * g[u * BLOCK:(u + 1) * BLOCK, r * SWA_HEAD_DIM:(r + 1) * SWA_HEAD_DIM]).astype(BF16)


def _swa_attn(qs, ks, vs, sink_rows, gate, *, batch, tp):
    nb = tp // BLOCK
    assert nb >= 4, "the four band-mask kinds assume at least four token blocks"
    nblk = 3 if nb % 3 == 0 else 1
    rows_q = nblk * BLOCK
    band_bias, meta_bias = _swa_masks(tp)
    gcols = SWA_REP * SWA_HEAD_DIM
    est = 4 * tp * SWA_HEAD_DIM * 2 + tp * MXU_DIM * 2 + nblk * 12 * SWA_REP * BLOCK * 4 * BLOCK * 4
    return pl.pallas_call(
        functools.partial(_swa_attn_kernel, tp=tp, nblk=nblk),
        name="swa_attn",
        grid=(batch, SWA_KV_HEADS, nb // nblk),
        in_specs=[
            pl.BlockSpec((1, SWA_REP, rows_q, SWA_HEAD_DIM), lambda b, g, n: (b, g, n, 0)),
            pl.BlockSpec((1, 1, tp, SWA_HEAD_DIM), lambda b, g, n: (b, g, 0, 0)),
            pl.BlockSpec((1, 1, tp, SWA_HEAD_DIM), lambda b, g, n: (b, g, 0, 0)),
            pl.BlockSpec((1, SWA_REP * BLOCK, 1), lambda b, g, n: (g, 0, 0)),
            pl.BlockSpec((1, rows_q, gcols), lambda b, g, n: (b, n, MLA_WIDTH // gcols + g)),
            _const_spec(band_bias.shape),
            _const_spec(meta_bias.shape),
        ],
        out_specs=pl.BlockSpec((1, rows_q, gcols), lambda b, g, n: (b, n, g)),
        out_shape=jax.ShapeDtypeStruct((batch, tp, SWA_WIDTH), BF16),
        scratch_shapes=[pltpu.VMEM((tp, MXU_DIM), BF16)],
        compiler_params=pltpu.CompilerParams(
            dimension_semantics=("arbitrary", "arbitrary", "arbitrary"), vmem_limit_bytes=_vmem_limit(est)),
    )(qs, ks, vs, sink_rows, gate, band_bias, meta_bias)


def _even_out_kernel(ya_ref, yb_ref, w_ref, h_ref, o_ref, *, tm):
    i = pl.program_id(1)
    d = (jnp.dot(ya_ref[0], w_ref[0:MLA_WIDTH, :], preferred_element_type=F32)
         + jnp.dot(yb_ref[0], w_ref[MLA_WIDTH:, :], preferred_element_type=F32))
    row = i * tm + lax.broadcasted_iota(jnp.int32, (tm, D_MODEL), 0)
    o_ref[...] = h_ref[...] + jnp.where(row >= PAD_FRONT, d, 0.0)


def _even_out(ya, yb, w_out, h2d, *, batch, tp, tm):
    nt = tp // tm
    est = w_out.size * 2 + 4 * tm * D_MODEL * 2 + 6 * tm * D_MODEL * 4
    return pl.pallas_call(
        functools.partial(_even_out_kernel, tm=tm),
        name="even_out",
        grid=(batch, nt),
        in_specs=[
            pl.BlockSpec((1, tm, MLA_WIDTH), lambda b, i: (b, i, 0)),
            pl.BlockSpec((1, tm, SWA_WIDTH), lambda b, i: (b, i, 0)),
            _const_spec(w_out.shape),
            _h_tile_spec(h2d, tm),
        ],
        out_specs=pl.BlockSpec((tm, D_MODEL), lambda b, i: (i, b)),
        out_shape=jax.ShapeDtypeStruct((tp, batch * D_MODEL), F32),
        compiler_params=pltpu.CompilerParams(
            dimension_semantics=("arbitrary", "arbitrary"), vmem_limit_bytes=_vmem_limit(est)),
    )(ya, yb, w_out, h2d)


def _lru_coeffs(xc, xcb, wg_ref, bg_ref, lam_ref, a_ref, b_ref, t0=None):
    tmr = xc.shape[0]
    if t0 is not None:
        t_idx = t0 + (lax.broadcasted_iota(jnp.int32, (tmr, LRU_BLOCK_DIM), 0) // SUBLANE)
        live = t_idx >= PAD_FRONT
    for n in range(LRU_BLOCKS):
        sl = slice(n * LRU_BLOCK_DIM, (n + 1) * LRU_BLOCK_DIM)
        xn = xc[:, sl]
        gh = jnp.dot(xcb[:, sl], wg_ref[n], preferred_element_type=F32) + bg_ref[n]
        tr = jnp.tanh(gh[:, :LRU_BLOCK_DIM])
        ti = jnp.tanh(gh[:, LRU_BLOCK_DIM:])
        ch = (-0.5 * LRU_C) * jax.nn.softplus(-lam_ref[:, sl])
        log_a = tr * ch + ch
        a = jnp.exp(log_a)
        a_ref[:, sl] = a
        one_m_a2 = jnp.tanh(log_a) * (-1.0 - a * a)
        b = jnp.sqrt(one_m_a2) * (0.5 * ti + 0.5) * xn
        b_ref[:, sl] = b if t0 is None else jnp.where(live, b, 0.0)


def _odd_fwd_kernel(h_ref, halo_ref, gn_ref, w_ref, cw_ref, cb_ref, wg_ref, bg_ref, lam_ref,
                    xc_ref, gate_ref, hf_ref,
                    uprev_ref, hst_ref, a_ref, b_ref, *, tc, nt):
    j = pl.program_id(0)
    tmr = tc * SUBLANE
    halo_rows = SUBLANE
    prev_rows = 2 * SUBLANE

    @pl.when(j == 0)
    def _():
        uprev_ref[...] = jnp.zeros_like(uprev_ref)
        hst_ref[...] = jnp.zeros_like(hst_ref)

    x = jnp.concatenate([h_ref[...].reshape(tmr, D_MODEL), halo_ref[...].reshape(halo_rows, D_MODEL)], axis=0)
    z = (x * _rms_scale(x, D_MODEL) * gn_ref[...]).astype(BF16)
    u = jnp.dot(z, w_ref[:, 0:LRU_WIDTH], preferred_element_type=F32)
    gate_ref[...] = jnp.dot(z[0:tmr], w_ref[:, LRU_WIDTH:], preferred_element_type=F32).astype(BF16)

    u_next = jnp.where(j == nt - 1, 0.0, u[tmr:])
    ue = jnp.concatenate([uprev_ref[...], u[0:tmr], u_next], axis=0)
    uprev_ref[...] = u[tmr - prev_rows:tmr]
    xc = cb_ref[...] + ue[0:tmr] * cw_ref[0:1, :]
    for tap in range(1, CONV_WIDTH):
        xc = xc + ue[tap * SUBLANE:tap * SUBLANE + tmr] * cw_ref[tap:tap + 1, :]
    xcb = xc.astype(BF16)
    xc_ref[...] = xcb

    _lru_coeffs(xc, xcb, wg_ref, bg_ref, lam_ref, a_ref, b_ref, j * tc)

    def step(t, hs):
        r = pl.multiple_of(t * SUBLANE, SUBLANE)
        hs = a_ref[pl.ds(r, SUBLANE), :] * hs + b_ref[pl.ds(r, SUBLANE), :]
        b_ref[pl.ds(r, SUBLANE), :] = hs
        return hs

    hst_ref[...] = lax.fori_loop(0, tc, step, hst_ref[...], unroll=8)
    hf_ref[...] = b_ref[...].astype(BF16)


def _odd_fwd(h, gn, w_in, cw, cb, wg, bg, lam, *, batch, tp, tc):
    nt = tp // tc
    tmr = tc * batch
    blk = lambda cols: pl.BlockSpec((tc, batch, cols), lambda j: (j, 0, 0))
    row_blk = pl.BlockSpec((tmr, LRU_WIDTH), lambda j: (j, 0))
    est = (w_in.size * 2 + 2 * tmr * D_MODEL * 4 + 3 * 2 * tmr * LRU_WIDTH * 2
           + 2 * tmr * LRU_WIDTH * 4 + 5 * tmr * LRU_WIDTH * 4)
    return pl.pallas_call(
        functools.partial(_odd_fwd_kernel, tc=tc, nt=nt),
        name="odd_fwd",
        grid=(nt,),
        in_specs=[
            blk(D_MODEL),
            pl.BlockSpec((1, batch, D_MODEL), lambda j: (jnp.minimum((j + 1) * tc, tp - 1), 0, 0)),
            _const_spec((1, D_MODEL)),
            _const_spec(w_in.shape),
            _const_spec(cw.shape),
            _const_spec(cb.shape),
            _const_spec(wg.shape),
            _const_spec(bg.shape),
            _const_spec(lam.shape),
        ],
        out_specs=[row_blk, row_blk, row_blk],
        out_shape=[jax.ShapeDtypeStruct((tp * batch, LRU_WIDTH), BF16)] * 3,
        scratch_shapes=[pltpu.VMEM((2 * SUBLANE, LRU_WIDTH), F32),
                        pltpu.VMEM((SUBLANE, LRU_WIDTH), F32),
                        pltpu.VMEM((tmr, LRU_WIDTH), F32),
                        pltpu.VMEM((tmr, LRU_WIDTH), F32)],
        compiler_params=pltpu.CompilerParams(
            dimension_semantics=("arbitrary",), vmem_limit_bytes=_vmem_limit(est)),
    )(h, h, gn, w_in, cw, cb, wg, bg, lam)


def _odd_bwd_kernel(xc_ref, gate_ref, hf_ref, h_ref, wg_ref, bg_ref, lam_ref, w_ref, o_ref,
                    hst_ref, a_ref, b_ref, st_ref, *, tc, nt, final):
    j = pl.program_id(0)
    jj = nt - 1 - j
    tmr = tc * SUBLANE

    @pl.when(j == 0)
    def _():
        hst_ref[...] = jnp.zeros_like(hst_ref)

    xcb = xc_ref[...]
    _lru_coeffs(xcb.astype(F32), xcb, wg_ref, bg_ref, lam_ref, a_ref, b_ref)

    def step(s, hs):
        t = tc - 1 - s
        r = pl.multiple_of(t * SUBLANE, SUBLANE)
        hs = a_ref[pl.ds(r, SUBLANE), :] * hs + b_ref[pl.ds(r, SUBLANE), :]
        b_ref[pl.ds(r, SUBLANE), :] = hs
        return hs

    hst_ref[...] = lax.fori_loop(0, tc, step, hst_ref[...], unroll=8)

    y = ((hf_ref[...].astype(F32) + b_ref[...]) * _silu(gate_ref[...].astype(F32))).astype(BF16)
    d = jnp.dot(y, w_ref[...], preferred_element_type=F32)
    if not final:
        t_idx = jj * tc + (lax.broadcasted_iota(jnp.int32, (tmr, D_MODEL), 0) // SUBLANE)
        o_ref[...] = h_ref[...] + jnp.where(t_idx >= PAD_FRONT, d, 0.0).reshape(tc, SUBLANE, D_MODEL)
    else:
        @pl.when(jj >= BLOCK // tc)
        def _():
            res = h_ref[...].reshape(tmr, D_MODEL) + d
            for c in range(D_MODEL // LANE):
                st_ref[c] = res[:, c * LANE:(c + 1) * LANE]
                for b in range(SUBLANE):
                    o_ref[b, :, c * LANE:(c + 1) * LANE] = st_ref[c, pl.ds(b, tc, stride=SUBLANE), :]


def _odd_bwd(xc, gate, hf, h, wg, bg, lam, w_out, *, batch, tp, tc, final):
    nt = tp // tc
    tmr = tc * batch
    blk = lambda cols: pl.BlockSpec((tc, batch, cols), lambda j: (nt - 1 - j, 0, 0))
    if final:
        first = BLOCK // tc
        out_spec = pl.BlockSpec((batch, tc, D_MODEL), lambda j: (0, jnp.maximum(nt - 1 - j - first, 0), 0))
        out_shape = jax.ShapeDtypeStruct((batch, tp - BLOCK, D_MODEL), F32)
    else:
        out_spec = blk(D_MODEL)
        out_shape = jax.ShapeDtypeStruct((tp, batch, D_MODEL), F32)
    row_blk = pl.BlockSpec((tmr, LRU_WIDTH), lambda j: (nt - 1 - j, 0))
    est = (w_out.size * 2 + 3 * 2 * tmr * LRU_WIDTH * 2 + 4 * tmr * D_MODEL * 4
           + 2 * tmr * LRU_WIDTH * 4 + 5 * tmr * LRU_WIDTH * 4)
    return pl.pallas_call(
        functools.partial(_odd_bwd_kernel, tc=tc, nt=nt, final=final),
        name="odd_bwd",
        grid=(nt,),
        in_specs=[
            row_blk, row_blk, row_blk, blk(D_MODEL),
            _const_spec(wg.shape),
            _const_spec(bg.shape),
            _const_spec(lam.shape),
            _const_spec(w_out.shape),
        ],
        out_specs=out_spec,
        out_shape=out_shape,
        scratch_shapes=[pltpu.VMEM((SUBLANE, LRU_WIDTH), F32),
                        pltpu.VMEM((tmr, LRU_WIDTH), F32),
                        pltpu.VMEM((tmr, LRU_WIDTH), F32),
                        pltpu.VMEM((D_MODEL // LANE, tmr, LANE), F32)],
        compiler_params=pltpu.CompilerParams(
            dimension_semantics=("arbitrary",), vmem_limit_bytes=_vmem_limit(est)),
    )(xc, gate, hf, h, wg, bg, lam, w_out)


def _rope_tables(tp):
    pos = (jnp.arange(tp, dtype=F32) - PAD_FRONT)[:, None]

    def table(dim):
        inv = ROPE_THETA ** (-jnp.arange(0, dim, 2, dtype=F32) / dim)
        ang = pos * inv[None, :]
        return jnp.cos(ang), jnp.sin(ang)

    c, s = table(MLA_ROPE)
    zeros = jnp.zeros((tp, LANE - MLA_ROPE), F32)
    cm = jnp.concatenate([c, c, zeros], axis=1)
    sm = jnp.concatenate([-s, s, zeros], axis=1)
    c, s = table(SWA_HEAD_DIM)
    cs = jnp.concatenate([c, c], axis=1)
    ss = jnp.concatenate([-s, s], axis=1)
    return cm, sm, cs, ss


def _prep_even(w_in, w_uq, w_ukv, g_qn, g_kn):
    lat_w = MLA_Q_RANK + MLA_KV_RANK + MLA_ROPE
    w_all = jnp.concatenate(
        [w_in[:, :lat_w], jnp.zeros((D_MODEL, LAT_COLS - lat_w), w_in.dtype), w_in[:, lat_w:]], axis=1).astype(BF16)
    wuq = jnp.pad(w_uq, ((0, 0), (0, 0), (0, MLA_QK_PAD - MLA_QK))).reshape(
        MLA_Q_RANK, MLA_HEADS * MLA_QK_PAD).astype(BF16)
    wukv = w_ukv.reshape(MLA_KV_RANK, MLA_HEADS * (MLA_NOPE + MLA_V)).astype(BF16)
    pad_g = lambda g: jnp.pad(g, (0, MLA_QK_PAD - MLA_QK)).reshape(1, MLA_QK_PAD)
    return w_all, wuq, wukv, pad_g(g_qn), pad_g(g_kn)


def _prep_gates(w_a, b_a, w_x, b_x):
    wg = (0.5 * jnp.concatenate([w_a, w_x], axis=-1)).astype(BF16)
    bg = 0.5 * jnp.concatenate([b_a.reshape(LRU_BLOCKS, 1, LRU_BLOCK_DIM),
                                b_x.reshape(LRU_BLOCKS, 1, LRU_BLOCK_DIM)], axis=-1)
    return wg, bg


def _tile(tp, prefs):
    for t in prefs:
        if tp % t == 0:
            return t
    raise ValueError(f"no tile of {prefs} divides {tp}")


def kernel(x, meta_tokens, norm_g, even_w_in, mla_g_q_lat, mla_g_kv_lat, mla_w_uq, mla_w_ukv, mla_g_qn, mla_g_kn, swa_g_qn, swa_g_kn, swa_sink, even_w_out, odd_w_in, lru_conv_w, lru_conv_b, lru_w_a, lru_b_a, lru_w_x, lru_b_x, lru_lambda, odd_w_out):
    batch, seq, d = x.shape
    assert d == D_MODEL and batch == SUBLANE and seq % BLOCK == 0
    tp = BLOCK + seq
    tm = _tile(tp, (384, 128))
    tq = _tile(tp, (1056, 384, 128))
    tkc = _tile(seq, (512, 256, 128))
    assert seq // tkc >= 2, "the MLA key-chunk pipeline needs at least two chunks"
    tc = _tile(tp, (64, 32, 16))

    meta = jnp.broadcast_to(meta_tokens.astype(x.dtype)[None], (batch, N_META, D_MODEL))
    h = jnp.concatenate([jnp.zeros((batch, PAD_FRONT, D_MODEL), x.dtype), meta, x], axis=1)
    cm, sm, cs, ss = _rope_tables(tp)

    for l in range(DEPTH):
        j = l // 2
        gn = norm_g[l].reshape(1, D_MODEL)
        if l % 2 == 0:
            w_all, wuq, wukv, gqn, gkn = _prep_even(even_w_in[j], mla_w_uq[j], mla_w_ukv[j], mla_g_qn[j],
                                                    mla_g_kn[j])
            h2d = h if l == 0 else h.reshape(tp, batch * D_MODEL)
            qm, km, vm, qs, ks, vs, gate = _even_in(
                h2d, gn, w_all, mla_g_q_lat[j].reshape(1, -1), mla_g_kv_lat[j].reshape(1, -1), wuq, wukv,
                gqn, gkn, swa_g_qn[j].reshape(1, -1), swa_g_kn[j].reshape(1, -1), cm, sm, cs, ss,
                batch=batch, tp=tp, tm=tm)
            ya = _mla_attn(qm, km, vm, gate, batch=batch, tp=tp, tq=tq, tkc=tkc)
            sink_rows = jnp.repeat(swa_sink[j].astype(F32) * LOG2E, BLOCK).reshape(
                SWA_KV_HEADS, SWA_REP * BLOCK, 1)
            yb = _swa_attn(qs, ks, vs, sink_rows, gate, batch=batch, tp=tp)
            h2d = _even_out(ya, yb, even_w_out[j].astype(BF16), h2d, batch=batch, tp=tp, tm=tm)
            h = h2d.reshape(tp, batch, D_MODEL)
        else:
            wgf, bgf = _prep_gates(lru_w_a[j, 0], lru_b_a[j, 0], lru_w_x[j, 0], lru_b_x[j, 0])
            wgb, bgb = _prep_gates(lru_w_a[j, 1], lru_b_a[j, 1], lru_w_x[j, 1], lru_b_x[j, 1])
            xc, gate, hf = _odd_fwd(h, gn, odd_w_in[j].astype(BF16), lru_conv_w[j], lru_conv_b[j].reshape(1, -1),
                                    wgf, bgf, lru_lambda[j, 0].reshape(1, -1), batch=batch, tp=tp, tc=tc)
            h = _odd_bwd(xc, gate, hf, h, wgb, bgb, lru_lambda[j, 1].reshape(1, -1), odd_w_out[j].astype(BF16),
                         batch=batch, tp=tp, tc=tc, final=(l == DEPTH - 1))
    return h
```

```python
import functools
import math

import jax
import jax.numpy as jnp
from jax import lax
from jax.experimental import pallas as pl
from jax.experimental.pallas import tpu as pltpu

F32 = jnp.float32
BF16 = jnp.bfloat16

D_MODEL = 1024
DEPTH = 4
N_META = 16
D_INNER = 2 * D_MODEL
ROPE_THETA = 10000.0
EPS = 1e-6
NEG_INF = -1e30

MLA_HEADS = 8
MLA_Q_RANK = 384
MLA_KV_RANK = 256
MLA_NOPE = 128
MLA_ROPE = 64
MLA_QK = MLA_NOPE + MLA_ROPE
MLA_V = 128
MLA_WIDTH = MLA_HEADS * MLA_V

SWA_HEADS = 8
SWA_KV_HEADS = 2
SWA_REP = SWA_HEADS // SWA_KV_HEADS
SWA_HEAD_DIM = 128
SWA_WINDOW = 128
SWA_WIDTH = SWA_HEADS * SWA_HEAD_DIM
SWA_KV_WIDTH = SWA_KV_HEADS * SWA_HEAD_DIM

LRU_WIDTH = D_INNER
LRU_BLOCKS = 16
LRU_BLOCK_DIM = LRU_WIDTH // LRU_BLOCKS
LRU_C = 8.0
CONV_WIDTH = 4

LANE = 128
SUBLANE = 8
MXU_DIM = 256
V7X_VMEM_BYTES = 64 * 1024 * 1024

BLOCK = 128
PAD_FRONT = BLOCK - N_META
MLA_QK_PAD = MXU_DIM
LAT_COLS = 768
SWA_COLS = SWA_WIDTH + 2 * SWA_KV_WIDTH
LOG2E = 1.4426950408889634
MLA_FIXED_SHIFT_MAX = 40.0


def _vmem_limit(nbytes):
    return int(min(V7X_VMEM_BYTES - (4 << 20), max(nbytes, 16 << 20)))


def _const_spec(shape):
    nd = len(shape)
    return pl.BlockSpec(shape, lambda *_: (0,) * nd, pipeline_mode=pl.Buffered(1))


def _rms_scale(x, n):
    return lax.rsqrt(jnp.sum(x * x, axis=-1, keepdims=True) * (1.0 / n) + EPS)


def _silu(g):
    gh = 0.5 * g
    return gh * (1.0 + jnp.tanh(gh))


def _nt_dot(a, b):
    return lax.dot_general(a, b, (((1,), (1,)), ((), ())), preferred_element_type=F32)


def _even_in_kernel(h_ref, gn_ref, w_ref, gql_ref, gkvl_ref, wuq_ref, wukv_ref, gqn_ref, gkn_ref,
                    sgq_ref, sgk_ref, cm_ref, sm_ref, cs_ref, ss_ref,
                    qm_ref, km_ref, vm_ref, qs_ref, ks_ref, vs_ref, gate_ref, zp_a_ref, zp_b_ref):
    n = pl.program_id(0)

    @pl.when(n == 0)
    def _():
        zp_b_ref[...] = jnp.zeros_like(zp_b_ref)

    @pl.when(n % 2 == 0)
    def _():
        _even_in_step(h_ref, gn_ref, w_ref, gql_ref, gkvl_ref, wuq_ref, wukv_ref, gqn_ref, gkn_ref,
                      sgq_ref, sgk_ref, cm_ref, sm_ref, cs_ref, ss_ref,
                      qm_ref, km_ref, vm_ref, qs_ref, ks_ref, vs_ref, gate_ref, zp_a_ref, zp_b_ref)

    @pl.when(n % 2 == 1)
    def _():
        _even_in_step(h_ref, gn_ref, w_ref, gql_ref, gkvl_ref, wuq_ref, wukv_ref, gqn_ref, gkn_ref,
                      sgq_ref, sgk_ref, cm_ref, sm_ref, cs_ref, ss_ref,
                      qm_ref, km_ref, vm_ref, qs_ref, ks_ref, vs_ref, gate_ref, zp_b_ref, zp_a_ref)


def _even_in_step(h_ref, gn_ref, w_ref, gql_ref, gkvl_ref, wuq_ref, wukv_ref, gqn_ref, gkn_ref,
                  sgq_ref, sgk_ref, cm_ref, sm_ref, cs_ref, ss_ref,
                  qm_ref, km_ref, vm_ref, qs_ref, ks_ref, vs_ref, gate_ref, zp_new_ref, zp_old_ref):
    x = h_ref[...]
    z = (x * _rms_scale(x, D_MODEL) * gn_ref[...]).astype(BF16)
    zp_new_ref[...] = jnp.dot(z, w_ref[:, 0:LAT_COLS + SWA_COLS], preferred_element_type=F32)
    gate_ref[0] = jnp.dot(z, w_ref[:, LAT_COLS + SWA_COLS:], preferred_element_type=F32).astype(BF16)

    cq = zp_old_ref[:, 0:MLA_Q_RANK]
    ckv = zp_old_ref[:, MLA_Q_RANK:MLA_Q_RANK + MLA_KV_RANK]
    kpe = zp_old_ref[:, MLA_Q_RANK + MLA_KV_RANK:LAT_COLS]
    cqn = (cq * _rms_scale(cq, MLA_Q_RANK) * gql_ref[...]).astype(BF16)
    ckvn = (ckv * _rms_scale(ckv, MLA_KV_RANK) * gkvl_ref[...]).astype(BF16)

    cm = cm_ref[...]
    sm = sm_ref[...]

    def rope_mla(t):
        rot = pltpu.roll(t, MLA_ROPE // 2, 1) + pltpu.roll(t, LANE - MLA_ROPE // 2, 1)
        return t * cm + rot * sm

    q_scale = (MLA_QK ** -0.5) * LOG2E
    gqn = gqn_ref[0:1, :]
    q_extra = gqn_ref[1:2, :]
    q_all = jnp.dot(cqn, wuq_ref[...], preferred_element_type=F32)
    for hd in range(MLA_HEADS):
        qh = q_all[:, hd * MLA_QK_PAD:(hd + 1) * MLA_QK_PAD]
        qn = qh * _rms_scale(qh, MLA_QK) * gqn
        q_out = jnp.concatenate([qn[:, :MLA_NOPE], rope_mla(qn[:, MLA_NOPE:])], axis=-1) * q_scale + q_extra
        qm_ref[0, hd] = q_out.astype(BF16)

    gkn = gkn_ref[0:1, :]
    k_extra = gkn_ref[1:2, :]
    kv = jnp.dot(ckvn, wukv_ref[...], preferred_element_type=F32)
    ss_pe = jnp.sum(kpe * kpe, axis=-1, keepdims=True)
    kr = rope_mla(kpe * gkn[:, MLA_NOPE:])
    for hd in range(MLA_HEADS):
        kn = kv[:, hd * MLA_QK_PAD:hd * MLA_QK_PAD + MLA_NOPE]
        rs = lax.rsqrt((jnp.sum(kn * kn, axis=-1, keepdims=True) + ss_pe) * (1.0 / MLA_QK) + EPS)
        k_out = jnp.concatenate([kn * rs * gkn[:, :MLA_NOPE], kr * rs], axis=-1) + k_extra
        km_ref[0, hd] = k_out.astype(BF16)
        vm_ref[0, hd] = kv[:, hd * MLA_QK_PAD + MLA_NOPE:(hd + 1) * MLA_QK_PAD].astype(BF16)

    sw = zp_old_ref.at[:, LAT_COLS:LAT_COLS + SWA_COLS]
    cs = cs_ref[...]
    ss = ss_ref[...]

    def norm_rope_swa(t, g):
        n = t * _rms_scale(t, SWA_HEAD_DIM) * g
        return n * cs + pltpu.roll(n, SWA_HEAD_DIM // 2, 1) * ss

    s_scale = (SWA_HEAD_DIM ** -0.5) * LOG2E
    for hd in range(SWA_HEADS):
        t = sw[:, hd * SWA_HEAD_DIM:(hd + 1) * SWA_HEAD_DIM]
        qs_ref[0, hd] = (norm_rope_swa(t, sgq_ref[...]) * s_scale).astype(BF16)
    for hd in range(SWA_KV_HEADS):
        t = sw[:, SWA_WIDTH + hd * SWA_HEAD_DIM:SWA_WIDTH + (hd + 1) * SWA_HEAD_DIM]
        ks_ref[0, hd] = norm_rope_swa(t, sgk_ref[...]).astype(BF16)
        vs_ref[0, hd] = sw[:, SWA_WIDTH + SWA_KV_WIDTH + hd * SWA_HEAD_DIM:
                           SWA_WIDTH + SWA_KV_WIDTH + (hd + 1) * SWA_HEAD_DIM].astype(BF16)


def _h_tile_spec(h, tm, tile_of=lambda b, i: (b, i)):
    if h.ndim == 3:
        return pl.BlockSpec((None, tm, D_MODEL), lambda *g: (*tile_of(*g), 0))
    return pl.BlockSpec((tm, D_MODEL), lambda *g: tile_of(*g)[::-1])


def _even_in(h2d, gn, w_all, gql, gkvl, wuq, wukv, gqn, gkn, sgq, sgk, cm, sm, cs, ss, *, batch, tp, tm):
    nt = tp // tm
    n_tiles = batch * nt
    new_tile = lambda n: divmod(jnp.minimum(n, n_tiles - 1), nt)
    old_tile = lambda n: divmod(jnp.maximum(n - 1, 0), nt)
    row_spec = lambda cols: pl.BlockSpec((tm, cols), lambda n: (old_tile(n)[1], 0))
    head_out = lambda heads, cols: pl.BlockSpec(
        (1, heads, tm, cols), lambda n: (old_tile(n)[0], 0, old_tile(n)[1], 0))
    n_w = w_all.shape[1]
    est = (w_all.size * 2 + (wuq.size + wukv.size) * 2 + 2 * tm * D_MODEL * 4
           + 2 * tm * 2 * (2 * MLA_HEADS * MLA_QK_PAD + MLA_WIDTH + SWA_WIDTH + 2 * SWA_KV_WIDTH)
           + 2 * tm * D_INNER * 2 + 2 * tm * (LAT_COLS + SWA_COLS) * 4 + 3 * tm * n_w * 4)
    return pl.pallas_call(
        _even_in_kernel,
        name="even_in",
        grid=(n_tiles + 1,),
        in_specs=[
            _h_tile_spec(h2d, tm, new_tile),
            _const_spec((1, D_MODEL)),
            _const_spec(w_all.shape),
            _const_spec((1, MLA_Q_RANK)),
            _const_spec((1, MLA_KV_RANK)),
            _const_spec(wuq.shape),
            _const_spec(wukv.shape),
            _const_spec((2, MLA_QK_PAD)),
            _const_spec((2, MLA_QK_PAD)),
            _const_spec((1, SWA_HEAD_DIM)),
            _const_spec((1, SWA_HEAD_DIM)),
            row_spec(LANE), row_spec(LANE), row_spec(LANE), row_spec(LANE),
        ],
        out_specs=[
            head_out(MLA_HEADS, MLA_QK_PAD),
            head_out(MLA_HEADS, MLA_QK_PAD),
            head_out(MLA_HEADS, MLA_V),
            head_out(SWA_HEADS, SWA_HEAD_DIM),
            head_out(SWA_KV_HEADS, SWA_HEAD_DIM),
            head_out(SWA_KV_HEADS, SWA_HEAD_DIM),
            pl.BlockSpec((1, tm, D_INNER), lambda n: (*new_tile(n), 0)),
        ],
        out_shape=[
            jax.ShapeDtypeStruct((batch, MLA_HEADS, tp, MLA_QK_PAD), BF16),
            jax.ShapeDtypeStruct((batch, MLA_HEADS, tp, MLA_QK_PAD), BF16),
            jax.ShapeDtypeStruct((batch, MLA_HEADS, tp, MLA_V), BF16),
            jax.ShapeDtypeStruct((batch, SWA_HEADS, tp, SWA_HEAD_DIM), BF16),
            jax.ShapeDtypeStruct((batch, SWA_KV_HEADS, tp, SWA_HEAD_DIM), BF16),
            jax.ShapeDtypeStruct((batch, SWA_KV_HEADS, tp, SWA_HEAD_DIM), BF16),
            jax.ShapeDtypeStruct((batch, tp, D_INNER), BF16),
        ],
        scratch_shapes=[pltpu.VMEM((tm, LAT_COLS + SWA_COLS), F32),
                        pltpu.VMEM((tm, LAT_COLS + SWA_COLS), F32)],
        compiler_params=pltpu.CompilerParams(
            dimension_semantics=("arbitrary",), vmem_limit_bytes=_vmem_limit(est)),
    )(h2d, gn, w_all, gql, gkvl, wuq, wukv, gqn, gkn, sgq, sgk, cm, sm, cs, ss)


def _mla_attn_kernel(fixed_ref, q_ref, k_ref, v_ref, gate_ref, o_ref, vx_ref, *, tp, tkc):
    @pl.when(pl.program_id(2) == 0)
    def _():
        lane = lax.broadcasted_iota(jnp.int32, (tp, MXU_DIM - MLA_V), 1)
        vx_ref[:, 0:MLA_V] = v_ref[0, 0]
        vx_ref[:, MLA_V:] = jnp.where(lane == 0, 1.0, 0.0).astype(BF16)

    @pl.when(fixed_ref[0] == 1)
    def _():
        _mla_fixed_shift(q_ref, k_ref, gate_ref, o_ref, vx_ref, tp=tp, tkc=tkc)

    @pl.when(fixed_ref[0] != 1)
    def _():
        _mla_online(q_ref, k_ref, gate_ref, o_ref, vx_ref, tp=tp, tkc=tkc)


def _mla_finish(acc, gate_ref, o_ref):
    o = acc[:, 0:MLA_V] / acc[:, MLA_V:MLA_V + 1]
    o_ref[0] = (o * _silu(gate_ref[0].astype(F32))).astype(BF16)


def _mla_fixed_shift(q_ref, k_ref, gate_ref, o_ref, vx_ref, *, tp, tkc):
    q = q_ref[0, 0]
    tq = q.shape[0]
    n_chunks = (tp - BLOCK) // tkc

    def scores(c):
        return _nt_dot(q, k_ref[0, 0, BLOCK + c * tkc:BLOCK + (c + 1) * tkc, :])

    def pv(c, p, acc):
        return acc + jnp.dot(p, vx_ref[BLOCK + c * tkc:BLOCK + (c + 1) * tkc, :], preferred_element_type=F32)

    col = lax.broadcasted_iota(jnp.int32, (tq, BLOCK), 1)
    s = jnp.where(col >= PAD_FRONT, _nt_dot(q, k_ref[0, 0, 0:BLOCK, :]), NEG_INF)
    s_next = scores(0)
    acc = jnp.dot(jnp.exp2(s).astype(BF16), vx_ref[0:BLOCK, :], preferred_element_type=F32)
    p_prev = None
    for c in range(n_chunks):
        s = s_next
        if c + 1 < n_chunks:
            s_next = scores(c + 1)
        p = jnp.exp2(s).astype(BF16)
        if p_prev is not None:
            acc = pv(c - 1, p_prev, acc)
        p_prev = p
    acc = pv(n_chunks - 1, p_prev, acc)
    _mla_finish(acc, gate_ref, o_ref)


def _mla_online(q_ref, k_ref, gate_ref, o_ref, vx_ref, *, tp, tkc):
    q = q_ref[0, 0]
    tq = q.shape[0]

    s = _nt_dot(q, k_ref[0, 0, 0:BLOCK, :])
    col = lax.broadcasted_iota(jnp.int32, (tq, BLOCK), 1)
    s = jnp.where(col >= PAD_FRONT, s, NEG_INF)
    m = jnp.max(s, axis=-1, keepdims=True)
    p = jnp.exp2(s - m)
    acc = jnp.dot(p.astype(BF16), vx_ref[0:BLOCK, :], preferred_element_type=F32)

    n_chunks = (tp - BLOCK) // tkc

    def scores(c):
        start = pl.multiple_of(BLOCK + c * tkc, BLOCK)
        return _nt_dot(q, k_ref[0, 0, pl.ds(start, tkc), :])

    def softmax_step(s, m):
        m_new = jnp.maximum(m, jnp.max(s, axis=-1, keepdims=True))
        alpha = jnp.exp2(m - m_new)
        p = jnp.exp2(s - m_new)
        return p.astype(BF16), alpha, m_new

    def pv_step(c, p, alpha, acc):
        start = pl.multiple_of(BLOCK + c * tkc, BLOCK)
        return alpha * acc + jnp.dot(p, vx_ref[pl.ds(start, tkc), :], preferred_element_type=F32)

    def chunk(c, carry):
        s, p_prev, alpha_prev, m, acc = carry
        s_next = scores(c + 1)
        p, alpha, m = softmax_step(s, m)
        acc = pv_step(c - 1, p_prev, alpha_prev, acc)
        return s_next, p, alpha, m, acc

    s1 = scores(1)
    p0, alpha0, m = softmax_step(scores(0), m)
    carry = (s1, p0, alpha0, m, acc)
    for c in range(1, n_chunks - 1):
        carry = chunk(c, carry)
    s, p_prev, alpha_prev, m, acc = carry
    p, alpha, m = softmax_step(s, m)
    acc = pv_step(n_chunks - 2, p_prev, alpha_prev, acc)
    acc = pv_step(n_chunks - 1, p, alpha, acc)
    _mla_finish(acc, gate_ref, o_ref)


def _mla_attn(fixed_flag, qm, km, vm, gate, *, batch, tp, tq, tkc):
    nq = tp // tq
    est = (2 * tp * (MLA_QK_PAD + MLA_V) * 2 + tp * MXU_DIM * 2 + 6 * tq * tkc * 4
           + 4 * tq * MLA_QK_PAD * 2)
    return pl.pallas_call(
        functools.partial(_mla_attn_kernel, tp=tp, tkc=tkc),
        name="mla_attn",
        grid=(batch, MLA_HEADS, nq),
        in_specs=[
            pl.BlockSpec(memory_space=pltpu.SMEM),
            pl.BlockSpec((1, 1, tq, MLA_QK_PAD), lambda b, h, i: (b, h, i, 0)),
            pl.BlockSpec((1, 1, tp, MLA_QK_PAD), lambda b, h, i: (b, h, 0, 0)),
            pl.BlockSpec((1, 1, tp, MLA_V), lambda b, h, i: (b, h, 0, 0)),
            pl.BlockSpec((1, tq, MLA_V), lambda b, h, i: (b, i, h)),
        ],
        out_specs=pl.BlockSpec((1, tq, MLA_V), lambda b, h, i: (b, i, h)),
        out_shape=jax.ShapeDtypeStruct((batch, tp, MLA_WIDTH), BF16),
        scratch_shapes=[pltpu.VMEM((tp, MXU_DIM), BF16)],
        compiler_params=pltpu.CompilerParams(
            dimension_semantics=("arbitrary", "arbitrary", "arbitrary"), vmem_limit_bytes=_vmem_limit(est)),
    )(fixed_flag, qm, km, vm, gate)


def _swa_masks(tp):
    band = 3 * BLOCK
    nb = tp // BLOCK
    r = (jnp.arange(SWA_REP * BLOCK, dtype=jnp.int32) % BLOCK)[:, None]
    c = jnp.arange(band, dtype=jnp.int32)[None, :]

    def mask(n):
        start = min(max((n - 1) * BLOCK, 0), tp - band)
        rq, rk = n * BLOCK + r, start + c
        return jnp.where((jnp.abs(rq - rk) <= SWA_WINDOW) & (rk >= BLOCK), 0.0, NEG_INF).astype(F32)

    band_bias = jnp.stack([mask(2), mask(0), mask(1), mask(nb - 1)])
    meta_bias = jnp.where(jnp.arange(BLOCK) >= PAD_FRONT, 0.0, NEG_INF).astype(F32)
    return band_bias, jnp.broadcast_to(meta_bias[None, :], (SWA_REP * BLOCK, BLOCK))


def _swa_attn_kernel(q_ref, k_ref, v_ref, sink_ref, gate_ref, bias_ref, mbias_ref, o_ref, vx_ref, *, tp, nblk):
    step = pl.program_id(2)
    rows = SWA_REP * BLOCK
    band = 3 * BLOCK

    @pl.when(step == 0)
    def _():
        lane = lax.broadcasted_iota(jnp.int32, (tp, MXU_DIM - SWA_HEAD_DIM), 1)
        vx_ref[:, 0:SWA_HEAD_DIM] = v_ref[0, 0]
        vx_ref[:, SWA_HEAD_DIM:] = jnp.where(lane == 0, 1.0, 0.0).astype(BF16)

    sink = sink_ref[0]
    g = _silu(gate_ref[0].astype(F32))
    starts, scores = [], []
    for u in range(nblk):
        n = step * nblk + u
        q = q_ref[0, :, u * BLOCK:(u + 1) * BLOCK, :].reshape(rows, SWA_HEAD_DIM)
        start = pl.multiple_of(jnp.clip((n - 1) * BLOCK, 0, tp - band), BLOCK)
        kind = jnp.where(n == 0, 1, jnp.where(n == 1, 2, jnp.where(n == tp // BLOCK - 1, 3, 0)))
        s_b = _nt_dot(q, k_ref[0, 0, pl.ds(start, band), :]) + bias_ref[kind]
        s_m = _nt_dot(q, k_ref[0, 0, 0:BLOCK, :]) + mbias_ref[...]
        starts.append(start)
        scores.append((s_b, s_m))

    for u in range(nblk):
        s_b, s_m = scores[u]
        mx = jnp.maximum(jnp.maximum(s_b[:, 0:BLOCK], s_b[:, BLOCK:2 * BLOCK]),
                         jnp.maximum(s_b[:, 2 * BLOCK:], s_m))
        m = jnp.maximum(jnp.max(mx, axis=-1, keepdims=True), sink)
        p_b = jnp.exp2(s_b - m).astype(BF16)
        p_m = jnp.exp2(s_m - m).astype(BF16)
        acc = (jnp.dot(p_b, vx_ref[pl.ds(starts[u], band), :], preferred_element_type=F32)
               + jnp.dot(p_m, vx_ref[0:BLOCK, :], preferred_element_type=F32))
        l = jnp.exp2(sink - m) + acc[:, SWA_HEAD_DIM:SWA_HEAD_DIM + 1]
        o = acc[:, 0:SWA_HEAD_DIM] / l
        for r in range(SWA_REP):
            o_ref[0, u * BLOCK:(u + 1) * BLOCK, r * SWA_HEAD_DIM:(r + 1) * SWA_HEAD_DIM] = (
                o[r * BLOCK:(r + 1) * BLOCK]
                * g[u * BLOCK:(u + 1) * BLOCK, r * SWA_HEAD_DIM:(r + 1) * SWA_HEAD_DIM]).astype(BF16)


def _swa_attn(qs, ks, vs, sink_rows, gate, *, batch, tp):
    nb = tp // BLOCK
    assert nb >= 4, "the four band-mask kinds assume at least four token blocks"
    nblk = 3 if nb % 3 == 0 else 1
    rows_q = nblk * BLOCK
    band_bias, meta_bias = _swa_masks(tp)
    gcols = SWA_REP * SWA_HEAD_DIM
    est = 4 * tp * SWA_HEAD_DIM * 2 + tp * MXU_DIM * 2 + nblk * 12 * SWA_REP * BLOCK * 4 * BLOCK * 4
    return pl.pallas_call(
        functools.partial(_swa_attn_kernel, tp=tp, nblk=nblk),
        name="swa_attn",
        grid=(batch, SWA_KV_HEADS, nb // nblk),
        in_specs=[
            pl.BlockSpec((1, SWA_REP, rows_q, SWA_HEAD_DIM), lambda b, g, n: (b, g, n, 0)),
            pl.BlockSpec((1, 1, tp, SWA_HEAD_DIM), lambda b, g, n: (b, g, 0, 0)),
            pl.BlockSpec((1, 1, tp, SWA_HEAD_DIM), lambda b, g, n: (b, g, 0, 0)),
            pl.BlockSpec((1, SWA_REP * BLOCK, 1), lambda b, g, n: (g, 0, 0)),
            pl.BlockSpec((1, rows_q, gcols), lambda b, g, n: (b, n, MLA_WIDTH // gcols + g)),
            _const_spec(band_bias.shape),
            _const_spec(meta_bias.shape),
        ],
        out_specs=pl.BlockSpec((1, rows_q, gcols), lambda b, g, n: (b, n, g)),
        out_shape=jax.ShapeDtypeStruct((batch, tp, SWA_WIDTH), BF16),
        scratch_shapes=[pltpu.VMEM((tp, MXU_DIM), BF16)],
        compiler_params=pltpu.CompilerParams(
            dimension_semantics=("arbitrary", "arbitrary", "arbitrary"), vmem_limit_bytes=_vmem_limit(est)),
    )(qs, ks, vs, sink_rows, gate, band_bias, meta_bias)


def _even_out_kernel(ya_ref, yb_ref, w_ref, h_ref, o_ref, *, tm):
    i = pl.program_id(1)
    d = (jnp.dot(ya_ref[0], w_ref[0:MLA_WIDTH, :], preferred_element_type=F32)
         + jnp.dot(yb_ref[0], w_ref[MLA_WIDTH:, :], preferred_element_type=F32))
    row = i * tm + lax.broadcasted_iota(jnp.int32, (tm, D_MODEL), 0)
    o_ref[...] = h_ref[...] + jnp.where(row >= PAD_FRONT, d, 0.0)


def _even_out(ya, yb, w_out, h2d, *, batch, tp, tm):
    nt = tp // tm
    est = w_out.size * 2 + 4 * tm * D_MODEL * 2 + 6 * tm * D_MODEL * 4
    return pl.pallas_call(
        functools.partial(_even_out_kernel, tm=tm),
        name="even_out",
        grid=(batch, nt),
        in_specs=[
            pl.BlockSpec((1, tm, MLA_WIDTH), lambda b, i: (b, i, 0)),
            pl.BlockSpec((1, tm, SWA_WIDTH), lambda b, i: (b, i, 0)),
            _const_spec(w_out.shape),
            _h_tile_spec(h2d, tm),
        ],
        out_specs=pl.BlockSpec((tm, D_MODEL), lambda b, i: (i, b)),
        out_shape=jax.ShapeDtypeStruct((tp, batch * D_MODEL), F32),
        compiler_params=pltpu.CompilerParams(
            dimension_semantics=("arbitrary", "arbitrary"), vmem_limit_bytes=_vmem_limit(est)),
    )(ya, yb, w_out, h2d)


def _lru_coeffs(xc, xcb, wg_ref, bg_ref, lam_ref, a_ref, b_ref, t0=None):
    tmr = xc.shape[0]
    if t0 is not None:
        t_idx = t0 + (lax.broadcasted_iota(jnp.int32, (tmr, LRU_BLOCK_DIM), 0) // SUBLANE)
        live = t_idx >= PAD_FRONT
    for n in range(LRU_BLOCKS):
        sl = slice(n * LRU_BLOCK_DIM, (n + 1) * LRU_BLOCK_DIM)
        xn = xc[:, sl]
        gh = jnp.dot(xcb[:, sl], wg_ref[n], preferred_element_type=F32) + bg_ref[n]
        tr = jnp.tanh(gh[:, :LRU_BLOCK_DIM])
        ti = jnp.tanh(gh[:, LRU_BLOCK_DIM:])
        ch = (-0.5 * LRU_C) * jax.nn.softplus(-lam_ref[:, sl])
        log_a = tr * ch + ch
        a = jnp.exp(log_a)
        a_ref[:, sl] = a
        one_m_a2 = jnp.tanh(log_a) * (-1.0 - a * a)
        b = jnp.sqrt(one_m_a2) * (0.5 * ti + 0.5) * xn
        b_ref[:, sl] = b if t0 is None else jnp.where(live, b, 0.0)


def _odd_fwd_kernel(h_ref, halo_ref, gn_ref, w_ref, cw_ref, cb_ref, wg_ref, bg_ref, lam_ref,
                    xc_ref, gate_ref, hf_ref,
                    uprev_ref, hst_ref, a_ref, b_ref, *, tc, nt):
    j = pl.program_id(0)
    tmr = tc * SUBLANE
    halo_rows = SUBLANE
    prev_rows = 2 * SUBLANE

    @pl.when(j == 0)
    def _():
        uprev_ref[...] = jnp.zeros_like(uprev_ref)
        hst_ref[...] = jnp.zeros_like(hst_ref)

    x = jnp.concatenate([h_ref[...].reshape(tmr, D_MODEL), halo_ref[...].reshape(halo_rows, D_MODEL)], axis=0)
    z = (x * _rms_scale(x, D_MODEL) * gn_ref[...]).astype(BF16)
    u = jnp.dot(z, w_ref[:, 0:LRU_WIDTH], preferred_element_type=F32)
    gate_ref[...] = jnp.dot(z[0:tmr], w_ref[:, LRU_WIDTH:], preferred_element_type=F32).astype(BF16)

    u_next = jnp.where(j == nt - 1, 0.0, u[tmr:])
    ue = jnp.concatenate([uprev_ref[...], u[0:tmr], u_next], axis=0)
    uprev_ref[...] = u[tmr - prev_rows:tmr]
    xc = cb_ref[...] + ue[0:tmr] * cw_ref[0:1, :]
    for tap in range(1, CONV_WIDTH):
        xc = xc + ue[tap * SUBLANE:tap * SUBLANE + tmr] * cw_ref[tap:tap + 1, :]
    xcb = xc.astype(BF16)
    xc_ref[...] = xcb

    _lru_coeffs(xc, xcb, wg_ref, bg_ref, lam_ref, a_ref, b_ref, j * tc)

    def step(t, hs):
        r = pl.multiple_of(t * SUBLANE, SUBLANE)
        hs = a_ref[pl.ds(r, SUBLANE), :] * hs + b_ref[pl.ds(r, SUBLANE), :]
        b_ref[pl.ds(r, SUBLANE), :] = hs
        return hs

    hst_ref[...] = lax.fori_loop(0, tc, step, hst_ref[...], unroll=8)
    hf_ref[...] = b_ref[...].astype(BF16)


def _odd_fwd(h, gn, w_in, cw, cb, wg, bg, lam, *, batch, tp, tc):
    nt = tp // tc
    tmr = tc * batch
    blk = lambda cols: pl.BlockSpec((tc, batch, cols), lambda j: (j, 0, 0))
    row_blk = pl.BlockSpec((tmr, LRU_WIDTH), lambda j: (j, 0))
    est = (w_in.size * 2 + 2 * tmr * D_MODEL * 4 + 3 * 2 * tmr * LRU_WIDTH * 2
           + 2 * tmr * LRU_WIDTH * 4 + 5 * tmr * LRU_WIDTH * 4)
    return pl.pallas_call(
        functools.partial(_odd_fwd_kernel, tc=tc, nt=nt),
        name="odd_fwd",
        grid=(nt,),
        in_specs=[
            blk(D_MODEL),
            pl.BlockSpec((1, batch, D_MODEL), lambda j: (jnp.minimum((j + 1) * tc, tp - 1), 0, 0)),
            _const_spec((1, D_MODEL)),
            _const_spec(w_in.shape),
            _const_spec(cw.shape),
            _const_spec(cb.shape),
            _const_spec(wg.shape),
            _const_spec(bg.shape),
            _const_spec(lam.shape),
        ],
        out_specs=[row_blk, row_blk, row_blk],
        out_shape=[jax.ShapeDtypeStruct((tp * batch, LRU_WIDTH), BF16)] * 3,
        scratch_shapes=[pltpu.VMEM((2 * SUBLANE, LRU_WIDTH), F32),
                        pltpu.VMEM((SUBLANE, LRU_WIDTH), F32),
                        pltpu.VMEM((tmr, LRU_WIDTH), F32),
                        pltpu.VMEM((tmr, LRU_WIDTH), F32)],
        compiler_params=pltpu.CompilerParams(
            dimension_semantics=("arbitrary",), vmem_limit_bytes=_vmem_limit(est)),
    )(h, h, gn, w_in, cw, cb, wg, bg, lam)


def _odd_bwd_kernel(xc_ref, gate_ref, hf_ref, h_ref, wg_ref, bg_ref, lam_ref, w_ref, o_ref,
                    hst_ref, a_ref, b_ref, st_ref, *, tc, nt, final):
    j = pl.program_id(0)
    jj = nt - 1 - j
    tmr = tc * SUBLANE

    @pl.when(j == 0)
    def _():
        hst_ref[...] = jnp.zeros_like(hst_ref)

    xcb = xc_ref[...]
    _lru_coeffs(xcb.astype(F32), xcb, wg_ref, bg_ref, lam_ref, a_ref, b_ref)

    def step(s, hs):
        t = tc - 1 - s
        r = pl.multiple_of(t * SUBLANE, SUBLANE)
        hs = a_ref[pl.ds(r, SUBLANE), :] * hs + b_ref[pl.ds(r, SUBLANE), :]
        b_ref[pl.ds(r, SUBLANE), :] = hs
        return hs

    hst_ref[...] = lax.fori_loop(0, tc, step, hst_ref[...], unroll=8)

    y = ((hf_ref[...].astype(F32) + b_ref[...]) * _silu(gate_ref[...].astype(F32))).astype(BF16)
    d = jnp.dot(y, w_ref[...], preferred_element_type=F32)
    if not final:
        t_idx = jj * tc + (lax.broadcasted_iota(jnp.int32, (tmr, D_MODEL), 0) // SUBLANE)
        o_ref[...] = h_ref[...] + jnp.where(t_idx >= PAD_FRONT, d, 0.0).reshape(tc, SUBLANE, D_MODEL)
    else:
        @pl.when(jj >= BLOCK // tc)
        def _():
            res = h_ref[...].reshape(tmr, D_MODEL) + d
            for c in range(D_MODEL // LANE):
                st_ref[c] = res[:, c * LANE:(c + 1) * LANE]
                for b in range(SUBLANE):
                    o_ref[b, :, c * LANE:(c + 1) * LANE] = st_ref[c, pl.ds(b, tc, stride=SUBLANE), :]


def _odd_bwd(xc, gate, hf, h, wg, bg, lam, w_out, *, batch, tp, tc, final):
    nt = tp // tc
    tmr = tc * batch
    blk = lambda cols: pl.BlockSpec((tc, batch, cols), lambda j: (nt - 1 - j, 0, 0))
    if final:
        first = BLOCK // tc
        out_spec = pl.BlockSpec((batch, tc, D_MODEL), lambda j: (0, jnp.maximum(nt - 1 - j - first, 0), 0))
        out_shape = jax.ShapeDtypeStruct((batch, tp - BLOCK, D_MODEL), F32)
    else:
        out_spec = blk(D_MODEL)
        out_shape = jax.ShapeDtypeStruct((tp, batch, D_MODEL), F32)
    row_blk = pl.BlockSpec((tmr, LRU_WIDTH), lambda j: (nt - 1 - j, 0))
    est = (w_out.size * 2 + 3 * 2 * tmr * LRU_WIDTH * 2 + 4 * tmr * D_MODEL * 4
           + 2 * tmr * LRU_WIDTH * 4 + 5 * tmr * LRU_WIDTH * 4)
    return pl.pallas_call(
        functools.partial(_odd_bwd_kernel, tc=tc, nt=nt, final=final),
        name="odd_bwd",
        grid=(nt,),
        in_specs=[
            row_blk, row_blk, row_blk, blk(D_MODEL),
            _const_spec(wg.shape),
            _const_spec(bg.shape),
            _const_spec(lam.shape),
            _const_spec(w_out.shape),
        ],
        out_specs=out_spec,
        out_shape=out_shape,
        scratch_shapes=[pltpu.VMEM((SUBLANE, LRU_WIDTH), F32),
                        pltpu.VMEM((tmr, LRU_WIDTH), F32),
                        pltpu.VMEM((tmr, LRU_WIDTH), F32),
                        pltpu.VMEM((D_MODEL // LANE, tmr, LANE), F32)],
        compiler_params=pltpu.CompilerParams(
            dimension_semantics=("arbitrary",), vmem_limit_bytes=_vmem_limit(est)),
    )(xc, gate, hf, h, wg, bg, lam, w_out)


def _rope_tables(tp):
    pos = (jnp.arange(tp, dtype=F32) - PAD_FRONT)[:, None]

    def table(dim):
        inv = ROPE_THETA ** (-jnp.arange(0, dim, 2, dtype=F32) / dim)
        ang = pos * inv[None, :]
        return jnp.cos(ang), jnp.sin(ang)

    c, s = table(MLA_ROPE)
    zeros = jnp.zeros((tp, LANE - MLA_ROPE), F32)
    cm = jnp.concatenate([c, c, zeros], axis=1)
    sm = jnp.concatenate([-s, s, zeros], axis=1)
    c, s = table(SWA_HEAD_DIM)
    cs = jnp.concatenate([c, c], axis=1)
    ss = jnp.concatenate([-s, s], axis=1)
    return cm, sm, cs, ss


def _prep_even(w_in, w_uq, w_ukv, g_qn, g_kn):
    lat_w = MLA_Q_RANK + MLA_KV_RANK + MLA_ROPE
    w_all = jnp.concatenate(
        [w_in[:, :lat_w], jnp.zeros((D_MODEL, LAT_COLS - lat_w), w_in.dtype), w_in[:, lat_w:]], axis=1).astype(BF16)
    wuq = jnp.pad(w_uq, ((0, 0), (0, 0), (0, MLA_QK_PAD - MLA_QK))).reshape(
        MLA_Q_RANK, MLA_HEADS * MLA_QK_PAD).astype(BF16)
    wukv = w_ukv.reshape(MLA_KV_RANK, MLA_HEADS * (MLA_NOPE + MLA_V)).astype(BF16)
    shift = 1.01 * LOG2E * math.sqrt(MLA_QK) * jnp.max(jnp.abs(g_qn)) * jnp.max(jnp.abs(g_kn))
    lane = jnp.arange(MLA_QK_PAD) == MLA_QK
    pad_g = lambda g: jnp.pad(g, (0, MLA_QK_PAD - MLA_QK))
    gq2 = jnp.stack([pad_g(g_qn), jnp.where(lane, -shift, 0.0)]).astype(F32)
    gk2 = jnp.stack([pad_g(g_kn), jnp.where(lane, 1.0, 0.0)]).astype(F32)
    fixed_flag = (shift <= MLA_FIXED_SHIFT_MAX).astype(jnp.int32).reshape(1)
    return w_all, wuq, wukv, gq2, gk2, fixed_flag


def _prep_gates(w_a, b_a, w_x, b_x):
    wg = (0.5 * jnp.concatenate([w_a, w_x], axis=-1)).astype(BF16)
    bg = 0.5 * jnp.concatenate([b_a.reshape(LRU_BLOCKS, 1, LRU_BLOCK_DIM),
                                b_x.reshape(LRU_BLOCKS, 1, LRU_BLOCK_DIM)], axis=-1)
    return wg, bg


def _tile(tp, prefs):
    for t in prefs:
        if tp % t == 0:
            return t
    raise ValueError(f"no tile of {prefs} divides {tp}")


def kernel(x, meta_tokens, norm_g, even_w_in, mla_g_q_lat, mla_g_kv_lat, mla_w_uq, mla_w_ukv, mla_g_qn, mla_g_kn, swa_g_qn, swa_g_kn, swa_sink, even_w_out, odd_w_in, lru_conv_w, lru_conv_b, lru_w_a, lru_b_a, lru_w_x, lru_b_x, lru_lambda, odd_w_out):
    batch, seq, d = x.shape
    assert d == D_MODEL and batch == SUBLANE and seq % BLOCK == 0
    tp = BLOCK + seq
    tm = _tile(tp, (384, 128))
    tq = _tile(tp, (1056, 384, 128))
    tkc = _tile(seq, (512, 256, 128))
    assert seq // tkc >= 2, "the MLA key-chunk pipeline needs at least two chunks"
    tc = _tile(tp, (64, 32, 16))

    meta = jnp.broadcast_to(meta_tokens.astype(x.dtype)[None], (batch, N_META, D_MODEL))
    h = jnp.concatenate([jnp.zeros((batch, PAD_FRONT, D_MODEL), x.dtype), meta, x], axis=1)
    cm, sm, cs, ss = _rope_tables(tp)

    for l in range(DEPTH):
        j = l // 2
        gn = norm_g[l].reshape(1, D_MODEL)
        if l % 2 == 0:
            w_all, wuq, wukv, gqn, gkn, fixed_flag = _prep_even(even_w_in[j], mla_w_uq[j], mla_w_ukv[j],
                                                                mla_g_qn[j], mla_g_kn[j])
            h2d = h if l == 0 else h.reshape(tp, batch * D_MODEL)
            qm, km, vm, qs, ks, vs, gate = _even_in(
                h2d, gn, w_all, mla_g_q_lat[j].reshape(1, -1), mla_g_kv_lat[j].reshape(1, -1), wuq, wukv,
                gqn, gkn, swa_g_qn[j].reshape(1, -1), swa_g_kn[j].reshape(1, -1), cm, sm, cs, ss,
                batch=batch, tp=tp, tm=tm)
            ya = _mla_attn(fixed_flag, qm, km, vm, gate, batch=batch, tp=tp, tq=tq, tkc=tkc)
            sink_rows = jnp.repeat(swa_sink[j].astype(F32) * LOG2E, BLOCK).reshape(
                SWA_KV_HEADS, SWA_REP * BLOCK, 1)
            yb = _swa_attn(qs, ks, vs, sink_rows, gate, batch=batch, tp=tp)
            h2d = _even_out(ya, yb, even_w_out[j].astype(BF16), h2d, batch=batch, tp=tp, tm=tm)
            h = h2d.reshape(tp, batch, D_MODEL)
        else:
            wgf, bgf = _prep_gates(lru_w_a[j, 0], lru_b_a[j, 0], lru_w_x[j, 0], lru_b_x[j, 0])
            wgb, bgb = _prep_gates(lru_w_a[j, 1], lru_b_a[j, 1], lru_w_x[j, 1], lru_b_x[j, 1])
            xc, gate, hf = _odd_fwd(h, gn, odd_w_in[j].astype(BF16), lru_conv_w[j], lru_conv_b[j].reshape(1, -1),
                                    wgf, bgf, lru_lambda[j, 0].reshape(1, -1), batch=batch, tp=tp, tc=tc)
            h = _odd_bwd(xc, gate, hf, h, wgb, bgb, lru_lambda[j, 1].reshape(1, -1), odd_w_out[j].astype(BF16),
                         batch=batch, tp=tp, tc=tc, final=(l == DEPTH - 1))
    return h
```

```python
import functools
import math

import jax
import jax.numpy as jnp
from jax import lax
from jax.experimental import pallas as pl
from jax.experimental.pallas import tpu as pltpu

F32 = jnp.float32
BF16 = jnp.bfloat16

D_MODEL = 1024
DEPTH = 4
N_META = 16
D_INNER = 2 * D_MODEL
ROPE_THETA = 10000.0
EPS = 1e-6
NEG_INF = -1e30

MLA_HEADS = 8
MLA_Q_RANK = 384
MLA_KV_RANK = 256
MLA_NOPE = 128
MLA_ROPE = 64
MLA_QK = MLA_NOPE + MLA_ROPE
MLA_V = 128
MLA_WIDTH = MLA_HEADS * MLA_V

SWA_HEADS = 8
SWA_KV_HEADS = 2
SWA_REP = SWA_HEADS // SWA_KV_HEADS
SWA_HEAD_DIM = 128
SWA_WINDOW = 128
SWA_WIDTH = SWA_HEADS * SWA_HEAD_DIM
SWA_KV_WIDTH = SWA_KV_HEADS * SWA_HEAD_DIM

LRU_WIDTH = D_INNER
LRU_BLOCKS = 16
LRU_BLOCK_DIM = LRU_WIDTH // LRU_BLOCKS
LRU_C = 8.0
CONV_WIDTH = 4

LANE = 128
SUBLANE = 8
MXU_DIM = 256
V7X_VMEM_BYTES = 64 * 1024 * 1024

BLOCK = 128
PAD_FRONT = BLOCK - N_META
MLA_QK_PAD = MXU_DIM
LAT_COLS = 768
SWA_COLS = SWA_WIDTH + 2 * SWA_KV_WIDTH
LOG2E = 1.4426950408889634
MLA_FIXED_SHIFT_MAX = 40.0


def _vmem_limit(nbytes):
    return int(min(V7X_VMEM_BYTES - (4 << 20), max(nbytes, 16 << 20)))


def _const_spec(shape):
    nd = len(shape)
    return pl.BlockSpec(shape, lambda *_: (0,) * nd, pipeline_mode=pl.Buffered(1))


def _rms_scale(x, n):
    return lax.rsqrt(jnp.sum(x * x, axis=-1, keepdims=True) * (1.0 / n) + EPS)


def _silu(g):
    gh = 0.5 * g
    return gh * (1.0 + jnp.tanh(gh))


def _nt_dot(a, b):
    return lax.dot_general(a, b, (((1,), (1,)), ((), ())), preferred_element_type=F32)


def _even_in_kernel(h_ref, gn_ref, w_ref, gql_ref, gkvl_ref, wuq_ref, wukv_ref, gqn_ref, gkn_ref,
                    sgq_ref, sgk_ref, cm_ref, sm_ref, cs_ref, ss_ref,
                    qm_ref, km_ref, vm_ref, qs_ref, ks_ref, vs_ref, gate_ref, zp_a_ref, zp_b_ref):
    n = pl.program_id(0)

    @pl.when(n == 0)
    def _():
        zp_b_ref[...] = jnp.zeros_like(zp_b_ref)

    @pl.when(n % 2 == 0)
    def _():
        _even_in_step(h_ref, gn_ref, w_ref, gql_ref, gkvl_ref, wuq_ref, wukv_ref, gqn_ref, gkn_ref,
                      sgq_ref, sgk_ref, cm_ref, sm_ref, cs_ref, ss_ref,
                      qm_ref, km_ref, vm_ref, qs_ref, ks_ref, vs_ref, gate_ref, zp_a_ref, zp_b_ref)

    @pl.when(n % 2 == 1)
    def _():
        _even_in_step(h_ref, gn_ref, w_ref, gql_ref, gkvl_ref, wuq_ref, wukv_ref, gqn_ref, gkn_ref,
                      sgq_ref, sgk_ref, cm_ref, sm_ref, cs_ref, ss_ref,
                      qm_ref, km_ref, vm_ref, qs_ref, ks_ref, vs_ref, gate_ref, zp_b_ref, zp_a_ref)


def _even_in_step(h_ref, gn_ref, w_ref, gql_ref, gkvl_ref, wuq_ref, wukv_ref, gqn_ref, gkn_ref,
                  sgq_ref, sgk_ref, cm_ref, sm_ref, cs_ref, ss_ref,
                  qm_ref, km_ref, vm_ref, qs_ref, ks_ref, vs_ref, gate_ref, zp_new_ref, zp_old_ref):
    cq = zp_old_ref[:, 0:MLA_Q_RANK]
    ckv = zp_old_ref[:, MLA_Q_RANK:MLA_Q_RANK + MLA_KV_RANK]
    kpe = zp_old_ref[:, MLA_Q_RANK + MLA_KV_RANK:LAT_COLS]
    cqn = (cq * _rms_scale(cq, MLA_Q_RANK) * gql_ref[...]).astype(BF16)
    ckvn = (ckv * _rms_scale(ckv, MLA_KV_RANK) * gkvl_ref[...]).astype(BF16)
    q_all = jnp.dot(cqn, wuq_ref[...], preferred_element_type=F32)
    kv = jnp.dot(ckvn, wukv_ref[...], preferred_element_type=F32)

    x = h_ref[...]
    z = (x * _rms_scale(x, D_MODEL) * gn_ref[...]).astype(BF16)
    zp_new_ref[...] = jnp.dot(z, w_ref[:, 0:LAT_COLS + SWA_COLS], preferred_element_type=F32)
    gate_ref[0] = jnp.dot(z, w_ref[:, LAT_COLS + SWA_COLS:], preferred_element_type=F32).astype(BF16)


    cm = cm_ref[...]
    sm = sm_ref[...]

    def rope_mla(t):
        rot = pltpu.roll(t, MLA_ROPE // 2, 1) + pltpu.roll(t, LANE - MLA_ROPE // 2, 1)
        return t * cm + rot * sm

    q_scale = (MLA_QK ** -0.5) * LOG2E
    gqn = gqn_ref[0:1, :]
    q_extra = gqn_ref[1:2, :]
    for hd in range(MLA_HEADS):
        qh = q_all[:, hd * MLA_QK_PAD:(hd + 1) * MLA_QK_PAD]
        qn = qh * _rms_scale(qh, MLA_QK) * gqn
        q_out = jnp.concatenate([qn[:, :MLA_NOPE], rope_mla(qn[:, MLA_NOPE:])], axis=-1) * q_scale + q_extra
        qm_ref[0, hd] = q_out.astype(BF16)

    gkn = gkn_ref[0:1, :]
    k_extra = gkn_ref[1:2, :]
    ss_pe = jnp.sum(kpe * kpe, axis=-1, keepdims=True)
    kr = rope_mla(kpe * gkn[:, MLA_NOPE:])
    for hd in range(MLA_HEADS):
        kn = kv[:, hd * MLA_QK_PAD:hd * MLA_QK_PAD + MLA_NOPE]
        rs = lax.rsqrt((jnp.sum(kn * kn, axis=-1, keepdims=True) + ss_pe) * (1.0 / MLA_QK) + EPS)
        k_out = jnp.concatenate([kn * rs * gkn[:, :MLA_NOPE], kr * rs], axis=-1) + k_extra
        km_ref[0, hd] = k_out.astype(BF16)
        vm_ref[0, hd] = kv[:, hd * MLA_QK_PAD + MLA_NOPE:(hd + 1) * MLA_QK_PAD].astype(BF16)

    sw = zp_old_ref.at[:, LAT_COLS:LAT_COLS + SWA_COLS]
    cs = cs_ref[...]
    ss = ss_ref[...]

    def norm_rope_swa(t, g):
        n = t * _rms_scale(t, SWA_HEAD_DIM) * g
        return n * cs + pltpu.roll(n, SWA_HEAD_DIM // 2, 1) * ss

    s_scale = (SWA_HEAD_DIM ** -0.5) * LOG2E
    for hd in range(SWA_HEADS):
        t = sw[:, hd * SWA_HEAD_DIM:(hd + 1) * SWA_HEAD_DIM]
        qs_ref[0, hd] = (norm_rope_swa(t, sgq_ref[...]) * s_scale).astype(BF16)
    for hd in range(SWA_KV_HEADS):
        t = sw[:, SWA_WIDTH + hd * SWA_HEAD_DIM:SWA_WIDTH + (hd + 1) * SWA_HEAD_DIM]
        ks_ref[0, hd] = norm_rope_swa(t, sgk_ref[...]).astype(BF16)
        vs_ref[0, hd] = sw[:, SWA_WIDTH + SWA_KV_WIDTH + hd * SWA_HEAD_DIM:
                           SWA_WIDTH + SWA_KV_WIDTH + (hd + 1) * SWA_HEAD_DIM].astype(BF16)


def _h_tile_spec(h, tm, tile_of=lambda b, i: (b, i)):
    if h.ndim == 3:
        return pl.BlockSpec((None, tm, D_MODEL), lambda *g: (*tile_of(*g), 0))
    return pl.BlockSpec((tm, D_MODEL), lambda *g: tile_of(*g)[::-1])


def _even_in(h2d, gn, w_all, gql, gkvl, wuq, wukv, gqn, gkn, sgq, sgk, cm, sm, cs, ss, *, batch, tp, tm):
    nt = tp // tm
    n_tiles = batch * nt
    new_tile = lambda n: divmod(jnp.minimum(n, n_tiles - 1), nt)
    old_tile = lambda n: divmod(jnp.maximum(n - 1, 0), nt)
    row_spec = lambda cols: pl.BlockSpec((tm, cols), lambda n: (old_tile(n)[1], 0))
    head_out = lambda heads, cols: pl.BlockSpec(
        (1, heads, tm, cols), lambda n: (old_tile(n)[0], 0, old_tile(n)[1], 0))
    n_w = w_all.shape[1]
    est = (w_all.size * 2 + (wuq.size + wukv.size) * 2 + 2 * tm * D_MODEL * 4
           + 2 * tm * 2 * (2 * MLA_HEADS * MLA_QK_PAD + MLA_WIDTH + SWA_WIDTH + 2 * SWA_KV_WIDTH)
           + 2 * tm * D_INNER * 2 + 2 * tm * (LAT_COLS + SWA_COLS) * 4 + 3 * tm * n_w * 4)
    return pl.pallas_call(
        _even_in_kernel,
        name="even_in",
        grid=(n_tiles + 1,),
        in_specs=[
            _h_tile_spec(h2d, tm, new_tile),
            _const_spec((1, D_MODEL)),
            _const_spec(w_all.shape),
            _const_spec((1, MLA_Q_RANK)),
            _const_spec((1, MLA_KV_RANK)),
            _const_spec(wuq.shape),
            _const_spec(wukv.shape),
            _const_spec((2, MLA_QK_PAD)),
            _const_spec((2, MLA_QK_PAD)),
            _const_spec((1, SWA_HEAD_DIM)),
            _const_spec((1, SWA_HEAD_DIM)),
            row_spec(LANE), row_spec(LANE), row_spec(LANE), row_spec(LANE),
        ],
        out_specs=[
            head_out(MLA_HEADS, MLA_QK_PAD),
            head_out(MLA_HEADS, MLA_QK_PAD),
            head_out(MLA_HEADS, MLA_V),
            head_out(SWA_HEADS, SWA_HEAD_DIM),
            head_out(SWA_KV_HEADS, SWA_HEAD_DIM),
            head_out(SWA_KV_HEADS, SWA_HEAD_DIM),
            pl.BlockSpec((1, tm, D_INNER), lambda n: (*new_tile(n), 0)),
        ],
        out_shape=[
            jax.ShapeDtypeStruct((batch, MLA_HEADS, tp, MLA_QK_PAD), BF16),
            jax.ShapeDtypeStruct((batch, MLA_HEADS, tp, MLA_QK_PAD), BF16),
            jax.ShapeDtypeStruct((batch, MLA_HEADS, tp, MLA_V), BF16),
            jax.ShapeDtypeStruct((batch, SWA_HEADS, tp, SWA_HEAD_DIM), BF16),
            jax.ShapeDtypeStruct((batch, SWA_KV_HEADS, tp, SWA_HEAD_DIM), BF16),
            jax.ShapeDtypeStruct((batch, SWA_KV_HEADS, tp, SWA_HEAD_DIM), BF16),
            jax.ShapeDtypeStruct((batch, tp, D_INNER), BF16),
        ],
        scratch_shapes=[pltpu.VMEM((tm, LAT_COLS + SWA_COLS), F32),
                        pltpu.VMEM((tm, LAT_COLS + SWA_COLS), F32)],
        compiler_params=pltpu.CompilerParams(
            dimension_semantics=("arbitrary",), vmem_limit_bytes=_vmem_limit(est)),
    )(h2d, gn, w_all, gql, gkvl, wuq, wukv, gqn, gkn, sgq, sgk, cm, sm, cs, ss)


def _mla_attn_kernel(fixed_ref, q_ref, k_ref, v_ref, gate_ref, o_ref, vx_ref, *, tp, tkc):
    @pl.when(pl.program_id(2) == 0)
    def _():
        lane = lax.broadcasted_iota(jnp.int32, (tp, MXU_DIM - MLA_V), 1)
        vx_ref[:, 0:MLA_V] = v_ref[0, 0]
        vx_ref[:, MLA_V:] = jnp.where(lane == 0, 1.0, 0.0).astype(BF16)

    @pl.when(fixed_ref[0] == 1)
    def _():
        _mla_fixed_shift(q_ref, k_ref, gate_ref, o_ref, vx_ref, tp=tp, tkc=tkc)

    @pl.when(fixed_ref[0] != 1)
    def _():
        _mla_online(q_ref, k_ref, gate_ref, o_ref, vx_ref, tp=tp, tkc=tkc)


def _mla_finish(acc, gate_ref, o_ref):
    o = acc[:, 0:MLA_V] / acc[:, MLA_V:MLA_V + 1]
    o_ref[0] = (o * _silu(gate_ref[0].astype(F32))).astype(BF16)


def _mla_fixed_shift(q_ref, k_ref, gate_ref, o_ref, vx_ref, *, tp, tkc):
    q = q_ref[0, 0]
    tq = q.shape[0]
    n_chunks = (tp - BLOCK) // tkc

    def scores(c):
        return _nt_dot(q, k_ref[0, 0, BLOCK + c * tkc:BLOCK + (c + 1) * tkc, :])

    def pv(c, p, acc):
        return acc + jnp.dot(p, vx_ref[BLOCK + c * tkc:BLOCK + (c + 1) * tkc, :], preferred_element_type=F32)

    col = lax.broadcasted_iota(jnp.int32, (tq, BLOCK), 1)
    s = jnp.where(col >= PAD_FRONT, _nt_dot(q, k_ref[0, 0, 0:BLOCK, :]), NEG_INF)
    s_next = scores(0)
    acc = jnp.dot(jnp.exp2(s).astype(BF16), vx_ref[0:BLOCK, :], preferred_element_type=F32)
    p_prev = None
    for c in range(n_chunks):
        s = s_next
        if c + 1 < n_chunks:
            s_next = scores(c + 1)
        p = jnp.exp2(s).astype(BF16)
        if p_prev is not None:
            acc = pv(c - 1, p_prev, acc)
        p_prev = p
    acc = pv(n_chunks - 1, p_prev, acc)
    _mla_finish(acc, gate_ref, o_ref)


def _mla_online(q_ref, k_ref, gate_ref, o_ref, vx_ref, *, tp, tkc):
    q = q_ref[0, 0]
    tq = q.shape[0]

    s = _nt_dot(q, k_ref[0, 0, 0:BLOCK, :])
    col = lax.broadcasted_iota(jnp.int32, (tq, BLOCK), 1)
    s = jnp.where(col >= PAD_FRONT, s, NEG_INF)
    m = jnp.max(s, axis=-1, keepdims=True)
    p = jnp.exp2(s - m)
    acc = jnp.dot(p.astype(BF16), vx_ref[0:BLOCK, :], preferred_element_type=F32)

    n_chunks = (tp - BLOCK) // tkc

    def scores(c):
        start = pl.multiple_of(BLOCK + c * tkc, BLOCK)
        return _nt_dot(q, k_ref[0, 0, pl.ds(start, tkc), :])

    def softmax_step(s, m):
        m_new = jnp.maximum(m, jnp.max(s, axis=-1, keepdims=True))
        alpha = jnp.exp2(m - m_new)
        p = jnp.exp2(s - m_new)
        return p.astype(BF16), alpha, m_new

    def pv_step(c, p, alpha, acc):
        start = pl.multiple_of(BLOCK + c * tkc, BLOCK)
        return alpha * acc + jnp.dot(p, vx_ref[pl.ds(start, tkc), :], preferred_element_type=F32)

    def chunk(c, carry):
        s, p_prev, alpha_prev, m, acc = carry
        s_next = scores(c + 1)
        p, alpha, m = softmax_step(s, m)
        acc = pv_step(c - 1, p_prev, alpha_prev, acc)
        return s_next, p, alpha, m, acc

    s1 = scores(1)
    p0, alpha0, m = softmax_step(scores(0), m)
    carry = (s1, p0, alpha0, m, acc)
    for c in range(1, n_chunks - 1):
        carry = chunk(c, carry)
    s, p_prev, alpha_prev, m, acc = carry
    p, alpha, m = softmax_step(s, m)
    acc = pv_step(n_chunks - 2, p_prev, alpha_prev, acc)
    acc = pv_step(n_chunks - 1, p, alpha, acc)
    _mla_finish(acc, gate_ref, o_ref)


def _mla_attn(fixed_flag, qm, km, vm, gate, *, batch, tp, tq, tkc):
    nq = tp // tq
    est = (2 * tp * (MLA_QK_PAD + MLA_V) * 2 + tp * MXU_DIM * 2 + 6 * tq * tkc * 4
           + 4 * tq * MLA_QK_PAD * 2)
    return pl.pallas_call(
        functools.partial(_mla_attn_kernel, tp=tp, tkc=tkc),
        name="mla_attn",
        grid=(batch, MLA_HEADS, nq),
        in_specs=[
            pl.BlockSpec(memory_space=pltpu.SMEM),
            pl.BlockSpec((1, 1, tq, MLA_QK_PAD), lambda b, h, i: (b, h, i, 0)),
            pl.BlockSpec((1, 1, tp, MLA_QK_PAD), lambda b, h, i: (b, h, 0, 0)),
            pl.BlockSpec((1, 1, tp, MLA_V), lambda b, h, i: (b, h, 0, 0)),
            pl.BlockSpec((1, tq, MLA_V), lambda b, h, i: (b, i, h)),
        ],
        out_specs=pl.BlockSpec((1, tq, MLA_V), lambda b, h, i: (b, i, h)),
        out_shape=jax.ShapeDtypeStruct((batch, tp, MLA_WIDTH), BF16),
        scratch_shapes=[pltpu.VMEM((tp, MXU_DIM), BF16)],
        compiler_params=pltpu.CompilerParams(
            dimension_semantics=("arbitrary", "arbitrary", "arbitrary"), vmem_limit_bytes=_vmem_limit(est)),
    )(fixed_flag, qm, km, vm, gate)


def _swa_masks(tp):
    band = 3 * BLOCK
    nb = tp // BLOCK
    r = (jnp.arange(SWA_REP * BLOCK, dtype=jnp.int32) % BLOCK)[:, None]
    c = jnp.arange(band, dtype=jnp.int32)[None, :]

    def mask(n):
        start = min(max((n - 1) * BLOCK, 0), tp - band)
        rq, rk = n * BLOCK + r, start + c
        return jnp.where((jnp.abs(rq - rk) <= SWA_WINDOW) & (rk >= BLOCK), 0.0, NEG_INF).astype(F32)

    band_bias = jnp.stack([mask(2), mask(0), mask(1), mask(nb - 1)])
    meta_bias = jnp.where(jnp.arange(BLOCK) >= PAD_FRONT, 0.0, NEG_INF).astype(F32)
    return band_bias, jnp.broadcast_to(meta_bias[None, :], (SWA_REP * BLOCK, BLOCK))


def _swa_attn_kernel(fixed_ref, q_ref, k_ref, v_ref, sink_ref, gate_ref, bias_ref, mbias_ref, o_ref, vx_ref,
                     *, tp, nblk):
    step = pl.program_id(2)

    @pl.when(step == 0)
    def _():
        lane = lax.broadcasted_iota(jnp.int32, (tp, MXU_DIM - SWA_HEAD_DIM), 1)
        vx_ref[:, 0:SWA_HEAD_DIM] = v_ref[0, 0]
        vx_ref[:, SWA_HEAD_DIM:] = jnp.where(lane == 0, 1.0, 0.0).astype(BF16)

    @pl.when(fixed_ref[0] == 1)
    def _():
        _swa_blocks(q_ref, k_ref, sink_ref, gate_ref, bias_ref, mbias_ref, o_ref, vx_ref,
                    tp=tp, nblk=nblk, row_max=False)

    @pl.when(fixed_ref[0] != 1)
    def _():
        _swa_blocks(q_ref, k_ref, sink_ref, gate_ref, bias_ref, mbias_ref, o_ref, vx_ref,
                    tp=tp, nblk=nblk, row_max=True)


def _swa_blocks(q_ref, k_ref, sink_ref, gate_ref, bias_ref, mbias_ref, o_ref, vx_ref, *, tp, nblk, row_max):
    step = pl.program_id(2)
    rows = SWA_REP * BLOCK
    band = 3 * BLOCK
    sink = sink_ref[0]
    g = _silu(gate_ref[0].astype(F32))
    starts, scores = [], []
    for u in range(nblk):
        n = step * nblk + u
        q = q_ref[0, :, u * BLOCK:(u + 1) * BLOCK, :].reshape(rows, SWA_HEAD_DIM)
        start = pl.multiple_of(jnp.clip((n - 1) * BLOCK, 0, tp - band), BLOCK)
        kind = jnp.where(n == 0, 1, jnp.where(n == 1, 2, jnp.where(n == tp // BLOCK - 1, 3, 0)))
        s_b = _nt_dot(q, k_ref[0, 0, pl.ds(start, band), :]) + bias_ref[kind]
        s_m = _nt_dot(q, k_ref[0, 0, 0:BLOCK, :]) + mbias_ref[...]
        starts.append(start)
        scores.append((s_b, s_m))

    for u in range(nblk):
        s_b, s_m = scores[u]
        sink_u = sink
        if row_max:
            mx = jnp.maximum(jnp.maximum(s_b[:, 0:BLOCK], s_b[:, BLOCK:2 * BLOCK]),
                             jnp.maximum(s_b[:, 2 * BLOCK:], s_m))
            m = jnp.maximum(jnp.max(mx, axis=-1, keepdims=True), sink)
            s_b, s_m, sink_u = s_b - m, s_m - m, sink - m
        p_b = jnp.exp2(s_b).astype(BF16)
        p_m = jnp.exp2(s_m).astype(BF16)
        acc = (jnp.dot(p_b, vx_ref[pl.ds(starts[u], band), :], preferred_element_type=F32)
               + jnp.dot(p_m, vx_ref[0:BLOCK, :], preferred_element_type=F32))
        l = jnp.exp2(sink_u) + acc[:, SWA_HEAD_DIM:SWA_HEAD_DIM + 1]
        o = acc[:, 0:SWA_HEAD_DIM] / l
        for r in range(SWA_REP):
            o_ref[0, u * BLOCK:(u + 1) * BLOCK, r * SWA_HEAD_DIM:(r + 1) * SWA_HEAD_DIM] = (
                o[r * BLOCK:(r + 1) * BLOCK]
                * g[u * BLOCK:(u + 1) * BLOCK, r * SWA_HEAD_DIM:(r + 1) * SWA_HEAD_DIM]).astype(BF16)


def _swa_attn(qs, ks, vs, sink, g_qn, g_kn, gate, *, batch, tp):
    nb = tp // BLOCK
    assert nb >= 4, "the four band-mask kinds assume at least four token blocks"
    nblk = 3 if nb % 3 == 0 else 1
    rows_q = nblk * BLOCK
    sink2 = sink.astype(F32) * LOG2E
    score_bound = 1.01 * LOG2E * math.sqrt(SWA_HEAD_DIM) * jnp.max(jnp.abs(g_qn)) * jnp.max(jnp.abs(g_kn))
    shift = jnp.maximum(score_bound, jnp.max(sink2))
    fixed_flag = (score_bound + shift <= 2.0 * MLA_FIXED_SHIFT_MAX).astype(jnp.int32).reshape(1)
    band_bias, meta_bias = _swa_masks(tp)
    band_bias, meta_bias = band_bias - shift, meta_bias - shift
    sink_rows = jnp.repeat(sink2 - shift, BLOCK).reshape(SWA_KV_HEADS, SWA_REP * BLOCK, 1)
    gcols = SWA_REP * SWA_HEAD_DIM
    est = 4 * tp * SWA_HEAD_DIM * 2 + tp * MXU_DIM * 2 + nblk * 12 * SWA_REP * BLOCK * 4 * BLOCK * 4
    return pl.pallas_call(
        functools.partial(_swa_attn_kernel, tp=tp, nblk=nblk),
        name="swa_attn",
        grid=(batch, SWA_KV_HEADS, nb // nblk),
        in_specs=[
            pl.BlockSpec(memory_space=pltpu.SMEM),
            pl.BlockSpec((1, SWA_REP, rows_q, SWA_HEAD_DIM), lambda b, g, n: (b, g, n, 0)),
            pl.BlockSpec((1, 1, tp, SWA_HEAD_DIM), lambda b, g, n: (b, g, 0, 0)),
            pl.BlockSpec((1, 1, tp, SWA_HEAD_DIM), lambda b, g, n: (b, g, 0, 0)),
            pl.BlockSpec((1, SWA_REP * BLOCK, 1), lambda b, g, n: (g, 0, 0)),
            pl.BlockSpec((1, rows_q, gcols), lambda b, g, n: (b, n, MLA_WIDTH // gcols + g)),
            _const_spec(band_bias.shape),
            _const_spec(meta_bias.shape),
        ],
        out_specs=pl.BlockSpec((1, rows_q, gcols), lambda b, g, n: (b, n, g)),
        out_shape=jax.ShapeDtypeStruct((batch, tp, SWA_WIDTH), BF16),
        scratch_shapes=[pltpu.VMEM((tp, MXU_DIM), BF16)],
        compiler_params=pltpu.CompilerParams(
            dimension_semantics=("arbitrary", "arbitrary", "arbitrary"), vmem_limit_bytes=_vmem_limit(est)),
    )(fixed_flag, qs, ks, vs, sink_rows, gate, band_bias, meta_bias)


def _even_out_kernel(ya_ref, yb_ref, w_ref, h_ref, o_ref, *, tm):
    i = pl.program_id(1)
    d = (jnp.dot(ya_ref[0], w_ref[0:MLA_WIDTH, :], preferred_element_type=F32)
         + jnp.dot(yb_ref[0], w_ref[MLA_WIDTH:, :], preferred_element_type=F32))
    row = i * tm + lax.broadcasted_iota(jnp.int32, (tm, D_MODEL), 0)
    o_ref[...] = h_ref[...] + jnp.where(row >= PAD_FRONT, d, 0.0)


def _even_out(ya, yb, w_out, h2d, *, batch, tp, tm):
    nt = tp // tm
    est = w_out.size * 2 + 4 * tm * D_MODEL * 2 + 6 * tm * D_MODEL * 4
    return pl.pallas_call(
        functools.partial(_even_out_kernel, tm=tm),
        name="even_out",
        grid=(batch, nt),
        in_specs=[
            pl.BlockSpec((1, tm, MLA_WIDTH), lambda b, i: (b, i, 0)),
            pl.BlockSpec((1, tm, SWA_WIDTH), lambda b, i: (b, i, 0)),
            _const_spec(w_out.shape),
            _h_tile_spec(h2d, tm),
        ],
        out_specs=pl.BlockSpec((tm, D_MODEL), lambda b, i: (i, b)),
        out_shape=jax.ShapeDtypeStruct((tp, batch * D_MODEL), F32),
        compiler_params=pltpu.CompilerParams(
            dimension_semantics=("arbitrary", "arbitrary"), vmem_limit_bytes=_vmem_limit(est)),
    )(ya, yb, w_out, h2d)


def _lru_coeffs(xc, xcb, wg_ref, bg_ref, lam_ref, a_ref, b_ref, t0=None):
    tmr = xc.shape[0]
    if t0 is not None:
        t_idx = t0 + (lax.broadcasted_iota(jnp.int32, (tmr, LRU_BLOCK_DIM), 0) // SUBLANE)
        live = t_idx >= PAD_FRONT
    for n in range(LRU_BLOCKS):
        sl = slice(n * LRU_BLOCK_DIM, (n + 1) * LRU_BLOCK_DIM)
        xn = xc[:, sl]
        gh = jnp.dot(xcb[:, sl], wg_ref[n], preferred_element_type=F32) + bg_ref[n]
        tr = jnp.tanh(gh[:, :LRU_BLOCK_DIM])
        ti = jnp.tanh(gh[:, LRU_BLOCK_DIM:])
        ch = (-0.5 * LRU_C) * jax.nn.softplus(-lam_ref[:, sl])
        log_a = tr * ch + ch
        a = jnp.exp(log_a)
        a_ref[:, sl] = a
        one_m_a2 = jnp.tanh(log_a) * (-1.0 - a * a)
        b = jnp.sqrt(one_m_a2) * (0.5 * ti + 0.5) * xn
        b_ref[:, sl] = b if t0 is None else jnp.where(live, b, 0.0)


def _odd_fwd_kernel(h_ref, halo_ref, gn_ref, w_ref, cw_ref, cb_ref, wg_ref, bg_ref, lam_ref,
                    xc_ref, gate_ref, hf_ref,
                    uprev_ref, hst_ref, a_ref, b_ref, *, tc, nt):
    j = pl.program_id(0)
    tmr = tc * SUBLANE
    halo_rows = SUBLANE
    prev_rows = 2 * SUBLANE

    @pl.when(j == 0)
    def _():
        uprev_ref[...] = jnp.zeros_like(uprev_ref)
        hst_ref[...] = jnp.zeros_like(hst_ref)

    x = jnp.concatenate([h_ref[...].reshape(tmr, D_MODEL), halo_ref[...].reshape(halo_rows, D_MODEL)], axis=0)
    z = (x * _rms_scale(x, D_MODEL) * gn_ref[...]).astype(BF16)
    u = jnp.dot(z, w_ref[:, 0:LRU_WIDTH], preferred_element_type=F32)
    gate_ref[...] = jnp.dot(z[0:tmr], w_ref[:, LRU_WIDTH:], preferred_element_type=F32).astype(BF16)

    u_next = jnp.where(j == nt - 1, 0.0, u[tmr:])
    ue = jnp.concatenate([uprev_ref[...], u[0:tmr], u_next], axis=0)
    uprev_ref[...] = u[tmr - prev_rows:tmr]
    xc = cb_ref[...] + ue[0:tmr] * cw_ref[0:1, :]
    for tap in range(1, CONV_WIDTH):
        xc = xc + ue[tap * SUBLANE:tap * SUBLANE + tmr] * cw_ref[tap:tap + 1, :]
    xcb = xc.astype(BF16)
    xc_ref[...] = xcb

    _lru_coeffs(xc, xcb, wg_ref, bg_ref, lam_ref, a_ref, b_ref, j * tc)

    def step(t, hs):
        r = pl.multiple_of(t * SUBLANE, SUBLANE)
        hs = a_ref[pl.ds(r, SUBLANE), :] * hs + b_ref[pl.ds(r, SUBLANE), :]
        b_ref[pl.ds(r, SUBLANE), :] = hs
        return hs

    hst_ref[...] = lax.fori_loop(0, tc, step, hst_ref[...], unroll=8)
    hf_ref[...] = b_ref[...].astype(BF16)


def _odd_fwd(h, gn, w_in, cw, cb, wg, bg, lam, *, batch, tp, tc):
    nt = tp // tc
    tmr = tc * batch
    blk = lambda cols: pl.BlockSpec((tc, batch, cols), lambda j: (j, 0, 0))
    row_blk = pl.BlockSpec((tmr, LRU_WIDTH), lambda j: (j, 0))
    est = (w_in.size * 2 + 2 * tmr * D_MODEL * 4 + 3 * 2 * tmr * LRU_WIDTH * 2
           + 2 * tmr * LRU_WIDTH * 4 + 5 * tmr * LRU_WIDTH * 4)
    return pl.pallas_call(
        functools.partial(_odd_fwd_kernel, tc=tc, nt=nt),
        name="odd_fwd",
        grid=(nt,),
        in_specs=[
            blk(D_MODEL),
            pl.BlockSpec((1, batch, D_MODEL), lambda j: (jnp.minimum((j + 1) * tc, tp - 1), 0, 0)),
            _const_spec((1, D_MODEL)),
            _const_spec(w_in.shape),
            _const_spec(cw.shape),
            _const_spec(cb.shape),
            _const_spec(wg.shape),
            _const_spec(bg.shape),
            _const_spec(lam.shape),
        ],
        out_specs=[row_blk, row_blk, row_blk],
        out_shape=[jax.ShapeDtypeStruct((tp * batch, LRU_WIDTH), BF16)] * 3,
        scratch_shapes=[pltpu.VMEM((2 * SUBLANE, LRU_WIDTH), F32),
                        pltpu.VMEM((SUBLANE, LRU_WIDTH), F32),
                        pltpu.VMEM((tmr, LRU_WIDTH), F32),
                        pltpu.VMEM((tmr, LRU_WIDTH), F32)],
        compiler_params=pltpu.CompilerParams(
            dimension_semantics=("arbitrary",), vmem_limit_bytes=_vmem_limit(est)),
    )(h, h, gn, w_in, cw, cb, wg, bg, lam)


def _odd_bwd_kernel(xc_ref, gate_ref, hf_ref, h_ref, wg_ref, bg_ref, lam_ref, w_ref, o_ref,
                    hst_ref, a_ref, b_ref, st_ref, *, tc, nt, final):
    j = pl.program_id(0)
    jj = nt - 1 - j
    tmr = tc * SUBLANE

    @pl.when(j == 0)
    def _():
        hst_ref[...] = jnp.zeros_like(hst_ref)

    xcb = xc_ref[...]
    _lru_coeffs(xcb.astype(F32), xcb, wg_ref, bg_ref, lam_ref, a_ref, b_ref)

    def step(s, hs):
        t = tc - 1 - s
        r = pl.multiple_of(t * SUBLANE, SUBLANE)
        hs = a_ref[pl.ds(r, SUBLANE), :] * hs + b_ref[pl.ds(r, SUBLANE), :]
        b_ref[pl.ds(r, SUBLANE), :] = hs
        return hs

    hst_ref[...] = lax.fori_loop(0, tc, step, hst_ref[...], unroll=8)

    y = ((hf_ref[...].astype(F32) + b_ref[...]) * _silu(gate_ref[...].astype(F32))).astype(BF16)
    d = jnp.dot(y, w_ref[...], preferred_element_type=F32)
    if not final:
        t_idx = jj * tc + (lax.broadcasted_iota(jnp.int32, (tmr, D_MODEL), 0) // SUBLANE)
        o_ref[...] = h_ref[...] + jnp.where(t_idx >= PAD_FRONT, d, 0.0).reshape(tc, SUBLANE, D_MODEL)
    else:
        @pl.when(jj >= BLOCK // tc)
        def _():
            res = h_ref[...].reshape(tmr, D_MODEL) + d
            for c in range(D_MODEL // LANE):
                st_ref[c] = res[:, c * LANE:(c + 1) * LANE]
                for b in range(SUBLANE):
                    o_ref[b, :, c * LANE:(c + 1) * LANE] = st_ref[c, pl.ds(b, tc, stride=SUBLANE), :]


def _odd_bwd(xc, gate, hf, h, wg, bg, lam, w_out, *, batch, tp, tc, final):
    nt = tp // tc
    tmr = tc * batch
    blk = lambda cols: pl.BlockSpec((tc, batch, cols), lambda j: (nt - 1 - j, 0, 0))
    if final:
        first = BLOCK // tc
        out_spec = pl.BlockSpec((batch, tc, D_MODEL), lambda j: (0, jnp.maximum(nt - 1 - j - first, 0), 0))
        out_shape = jax.ShapeDtypeStruct((batch, tp - BLOCK, D_MODEL), F32)
    else:
        out_spec = blk(D_MODEL)
        out_shape = jax.ShapeDtypeStruct((tp, batch, D_MODEL), F32)
    row_blk = pl.BlockSpec((tmr, LRU_WIDTH), lambda j: (nt - 1 - j, 0))
    est = (w_out.size * 2 + 3 * 2 * tmr * LRU_WIDTH * 2 + 4 * tmr * D_MODEL * 4
           + 2 * tmr * LRU_WIDTH * 4 + 5 * tmr * LRU_WIDTH * 4)
    return pl.pallas_call(
        functools.partial(_odd_bwd_kernel, tc=tc, nt=nt, final=final),
        name="odd_bwd",
        grid=(nt,),
        in_specs=[
            row_blk, row_blk, row_blk, blk(D_MODEL),
            _const_spec(wg.shape),
            _const_spec(bg.shape),
            _const_spec(lam.shape),
            _const_spec(w_out.shape),
        ],
        out_specs=out_spec,
        out_shape=out_shape,
        scratch_shapes=[pltpu.VMEM((SUBLANE, LRU_WIDTH), F32),
                        pltpu.VMEM((tmr, LRU_WIDTH), F32),
                        pltpu.VMEM((tmr, LRU_WIDTH), F32),
                        pltpu.VMEM((D_MODEL // LANE, tmr, LANE), F32)],
        compiler_params=pltpu.CompilerParams(
            dimension_semantics=("arbitrary",), vmem_limit_bytes=_vmem_limit(est)),
    )(xc, gate, hf, h, wg, bg, lam, w_out)


def _rope_tables(tp):
    pos = (jnp.arange(tp, dtype=F32) - PAD_FRONT)[:, None]

    def table(dim):
        inv = ROPE_THETA ** (-jnp.arange(0, dim, 2, dtype=F32) / dim)
        ang = pos * inv[None, :]
        return jnp.cos(ang), jnp.sin(ang)

    c, s = table(MLA_ROPE)
    zeros = jnp.zeros((tp, LANE - MLA_ROPE), F32)
    cm = jnp.concatenate([c, c, zeros], axis=1)
    sm = jnp.concatenate([-s, s, zeros], axis=1)
    c, s = table(SWA_HEAD_DIM)
    cs = jnp.concatenate([c, c], axis=1)
    ss = jnp.concatenate([-s, s], axis=1)
    return cm, sm, cs, ss


def _prep_even(w_in, w_uq, w_ukv, g_qn, g_kn):
    lat_w = MLA_Q_RANK + MLA_KV_RANK + MLA_ROPE
    w_all = jnp.concatenate(
        [w_in[:, :lat_w], jnp.zeros((D_MODEL, LAT_COLS - lat_w), w_in.dtype), w_in[:, lat_w:]], axis=1).astype(BF16)
    wuq = jnp.pad(w_uq, ((0, 0), (0, 0), (0, MLA_QK_PAD - MLA_QK))).reshape(
        MLA_Q_RANK, MLA_HEADS * MLA_QK_PAD).astype(BF16)
    wukv = w_ukv.reshape(MLA_KV_RANK, MLA_HEADS * (MLA_NOPE + MLA_V)).astype(BF16)
    shift = 1.01 * LOG2E * math.sqrt(MLA_QK) * jnp.max(jnp.abs(g_qn)) * jnp.max(jnp.abs(g_kn))
    lane = jnp.arange(MLA_QK_PAD) == MLA_QK
    pad_g = lambda g: jnp.pad(g, (0, MLA_QK_PAD - MLA_QK))
    gq2 = jnp.stack([pad_g(g_qn), jnp.where(lane, -shift, 0.0)]).astype(F32)
    gk2 = jnp.stack([pad_g(g_kn), jnp.where(lane, 1.0, 0.0)]).astype(F32)
    fixed_flag = (shift <= MLA_FIXED_SHIFT_MAX).astype(jnp.int32).reshape(1)
    return w_all, wuq, wukv, gq2, gk2, fixed_flag


def _prep_gates(w_a, b_a, w_x, b_x):
    wg = (0.5 * jnp.concatenate([w_a, w_x], axis=-1)).astype(BF16)
    bg = 0.5 * jnp.concatenate([b_a.reshape(LRU_BLOCKS, 1, LRU_BLOCK_DIM),
                                b_x.reshape(LRU_BLOCKS, 1, LRU_BLOCK_DIM)], axis=-1)
    return wg, bg


def _tile(tp, prefs):
    for t in prefs:
        if tp % t == 0:
            return t
    raise ValueError(f"no tile of {prefs} divides {tp}")


def kernel(x, meta_tokens, norm_g, even_w_in, mla_g_q_lat, mla_g_kv_lat, mla_w_uq, mla_w_ukv, mla_g_qn, mla_g_kn, swa_g_qn, swa_g_kn, swa_sink, even_w_out, odd_w_in, lru_conv_w, lru_conv_b, lru_w_a, lru_b_a, lru_w_x, lru_b_x, lru_lambda, odd_w_out):
    batch, seq, d = x.shape
    assert d == D_MODEL and batch == SUBLANE and seq % BLOCK == 0
    tp = BLOCK + seq
    tm = _tile(tp, (384, 128))
    tq = _tile(tp, (1056, 384, 128))
    tkc = _tile(seq, (512, 256, 128))
    assert seq // tkc >= 2, "the MLA key-chunk pipeline needs at least two chunks"
    tc = _tile(tp, (64, 32, 16))

    meta = jnp.broadcast_to(meta_tokens.astype(x.dtype)[None], (batch, N_META, D_MODEL))
    h = jnp.concatenate([jnp.zeros((batch, PAD_FRONT, D_MODEL), x.dtype), meta, x], axis=1)
    cm, sm, cs, ss = _rope_tables(tp)

    for l in range(DEPTH):
        j = l // 2
        gn = norm_g[l].reshape(1, D_MODEL)
        if l % 2 == 0:
            w_all, wuq, wukv, gqn, gkn, fixed_flag = _prep_even(even_w_in[j], mla_w_uq[j], mla_w_ukv[j],
                                                                mla_g_qn[j], mla_g_kn[j])
            h2d = h if l == 0 else h.reshape(tp, batch * D_MODEL)
            qm, km, vm, qs, ks, vs, gate = _even_in(
                h2d, gn, w_all, mla_g_q_lat[j].reshape(1, -1), mla_g_kv_lat[j].reshape(1, -1), wuq, wukv,
                gqn, gkn, swa_g_qn[j].reshape(1, -1), swa_g_kn[j].reshape(1, -1), cm, sm, cs, ss,
                batch=batch, tp=tp, tm=tm)
            ya = _mla_attn(fixed_flag, qm, km, vm, gate, batch=batch, tp=tp, tq=tq, tkc=tkc)
            yb = _swa_attn(qs, ks, vs, swa_sink[j], swa_g_qn[j], swa_g_kn[j], gate, batch=batch, tp=tp)
            h2d = _even_out(ya, yb, even_w_out[j].astype(BF16), h2d, batch=batch, tp=tp, tm=tm)
            h = h2d.reshape(tp, batch, D_MODEL)
        else:
            wgf, bgf = _prep_gates(lru_w_a[j, 0], lru_b_a[j, 0], lru_w_x[j, 0], lru_b_x[j, 0])
            wgb, bgb = _prep_gates(lru_w_a[j, 1], lru_b_a[j, 1], lru_w_x[j, 1], lru_b_x[j, 1])
            xc, gate, hf = _odd_fwd(h, gn, odd_w_in[j].astype(BF16), lru_conv_w[j], lru_conv_b[j].reshape(1, -1),
                                    wgf, bgf, lru_lambda[j, 0].reshape(1, -1), batch=batch, tp=tp, tc=tc)
            h = _odd_bwd(xc, gate, hf, h, wgb, bgb, lru_lambda[j, 1].reshape(1, -1), odd_w_out[j].astype(BF16),
                         batch=batch, tp=tp, tc=tc, final=(l == DEPTH - 1))
    return h
```

```python
import functools
import math

import jax
import jax.numpy as jnp
from jax import lax
from jax.experimental import pallas as pl
from jax.experimental.pallas import tpu as pltpu

F32 = jnp.float32
BF16 = jnp.bfloat16

D_MODEL = 1024
DEPTH = 4
N_META = 16
D_INNER = 2 * D_MODEL
ROPE_THETA = 10000.0
EPS = 1e-6
NEG_INF = -1e30

MLA_HEADS = 8
MLA_Q_RANK = 384
MLA_KV_RANK = 256
MLA_NOPE = 128
MLA_ROPE = 64
MLA_QK = MLA_NOPE + MLA_ROPE
MLA_V = 128
MLA_WIDTH = MLA_HEADS * MLA_V

SWA_HEADS = 8
SWA_KV_HEADS = 2
SWA_REP = SWA_HEADS // SWA_KV_HEADS
SWA_HEAD_DIM = 128
SWA_WINDOW = 128
SWA_WIDTH = SWA_HEADS * SWA_HEAD_DIM
SWA_KV_WIDTH = SWA_KV_HEADS * SWA_HEAD_DIM

LRU_WIDTH = D_INNER
LRU_BLOCKS = 16
LRU_BLOCK_DIM = LRU_WIDTH // LRU_BLOCKS
LRU_C = 8.0
CONV_WIDTH = 4

LANE = 128
SUBLANE = 8
MXU_DIM = 256
V7X_VMEM_BYTES = 64 * 1024 * 1024

BLOCK = 128
PAD_FRONT = BLOCK - N_META
MLA_QK_PAD = MXU_DIM
LAT_COLS = 768
SWA_COLS = SWA_WIDTH + 2 * SWA_KV_WIDTH
LOG2E = 1.4426950408889634
F32_MIN_NORMAL = 1.1754943508222875e-38
MLA_FIXED_SHIFT_MAX = 40.0


def _vmem_limit(nbytes):
    return int(min(V7X_VMEM_BYTES - (4 << 20), max(nbytes, 16 << 20)))


def _const_spec(shape):
    nd = len(shape)
    return pl.BlockSpec(shape, lambda *_: (0,) * nd, pipeline_mode=pl.Buffered(1))


def _rms_scale(x, n):
    return lax.rsqrt(jnp.sum(x * x, axis=-1, keepdims=True) * (1.0 / n) + EPS)


def _silu(g):
    gh = 0.5 * g
    return gh * (1.0 + jnp.tanh(gh))


def _nt_dot(a, b):
    return lax.dot_general(a, b, (((1,), (1,)), ((), ())), preferred_element_type=F32)


def _even_in_kernel(h_ref, gn_ref, w_ref, gql_ref, gkvl_ref, wuq_ref, wukv_ref, gqn_ref, gkn_ref,
                    sgq_ref, sgk_ref, cm_ref, sm_ref, cs_ref, ss_ref,
                    qm_ref, km_ref, vm_ref, qs_ref, ks_ref, vs_ref, gate_ref, zp_a_ref, zp_b_ref):
    n = pl.program_id(0)

    @pl.when(n == 0)
    def _():
        zp_b_ref[...] = jnp.zeros_like(zp_b_ref)

    @pl.when(n % 2 == 0)
    def _():
        _even_in_step(h_ref, gn_ref, w_ref, gql_ref, gkvl_ref, wuq_ref, wukv_ref, gqn_ref, gkn_ref,
                      sgq_ref, sgk_ref, cm_ref, sm_ref, cs_ref, ss_ref,
                      qm_ref, km_ref, vm_ref, qs_ref, ks_ref, vs_ref, gate_ref, zp_a_ref, zp_b_ref)

    @pl.when(n % 2 == 1)
    def _():
        _even_in_step(h_ref, gn_ref, w_ref, gql_ref, gkvl_ref, wuq_ref, wukv_ref, gqn_ref, gkn_ref,
                      sgq_ref, sgk_ref, cm_ref, sm_ref, cs_ref, ss_ref,
                      qm_ref, km_ref, vm_ref, qs_ref, ks_ref, vs_ref, gate_ref, zp_b_ref, zp_a_ref)


def _even_in_step(h_ref, gn_ref, w_ref, gql_ref, gkvl_ref, wuq_ref, wukv_ref, gqn_ref, gkn_ref,
                  sgq_ref, sgk_ref, cm_ref, sm_ref, cs_ref, ss_ref,
                  qm_ref, km_ref, vm_ref, qs_ref, ks_ref, vs_ref, gate_ref, zp_new_ref, zp_old_ref):
    cq = zp_old_ref[:, 0:MLA_Q_RANK]
    ckv = zp_old_ref[:, MLA_Q_RANK:MLA_Q_RANK + MLA_KV_RANK]
    kpe = zp_old_ref[:, MLA_Q_RANK + MLA_KV_RANK:LAT_COLS]
    cqn = (cq * _rms_scale(cq, MLA_Q_RANK) * gql_ref[...]).astype(BF16)
    ckvn = (ckv * _rms_scale(ckv, MLA_KV_RANK) * gkvl_ref[...]).astype(BF16)
    q_all = jnp.dot(cqn, wuq_ref[...], preferred_element_type=F32)
    kv = jnp.dot(ckvn, wukv_ref[...], preferred_element_type=F32)

    x = h_ref[...]
    z = (x * _rms_scale(x, D_MODEL) * gn_ref[...]).astype(BF16)
    zp_new_ref[...] = jnp.dot(z, w_ref[:, 0:LAT_COLS + SWA_COLS], preferred_element_type=F32)
    gate_ref[0] = jnp.dot(z, w_ref[:, LAT_COLS + SWA_COLS:], preferred_element_type=F32).astype(BF16)


    cm = cm_ref[...]
    sm = sm_ref[...]

    def rope_mla(t):
        rot = pltpu.roll(t, MLA_ROPE // 2, 1) + pltpu.roll(t, LANE - MLA_ROPE // 2, 1)
        return t * cm + rot * sm

    q_scale = (MLA_QK ** -0.5) * LOG2E
    gqn = gqn_ref[0:1, :]
    q_extra = gqn_ref[1:2, :]
    for hd in range(MLA_HEADS):
        qh = q_all[:, hd * MLA_QK_PAD:(hd + 1) * MLA_QK_PAD]
        qn = qh * _rms_scale(qh, MLA_QK) * gqn
        q_out = jnp.concatenate([qn[:, :MLA_NOPE], rope_mla(qn[:, MLA_NOPE:])], axis=-1) * q_scale + q_extra
        qm_ref[0, hd] = q_out.astype(BF16)

    gkn = gkn_ref[0:1, :]
    k_extra = gkn_ref[1:2, :]
    ss_pe = jnp.sum(kpe * kpe, axis=-1, keepdims=True)
    kr = rope_mla(kpe * gkn[:, MLA_NOPE:])
    for hd in range(MLA_HEADS):
        kn = kv[:, hd * MLA_QK_PAD:hd * MLA_QK_PAD + MLA_NOPE]
        rs = lax.rsqrt((jnp.sum(kn * kn, axis=-1, keepdims=True) + ss_pe) * (1.0 / MLA_QK) + EPS)
        k_out = jnp.concatenate([kn * rs * gkn[:, :MLA_NOPE], kr * rs], axis=-1) + k_extra
        km_ref[0, hd] = k_out.astype(BF16)
        vm_ref[0, hd] = kv[:, hd * MLA_QK_PAD + MLA_NOPE:(hd + 1) * MLA_QK_PAD].astype(BF16)

    sw = zp_old_ref.at[:, LAT_COLS:LAT_COLS + SWA_COLS]
    cs = cs_ref[...]
    ss = ss_ref[...]

    def norm_rope_swa(t, g):
        n = t * _rms_scale(t, SWA_HEAD_DIM) * g
        return n * cs + pltpu.roll(n, SWA_HEAD_DIM // 2, 1) * ss

    s_scale = (SWA_HEAD_DIM ** -0.5) * LOG2E
    for hd in range(SWA_HEADS):
        t = sw[:, hd * SWA_HEAD_DIM:(hd + 1) * SWA_HEAD_DIM]
        qs_ref[0, hd] = (norm_rope_swa(t, sgq_ref[...]) * s_scale).astype(BF16)
    for hd in range(SWA_KV_HEADS):
        t = sw[:, SWA_WIDTH + hd * SWA_HEAD_DIM:SWA_WIDTH + (hd + 1) * SWA_HEAD_DIM]
        ks_ref[0, hd] = norm_rope_swa(t, sgk_ref[...]).astype(BF16)
        vs_ref[0, hd] = sw[:, SWA_WIDTH + SWA_KV_WIDTH + hd * SWA_HEAD_DIM:
                           SWA_WIDTH + SWA_KV_WIDTH + (hd + 1) * SWA_HEAD_DIM].astype(BF16)


def _h_tile_spec(h, tm, tile_of=lambda b, i: (b, i)):
    if h.ndim == 3:
        return pl.BlockSpec((None, tm, D_MODEL), lambda *g: (*tile_of(*g), 0))
    return pl.BlockSpec((tm, D_MODEL), lambda *g: tile_of(*g)[::-1])


def _even_in(h2d, gn, w_all, gql, gkvl, wuq, wukv, gqn, gkn, sgq, sgk, cm, sm, cs, ss, *, batch, tp, tm):
    nt = tp // tm
    n_tiles = batch * nt
    new_tile = lambda n: divmod(jnp.minimum(n, n_tiles - 1), nt)
    old_tile = lambda n: divmod(jnp.maximum(n - 1, 0), nt)
    row_spec = lambda cols: pl.BlockSpec((tm, cols), lambda n: (old_tile(n)[1], 0))
    head_out = lambda heads, cols: pl.BlockSpec(
        (1, heads, tm, cols), lambda n: (old_tile(n)[0], 0, old_tile(n)[1], 0))
    n_w = w_all.shape[1]
    est = (w_all.size * 2 + (wuq.size + wukv.size) * 2 + 2 * tm * D_MODEL * 4
           + 2 * tm * 2 * (2 * MLA_HEADS * MLA_QK_PAD + MLA_WIDTH + SWA_WIDTH + 2 * SWA_KV_WIDTH)
           + 2 * tm * D_INNER * 2 + 2 * tm * (LAT_COLS + SWA_COLS) * 4 + 3 * tm * n_w * 4)
    return pl.pallas_call(
        _even_in_kernel,
        name="even_in",
        grid=(n_tiles + 1,),
        in_specs=[
            _h_tile_spec(h2d, tm, new_tile),
            _const_spec((1, D_MODEL)),
            _const_spec(w_all.shape),
            _const_spec((1, MLA_Q_RANK)),
            _const_spec((1, MLA_KV_RANK)),
            _const_spec(wuq.shape),
            _const_spec(wukv.shape),
            _const_spec((2, MLA_QK_PAD)),
            _const_spec((2, MLA_QK_PAD)),
            _const_spec((1, SWA_HEAD_DIM)),
            _const_spec((1, SWA_HEAD_DIM)),
            row_spec(LANE), row_spec(LANE), row_spec(LANE), row_spec(LANE),
        ],
        out_specs=[
            head_out(MLA_HEADS, MLA_QK_PAD),
            head_out(MLA_HEADS, MLA_QK_PAD),
            head_out(MLA_HEADS, MLA_V),
            head_out(SWA_HEADS, SWA_HEAD_DIM),
            head_out(SWA_KV_HEADS, SWA_HEAD_DIM),
            head_out(SWA_KV_HEADS, SWA_HEAD_DIM),
            pl.BlockSpec((1, tm, D_INNER), lambda n: (*new_tile(n), 0)),
        ],
        out_shape=[
            jax.ShapeDtypeStruct((batch, MLA_HEADS, tp, MLA_QK_PAD), BF16),
            jax.ShapeDtypeStruct((batch, MLA_HEADS, tp, MLA_QK_PAD), BF16),
            jax.ShapeDtypeStruct((batch, MLA_HEADS, tp, MLA_V), BF16),
            jax.ShapeDtypeStruct((batch, SWA_HEADS, tp, SWA_HEAD_DIM), BF16),
            jax.ShapeDtypeStruct((batch, SWA_KV_HEADS, tp, SWA_HEAD_DIM), BF16),
            jax.ShapeDtypeStruct((batch, SWA_KV_HEADS, tp, SWA_HEAD_DIM), BF16),
            jax.ShapeDtypeStruct((batch, tp, D_INNER), BF16),
        ],
        scratch_shapes=[pltpu.VMEM((tm, LAT_COLS + SWA_COLS), F32),
                        pltpu.VMEM((tm, LAT_COLS + SWA_COLS), F32)],
        compiler_params=pltpu.CompilerParams(
            dimension_semantics=("arbitrary",), vmem_limit_bytes=_vmem_limit(est)),
    )(h2d, gn, w_all, gql, gkvl, wuq, wukv, gqn, gkn, sgq, sgk, cm, sm, cs, ss)


def _mla_attn_kernel(fixed_ref, q_ref, k_ref, v_ref, gate_ref, o_ref, vx_ref, *, tp, tkc):
    @pl.when(pl.program_id(2) == 0)
    def _():
        lane = lax.broadcasted_iota(jnp.int32, (tp, MXU_DIM - MLA_V), 1)
        vx_ref[:, 0:MLA_V] = v_ref[0, 0]
        vx_ref[:, MLA_V:] = jnp.where(lane == 0, 1.0, 0.0).astype(BF16)

    @pl.when(fixed_ref[0] == 1)
    def _():
        _mla_fixed_shift(q_ref, k_ref, gate_ref, o_ref, vx_ref, tp=tp, tkc=tkc)

    @pl.when(fixed_ref[0] != 1)
    def _():
        _mla_online(q_ref, k_ref, gate_ref, o_ref, vx_ref, tp=tp, tkc=tkc)


def _mla_finish(acc, gate_ref, o_ref):
    o = acc[:, 0:MLA_V] / acc[:, MLA_V:MLA_V + 1]
    o_ref[0] = (o * _silu(gate_ref[0].astype(F32))).astype(BF16)


def _mla_fixed_shift(q_ref, k_ref, gate_ref, o_ref, vx_ref, *, tp, tkc):
    q = q_ref[0, 0]
    tq = q.shape[0]
    n_chunks = (tp - BLOCK) // tkc

    def scores(c):
        return _nt_dot(q, k_ref[0, 0, BLOCK + c * tkc:BLOCK + (c + 1) * tkc, :])

    def pv(c, p, acc):
        return acc + jnp.dot(p, vx_ref[BLOCK + c * tkc:BLOCK + (c + 1) * tkc, :], preferred_element_type=F32)

    col = lax.broadcasted_iota(jnp.int32, (tq, BLOCK), 1)
    s = jnp.where(col >= PAD_FRONT, _nt_dot(q, k_ref[0, 0, 0:BLOCK, :]), NEG_INF)
    s_next = scores(0)
    acc = jnp.dot(jnp.exp2(s).astype(BF16), vx_ref[0:BLOCK, :], preferred_element_type=F32)
    p_prev = None
    for c in range(n_chunks):
        s = s_next
        if c + 1 < n_chunks:
            s_next = scores(c + 1)
        p = jnp.exp2(s).astype(BF16)
        if p_prev is not None:
            acc = pv(c - 1, p_prev, acc)
        p_prev = p
    acc = pv(n_chunks - 1, p_prev, acc)
    _mla_finish(acc, gate_ref, o_ref)


def _mla_online(q_ref, k_ref, gate_ref, o_ref, vx_ref, *, tp, tkc):
    q = q_ref[0, 0]
    tq = q.shape[0]

    s = _nt_dot(q, k_ref[0, 0, 0:BLOCK, :])
    col = lax.broadcasted_iota(jnp.int32, (tq, BLOCK), 1)
    s = jnp.where(col >= PAD_FRONT, s, NEG_INF)
    m = jnp.max(s, axis=-1, keepdims=True)
    p = jnp.exp2(s - m)
    acc = jnp.dot(p.astype(BF16), vx_ref[0:BLOCK, :], preferred_element_type=F32)

    n_chunks = (tp - BLOCK) // tkc

    def scores(c):
        start = pl.multiple_of(BLOCK + c * tkc, BLOCK)
        return _nt_dot(q, k_ref[0, 0, pl.ds(start, tkc), :])

    def softmax_step(s, m):
        m_new = jnp.maximum(m, jnp.max(s, axis=-1, keepdims=True))
        alpha = jnp.exp2(m - m_new)
        p = jnp.exp2(s - m_new)
        return p.astype(BF16), alpha, m_new

    def pv_step(c, p, alpha, acc):
        start = pl.multiple_of(BLOCK + c * tkc, BLOCK)
        return alpha * acc + jnp.dot(p, vx_ref[pl.ds(start, tkc), :], preferred_element_type=F32)

    def chunk(c, carry):
        s, p_prev, alpha_prev, m, acc = carry
        s_next = scores(c + 1)
        p, alpha, m = softmax_step(s, m)
        acc = pv_step(c - 1, p_prev, alpha_prev, acc)
        return s_next, p, alpha, m, acc

    s1 = scores(1)
    p0, alpha0, m = softmax_step(scores(0), m)
    carry = (s1, p0, alpha0, m, acc)
    for c in range(1, n_chunks - 1):
        carry = chunk(c, carry)
    s, p_prev, alpha_prev, m, acc = carry
    p, alpha, m = softmax_step(s, m)
    acc = pv_step(n_chunks - 2, p_prev, alpha_prev, acc)
    acc = pv_step(n_chunks - 1, p, alpha, acc)
    _mla_finish(acc, gate_ref, o_ref)


def _mla_attn(fixed_flag, qm, km, vm, gate, *, batch, tp, tq, tkc):
    nq = tp // tq
    est = (2 * tp * (MLA_QK_PAD + MLA_V) * 2 + tp * MXU_DIM * 2 + 6 * tq * tkc * 4
           + 4 * tq * MLA_QK_PAD * 2)
    return pl.pallas_call(
        functools.partial(_mla_attn_kernel, tp=tp, tkc=tkc),
        name="mla_attn",
        grid=(batch, MLA_HEADS, nq),
        in_specs=[
            pl.BlockSpec(memory_space=pltpu.SMEM),
            pl.BlockSpec((1, 1, tq, MLA_QK_PAD), lambda b, h, i: (b, h, i, 0)),
            pl.BlockSpec((1, 1, tp, MLA_QK_PAD), lambda b, h, i: (b, h, 0, 0)),
            pl.BlockSpec((1, 1, tp, MLA_V), lambda b, h, i: (b, h, 0, 0)),
            pl.BlockSpec((1, tq, MLA_V), lambda b, h, i: (b, i, h)),
        ],
        out_specs=pl.BlockSpec((1, tq, MLA_V), lambda b, h, i: (b, i, h)),
        out_shape=jax.ShapeDtypeStruct((batch, tp, MLA_WIDTH), BF16),
        scratch_shapes=[pltpu.VMEM((tp, MXU_DIM), BF16)],
        compiler_params=pltpu.CompilerParams(
            dimension_semantics=("arbitrary", "arbitrary", "arbitrary"), vmem_limit_bytes=_vmem_limit(est)),
    )(fixed_flag, qm, km, vm, gate)


def _swa_masks(tp):
    band = 3 * BLOCK
    nb = tp // BLOCK
    r = (jnp.arange(SWA_REP * BLOCK, dtype=jnp.int32) % BLOCK)[:, None]
    c = jnp.arange(band, dtype=jnp.int32)[None, :]

    def mask(n):
        start = min(max((n - 1) * BLOCK, 0), tp - band)
        rq, rk = n * BLOCK + r, start + c
        return jnp.where((jnp.abs(rq - rk) <= SWA_WINDOW) & (rk >= BLOCK), 0.0, NEG_INF).astype(F32)

    band_bias = jnp.stack([mask(2), mask(0), mask(1), mask(nb - 1)])
    meta_bias = jnp.where(jnp.arange(BLOCK) >= PAD_FRONT, 0.0, NEG_INF).astype(F32)
    return band_bias, jnp.broadcast_to(meta_bias[None, :], (SWA_REP * BLOCK, BLOCK))


def _swa_attn_kernel(fixed_ref, q_ref, k_ref, v_ref, sink_ref, gate_ref, bias_ref, mbias_ref, o_ref, vx_ref,
                     *, tp, nblk):
    step = pl.program_id(2)

    @pl.when(step == 0)
    def _():
        lane = lax.broadcasted_iota(jnp.int32, (tp, MXU_DIM - SWA_HEAD_DIM), 1)
        vx_ref[:, 0:SWA_HEAD_DIM] = v_ref[0, 0]
        vx_ref[:, SWA_HEAD_DIM:] = jnp.where(lane == 0, 1.0, 0.0).astype(BF16)

    @pl.when(fixed_ref[0] == 1)
    def _():
        _swa_blocks(q_ref, k_ref, sink_ref, gate_ref, bias_ref, mbias_ref, o_ref, vx_ref,
                    tp=tp, nblk=nblk, row_max=False)

    @pl.when(fixed_ref[0] != 1)
    def _():
        _swa_blocks(q_ref, k_ref, sink_ref, gate_ref, bias_ref, mbias_ref, o_ref, vx_ref,
                    tp=tp, nblk=nblk, row_max=True)


def _swa_blocks(q_ref, k_ref, sink_ref, gate_ref, bias_ref, mbias_ref, o_ref, vx_ref, *, tp, nblk, row_max):
    step = pl.program_id(2)
    rows = SWA_REP * BLOCK
    band = 3 * BLOCK
    sink = sink_ref[0]
    g = _silu(gate_ref[0].astype(F32))
    starts, scores = [], []
    for u in range(nblk):
        n = step * nblk + u
        q = q_ref[0, :, u * BLOCK:(u + 1) * BLOCK, :].reshape(rows, SWA_HEAD_DIM)
        start = pl.multiple_of(jnp.clip((n - 1) * BLOCK, 0, tp - band), BLOCK)
        kind = jnp.where(n == 0, 1, jnp.where(n == 1, 2, jnp.where(n == tp // BLOCK - 1, 3, 0)))
        s_b = _nt_dot(q, k_ref[0, 0, pl.ds(start, band), :]) + bias_ref[kind]
        s_m = _nt_dot(q, k_ref[0, 0, 0:BLOCK, :]) + mbias_ref[...]
        starts.append(start)
        scores.append((s_b, s_m))

    for u in range(nblk):
        s_b, s_m = scores[u]
        sink_u = sink
        if row_max:
            mx = jnp.maximum(jnp.maximum(s_b[:, 0:BLOCK], s_b[:, BLOCK:2 * BLOCK]),
                             jnp.maximum(s_b[:, 2 * BLOCK:], s_m))
            m = jnp.maximum(jnp.max(mx, axis=-1, keepdims=True), sink)
            s_b, s_m, sink_u = s_b - m, s_m - m, sink - m
        p_b = jnp.exp2(s_b).astype(BF16)
        p_m = jnp.exp2(s_m).astype(BF16)
        acc = (jnp.dot(p_b, vx_ref[pl.ds(starts[u], band), :], preferred_element_type=F32)
               + jnp.dot(p_m, vx_ref[0:BLOCK, :], preferred_element_type=F32))
        l = jnp.exp2(sink_u) + acc[:, SWA_HEAD_DIM:SWA_HEAD_DIM + 1]
        o = acc[:, 0:SWA_HEAD_DIM] / l
        for r in range(SWA_REP):
            o_ref[0, u * BLOCK:(u + 1) * BLOCK, r * SWA_HEAD_DIM:(r + 1) * SWA_HEAD_DIM] = (
                o[r * BLOCK:(r + 1) * BLOCK]
                * g[u * BLOCK:(u + 1) * BLOCK, r * SWA_HEAD_DIM:(r + 1) * SWA_HEAD_DIM]).astype(BF16)


def _swa_attn(qs, ks, vs, sink, g_qn, g_kn, gate, *, batch, tp):
    nb = tp // BLOCK
    assert nb >= 4, "the four band-mask kinds assume at least four token blocks"
    nblk = 3 if nb % 3 == 0 else 1
    rows_q = nblk * BLOCK
    sink2 = sink.astype(F32) * LOG2E
    score_bound = 1.01 * LOG2E * math.sqrt(SWA_HEAD_DIM) * jnp.max(jnp.abs(g_qn)) * jnp.max(jnp.abs(g_kn))
    shift = jnp.maximum(score_bound, jnp.max(sink2))
    fixed_flag = (score_bound + shift <= 2.0 * MLA_FIXED_SHIFT_MAX).astype(jnp.int32).reshape(1)
    band_bias, meta_bias = _swa_masks(tp)
    band_bias, meta_bias = band_bias - shift, meta_bias - shift
    sink_rows = jnp.repeat(sink2 - shift, BLOCK).reshape(SWA_KV_HEADS, SWA_REP * BLOCK, 1)
    gcols = SWA_REP * SWA_HEAD_DIM
    est = 4 * tp * SWA_HEAD_DIM * 2 + tp * MXU_DIM * 2 + nblk * 12 * SWA_REP * BLOCK * 4 * BLOCK * 4
    return pl.pallas_call(
        functools.partial(_swa_attn_kernel, tp=tp, nblk=nblk),
        name="swa_attn",
        grid=(batch, SWA_KV_HEADS, nb // nblk),
        in_specs=[
            pl.BlockSpec(memory_space=pltpu.SMEM),
            pl.BlockSpec((1, SWA_REP, rows_q, SWA_HEAD_DIM), lambda b, g, n: (b, g, n, 0)),
            pl.BlockSpec((1, 1, tp, SWA_HEAD_DIM), lambda b, g, n: (b, g, 0, 0)),
            pl.BlockSpec((1, 1, tp, SWA_HEAD_DIM), lambda b, g, n: (b, g, 0, 0)),
            pl.BlockSpec((1, SWA_REP * BLOCK, 1), lambda b, g, n: (g, 0, 0)),
            pl.BlockSpec((1, rows_q, gcols), lambda b, g, n: (b, n, MLA_WIDTH // gcols + g)),
            _const_spec(band_bias.shape),
            _const_spec(meta_bias.shape),
        ],
        out_specs=pl.BlockSpec((1, rows_q, gcols), lambda b, g, n: (b, n, g)),
        out_shape=jax.ShapeDtypeStruct((batch, tp, SWA_WIDTH), BF16),
        scratch_shapes=[pltpu.VMEM((tp, MXU_DIM), BF16)],
        compiler_params=pltpu.CompilerParams(
            dimension_semantics=("arbitrary", "arbitrary", "arbitrary"), vmem_limit_bytes=_vmem_limit(est)),
    )(fixed_flag, qs, ks, vs, sink_rows, gate, band_bias, meta_bias)


def _even_out_kernel(ya_ref, yb_ref, w_ref, h_ref, o_ref, *, tm):
    i = pl.program_id(1)
    d = (jnp.dot(ya_ref[0], w_ref[0:MLA_WIDTH, :], preferred_element_type=F32)
         + jnp.dot(yb_ref[0], w_ref[MLA_WIDTH:, :], preferred_element_type=F32))
    row = i * tm + lax.broadcasted_iota(jnp.int32, (tm, D_MODEL), 0)
    o_ref[...] = h_ref[...] + jnp.where(row >= PAD_FRONT, d, 0.0)


def _even_out(ya, yb, w_out, h2d, *, batch, tp, tm):
    nt = tp // tm
    est = w_out.size * 2 + 4 * tm * D_MODEL * 2 + 6 * tm * D_MODEL * 4
    return pl.pallas_call(
        functools.partial(_even_out_kernel, tm=tm),
        name="even_out",
        grid=(batch, nt),
        in_specs=[
            pl.BlockSpec((1, tm, MLA_WIDTH), lambda b, i: (b, i, 0)),
            pl.BlockSpec((1, tm, SWA_WIDTH), lambda b, i: (b, i, 0)),
            _const_spec(w_out.shape),
            _h_tile_spec(h2d, tm),
        ],
        out_specs=pl.BlockSpec((tm, D_MODEL), lambda b, i: (i, b)),
        out_shape=jax.ShapeDtypeStruct((tp, batch * D_MODEL), F32),
        compiler_params=pltpu.CompilerParams(
            dimension_semantics=("arbitrary", "arbitrary"), vmem_limit_bytes=_vmem_limit(est)),
    )(ya, yb, w_out, h2d)


def _lru_coeffs(xc, xcb, wg_ref, bg_ref, lam_ref, a_ref, b_ref, t0=None):
    tmr = xc.shape[0]
    if t0 is not None:
        t_idx = t0 + (lax.broadcasted_iota(jnp.int32, (tmr, LRU_BLOCK_DIM), 0) // SUBLANE)
        live = t_idx >= PAD_FRONT
    for n in range(LRU_BLOCKS):
        sl = slice(n * LRU_BLOCK_DIM, (n + 1) * LRU_BLOCK_DIM)
        xn = xc[:, sl]
        gh = jnp.dot(xcb[:, sl], wg_ref[n], preferred_element_type=F32) + bg_ref[n]
        tr = jnp.tanh(gh[:, :LRU_BLOCK_DIM])
        ti = jnp.tanh(gh[:, LRU_BLOCK_DIM:])
        ch = (-0.5 * LRU_C) * jax.nn.softplus(-lam_ref[:, sl])
        log_a = tr * ch + ch
        a = jnp.exp(log_a)
        a_ref[:, sl] = a
        one_m_a2 = jnp.tanh(log_a) * (-1.0 - a * a)
        root = one_m_a2 * lax.rsqrt(jnp.maximum(one_m_a2, F32_MIN_NORMAL))
        b = root * (0.5 * ti + 0.5) * xn
        b_ref[:, sl] = b if t0 is None else jnp.where(live, b, 0.0)


def _odd_fwd_kernel(h_ref, halo_ref, gn_ref, w_ref, cw_ref, cb_ref, wg_ref, bg_ref, lam_ref,
                    xc_ref, gate_ref, hf_ref,
                    uprev_ref, hst_ref, a_ref, b_ref, *, tc, nt):
    j = pl.program_id(0)
    tmr = tc * SUBLANE
    halo_rows = SUBLANE
    prev_rows = 2 * SUBLANE

    @pl.when(j == 0)
    def _():
        uprev_ref[...] = jnp.zeros_like(uprev_ref)
        hst_ref[...] = jnp.zeros_like(hst_ref)

    x = jnp.concatenate([h_ref[...].reshape(tmr, D_MODEL), halo_ref[...].reshape(halo_rows, D_MODEL)], axis=0)
    z = (x * _rms_scale(x, D_MODEL) * gn_ref[...]).astype(BF16)
    u = jnp.dot(z, w_ref[:, 0:LRU_WIDTH], preferred_element_type=F32)
    gate_ref[...] = jnp.dot(z[0:tmr], w_ref[:, LRU_WIDTH:], preferred_element_type=F32).astype(BF16)

    u_next = jnp.where(j == nt - 1, 0.0, u[tmr:])
    ue = jnp.concatenate([uprev_ref[...], u[0:tmr], u_next], axis=0)
    uprev_ref[...] = u[tmr - prev_rows:tmr]
    xc = cb_ref[...] + ue[0:tmr] * cw_ref[0:1, :]
    for tap in range(1, CONV_WIDTH):
        xc = xc + ue[tap * SUBLANE:tap * SUBLANE + tmr] * cw_ref[tap:tap + 1, :]
    xcb = xc.astype(BF16)
    xc_ref[...] = xcb

    _lru_coeffs(xc, xcb, wg_ref, bg_ref, lam_ref, a_ref, b_ref, j * tc)

    def step(t, hs):
        r = pl.multiple_of(t * SUBLANE, SUBLANE)
        hs = a_ref[pl.ds(r, SUBLANE), :] * hs + b_ref[pl.ds(r, SUBLANE), :]
        b_ref[pl.ds(r, SUBLANE), :] = hs
        return hs

    hst_ref[...] = lax.fori_loop(0, tc, step, hst_ref[...], unroll=8)
    hf_ref[...] = b_ref[...].astype(BF16)


def _odd_fwd(h, gn, w_in, cw, cb, wg, bg, lam, *, batch, tp, tc):
    nt = tp // tc
    tmr = tc * batch
    blk = lambda cols: pl.BlockSpec((tc, batch, cols), lambda j: (j, 0, 0))
    row_blk = pl.BlockSpec((tmr, LRU_WIDTH), lambda j: (j, 0))
    est = (w_in.size * 2 + 2 * tmr * D_MODEL * 4 + 3 * 2 * tmr * LRU_WIDTH * 2
           + 2 * tmr * LRU_WIDTH * 4 + 5 * tmr * LRU_WIDTH * 4)
    return pl.pallas_call(
        functools.partial(_odd_fwd_kernel, tc=tc, nt=nt),
        name="odd_fwd",
        grid=(nt,),
        in_specs=[
            blk(D_MODEL),
            pl.BlockSpec((1, batch, D_MODEL), lambda j: (jnp.minimum((j + 1) * tc, tp - 1), 0, 0)),
            _const_spec((1, D_MODEL)),
            _const_spec(w_in.shape),
            _const_spec(cw.shape),
            _const_spec(cb.shape),
            _const_spec(wg.shape),
            _const_spec(bg.shape),
            _const_spec(lam.shape),
        ],
        out_specs=[row_blk, row_blk, row_blk],
        out_shape=[jax.ShapeDtypeStruct((tp * batch, LRU_WIDTH), BF16)] * 3,
        scratch_shapes=[pltpu.VMEM((2 * SUBLANE, LRU_WIDTH), F32),
                        pltpu.VMEM((SUBLANE, LRU_WIDTH), F32),
                        pltpu.VMEM((tmr, LRU_WIDTH), F32),
                        pltpu.VMEM((tmr, LRU_WIDTH), F32)],
        compiler_params=pltpu.CompilerParams(
            dimension_semantics=("arbitrary",), vmem_limit_bytes=_vmem_limit(est)),
    )(h, h, gn, w_in, cw, cb, wg, bg, lam)


def _odd_bwd_kernel(xc_ref, gate_ref, hf_ref, h_ref, wg_ref, bg_ref, lam_ref, w_ref, o_ref,
                    hst_ref, a_ref, b_ref, st_ref, *, tc, nt, final):
    j = pl.program_id(0)
    jj = nt - 1 - j
    tmr = tc * SUBLANE

    @pl.when(j == 0)
    def _():
        hst_ref[...] = jnp.zeros_like(hst_ref)

    xcb = xc_ref[...]
    _lru_coeffs(xcb.astype(F32), xcb, wg_ref, bg_ref, lam_ref, a_ref, b_ref)

    def step(s, hs):
        t = tc - 1 - s
        r = pl.multiple_of(t * SUBLANE, SUBLANE)
        hs = a_ref[pl.ds(r, SUBLANE), :] * hs + b_ref[pl.ds(r, SUBLANE), :]
        b_ref[pl.ds(r, SUBLANE), :] = hs
        return hs

    hst_ref[...] = lax.fori_loop(0, tc, step, hst_ref[...], unroll=8)

    y = ((hf_ref[...].astype(F32) + b_ref[...]) * _silu(gate_ref[...].astype(F32))).astype(BF16)
    d = jnp.dot(y, w_ref[...], preferred_element_type=F32)
    if not final:
        t_idx = jj * tc + (lax.broadcasted_iota(jnp.int32, (tmr, D_MODEL), 0) // SUBLANE)
        o_ref[...] = h_ref[...] + jnp.where(t_idx >= PAD_FRONT, d, 0.0).reshape(tc, SUBLANE, D_MODEL)
    else:
        @pl.when(jj >= BLOCK // tc)
        def _():
            res = h_ref[...].reshape(tmr, D_MODEL) + d
            for c in range(D_MODEL // LANE):
                st_ref[c] = res[:, c * LANE:(c + 1) * LANE]
                for b in range(SUBLANE):
                    o_ref[b, :, c * LANE:(c + 1) * LANE] = st_ref[c, pl.ds(b, tc, stride=SUBLANE), :]


def _odd_bwd(xc, gate, hf, h, wg, bg, lam, w_out, *, batch, tp, tc, final):
    nt = tp // tc
    tmr = tc * batch
    blk = lambda cols: pl.BlockSpec((tc, batch, cols), lambda j: (nt - 1 - j, 0, 0))
    if final:
        first = BLOCK // tc
        out_spec = pl.BlockSpec((batch, tc, D_MODEL), lambda j: (0, jnp.maximum(nt - 1 - j - first, 0), 0))
        out_shape = jax.ShapeDtypeStruct((batch, tp - BLOCK, D_MODEL), F32)
    else:
        out_spec = blk(D_MODEL)
        out_shape = jax.ShapeDtypeStruct((tp, batch, D_MODEL), F32)
    row_blk = pl.BlockSpec((tmr, LRU_WIDTH), lambda j: (nt - 1 - j, 0))
    est = (w_out.size * 2 + 3 * 2 * tmr * LRU_WIDTH * 2 + 4 * tmr * D_MODEL * 4
           + 2 * tmr * LRU_WIDTH * 4 + 5 * tmr * LRU_WIDTH * 4)
    return pl.pallas_call(
        functools.partial(_odd_bwd_kernel, tc=tc, nt=nt, final=final),
        name="odd_bwd",
        grid=(nt,),
        in_specs=[
            row_blk, row_blk, row_blk, blk(D_MODEL),
            _const_spec(wg.shape),
            _const_spec(bg.shape),
            _const_spec(lam.shape),
            _const_spec(w_out.shape),
        ],
        out_specs=out_spec,
        out_shape=out_shape,
        scratch_shapes=[pltpu.VMEM((SUBLANE, LRU_WIDTH), F32),
                        pltpu.VMEM((tmr, LRU_WIDTH), F32),
                        pltpu.VMEM((tmr, LRU_WIDTH), F32),
                        pltpu.VMEM((D_MODEL // LANE, tmr, LANE), F32)],
        compiler_params=pltpu.CompilerParams(
            dimension_semantics=("arbitrary",), vmem_limit_bytes=_vmem_limit(est)),
    )(xc, gate, hf, h, wg, bg, lam, w_out)


def _rope_tables(tp):
    pos = (jnp.arange(tp, dtype=F32) - PAD_FRONT)[:, None]

    def table(dim):
        inv = ROPE_THETA ** (-jnp.arange(0, dim, 2, dtype=F32) / dim)
        ang = pos * inv[None, :]
        return jnp.cos(ang), jnp.sin(ang)

    c, s = table(MLA_ROPE)
    zeros = jnp.zeros((tp, LANE - MLA_ROPE), F32)
    cm = jnp.concatenate([c, c, zeros], axis=1)
    sm = jnp.concatenate([-s, s, zeros], axis=1)
    c, s = table(SWA_HEAD_DIM)
    cs = jnp.concatenate([c, c], axis=1)
    ss = jnp.concatenate([-s, s], axis=1)
    return cm, sm, cs, ss


def _prep_even(w_in, w_uq, w_ukv, g_qn, g_kn):
    lat_w = MLA_Q_RANK + MLA_KV_RANK + MLA_ROPE
    w_bf = w_in.astype(BF16)
    w_all = jnp.concatenate(
        [w_bf[:, :lat_w], jnp.zeros((D_MODEL, LAT_COLS - lat_w), BF16), w_bf[:, lat_w:]], axis=1)
    wuq = jnp.pad(w_uq, ((0, 0), (0, 0), (0, MLA_QK_PAD - MLA_QK))).reshape(
        MLA_Q_RANK, MLA_HEADS * MLA_QK_PAD).astype(BF16)
    wukv = w_ukv.reshape(MLA_KV_RANK, MLA_HEADS * (MLA_NOPE + MLA_V)).astype(BF16)
    shift = 1.01 * LOG2E * math.sqrt(MLA_QK) * jnp.max(jnp.abs(g_qn)) * jnp.max(jnp.abs(g_kn))
    lane = jnp.arange(MLA_QK_PAD) == MLA_QK
    pad_g = lambda g: jnp.pad(g, (0, MLA_QK_PAD - MLA_QK))
    gq2 = jnp.stack([pad_g(g_qn), jnp.where(lane, -shift, 0.0)]).astype(F32)
    gk2 = jnp.stack([pad_g(g_kn), jnp.where(lane, 1.0, 0.0)]).astype(F32)
    fixed_flag = (shift <= MLA_FIXED_SHIFT_MAX).astype(jnp.int32).reshape(1)
    return w_all, wuq, wukv, gq2, gk2, fixed_flag


def _prep_gates(w_a, b_a, w_x, b_x):
    wg = (0.5 * jnp.concatenate([w_a, w_x], axis=-1)).astype(BF16)
    bg = 0.5 * jnp.concatenate([b_a.reshape(LRU_BLOCKS, 1, LRU_BLOCK_DIM),
                                b_x.reshape(LRU_BLOCKS, 1, LRU_BLOCK_DIM)], axis=-1)
    return wg, bg


def _tile(tp, prefs):
    for t in prefs:
        if tp % t == 0:
            return t
    raise ValueError(f"no tile of {prefs} divides {tp}")


def kernel(x, meta_tokens, norm_g, even_w_in, mla_g_q_lat, mla_g_kv_lat, mla_w_uq, mla_w_ukv, mla_g_qn, mla_g_kn, swa_g_qn, swa_g_kn, swa_sink, even_w_out, odd_w_in, lru_conv_w, lru_conv_b, lru_w_a, lru_b_a, lru_w_x, lru_b_x, lru_lambda, odd_w_out):
    batch, seq, d = x.shape
    assert d == D_MODEL and batch == SUBLANE and seq % BLOCK == 0
    tp = BLOCK + seq
    tm = _tile(tp, (384, 128))
    tq = _tile(tp, (1056, 384, 128))
    tkc = _tile(seq, (512, 256, 128))
    assert seq // tkc >= 2, "the MLA key-chunk pipeline needs at least two chunks"
    tc = _tile(tp, (64, 32, 16))

    meta = jnp.broadcast_to(meta_tokens.astype(x.dtype)[None], (batch, N_META, D_MODEL))
    h = jnp.concatenate([jnp.zeros((batch, PAD_FRONT, D_MODEL), x.dtype), meta, x], axis=1)
    cm, sm, cs, ss = _rope_tables(tp)

    even_w_in_bf = even_w_in.astype(BF16)
    for l in range(DEPTH):
        j = l // 2
        gn = norm_g[l].reshape(1, D_MODEL)
        if l % 2 == 0:
            w_all, wuq, wukv, gqn, gkn, fixed_flag = _prep_even(even_w_in_bf[j], mla_w_uq[j], mla_w_ukv[j],
                                                                mla_g_qn[j], mla_g_kn[j])
            h2d = h if l == 0 else h.reshape(tp, batch * D_MODEL)
            qm, km, vm, qs, ks, vs, gate = _even_in(
                h2d, gn, w_all, mla_g_q_lat[j].reshape(1, -1), mla_g_kv_lat[j].reshape(1, -1), wuq, wukv,
                gqn, gkn, swa_g_qn[j].reshape(1, -1), swa_g_kn[j].reshape(1, -1), cm, sm, cs, ss,
                batch=batch, tp=tp, tm=tm)
            ya = _mla_attn(fixed_flag, qm, km, vm, gate, batch=batch, tp=tp, tq=tq, tkc=tkc)
            yb = _swa_attn(qs, ks, vs, swa_sink[j], swa_g_qn[j], swa_g_kn[j], gate, batch=batch, tp=tp)
            h2d = _even_out(ya, yb, even_w_out[j].astype(BF16), h2d, batch=batch, tp=tp, tm=tm)
            h = h2d.reshape(tp, batch, D_MODEL)
        else:
            wgf, bgf = _prep_gates(lru_w_a[j, 0], lru_b_a[j, 0], lru_w_x[j, 0], lru_b_x[j, 0])
            wgb, bgb = _prep_gates(lru_w_a[j, 1], lru_b_a[j, 1], lru_w_x[j, 1], lru_b_x[j, 1])
            xc, gate, hf = _odd_fwd(h, gn, odd_w_in[j].astype(BF16), lru_conv_w[j], lru_conv_b[j].reshape(1, -1),
                                    wgf, bgf, lru_lambda[j, 0].reshape(1, -1), batch=batch, tp=tp, tc=tc)
            h = _odd_bwd(xc, gate, hf, h, wgb, bgb, lru_lambda[j, 1].reshape(1, -1), odd_w_out[j].astype(BF16),
                         batch=batch, tp=tp, tc=tc, final=(l == DEPTH - 1))
    return h
```

```python
import functools
import math

import jax
import jax.numpy as jnp
from jax import lax
from jax.experimental import pallas as pl
from jax.experimental.pallas import tpu as pltpu

F32 = jnp.float32
BF16 = jnp.bfloat16

D_MODEL = 1024
DEPTH = 4
N_META = 16
D_INNER = 2 * D_MODEL
ROPE_THETA = 10000.0
EPS = 1e-6
NEG_INF = -1e30

MLA_HEADS = 8
MLA_Q_RANK = 384
MLA_KV_RANK = 256
MLA_NOPE = 128
MLA_ROPE = 64
MLA_QK = MLA_NOPE + MLA_ROPE
MLA_V = 128
MLA_WIDTH = MLA_HEADS * MLA_V

SWA_HEADS = 8
SWA_KV_HEADS = 2
SWA_REP = SWA_HEADS // SWA_KV_HEADS
SWA_HEAD_DIM = 128
SWA_WINDOW = 128
SWA_WIDTH = SWA_HEADS * SWA_HEAD_DIM
SWA_KV_WIDTH = SWA_KV_HEADS * SWA_HEAD_DIM

LRU_WIDTH = D_INNER
LRU_BLOCKS = 16
LRU_BLOCK_DIM = LRU_WIDTH // LRU_BLOCKS
LRU_C = 8.0
CONV_WIDTH = 4

LANE = 128
SUBLANE = 8
MXU_DIM = 256
V7X_VMEM_BYTES = 64 * 1024 * 1024

BLOCK = 128
PAD_FRONT = BLOCK - N_META
MLA_QK_PAD = MXU_DIM
LAT_COLS = 768
SWA_COLS = SWA_WIDTH + 2 * SWA_KV_WIDTH
LOG2E = 1.4426950408889634
F32_MIN_NORMAL = 1.1754943508222875e-38
MLA_FIXED_SHIFT_MAX = 40.0


def _vmem_limit(nbytes):
    return int(min(V7X_VMEM_BYTES - (4 << 20), max(nbytes, 16 << 20)))


def _const_spec(shape):
    nd = len(shape)
    return pl.BlockSpec(shape, lambda *_: (0,) * nd, pipeline_mode=pl.Buffered(1))


def _rms_scale(x, n):
    return lax.rsqrt(jnp.sum(x * x, axis=-1, keepdims=True) * (1.0 / n) + EPS)


def _silu(g):
    gh = 0.5 * g
    return gh * (1.0 + jnp.tanh(gh))


def _nt_dot(a, b):
    return lax.dot_general(a, b, (((1,), (1,)), ((), ())), preferred_element_type=F32)


def _even_in_kernel(h_ref, gn_ref, w_ref, gql_ref, gkvl_ref, wuq_ref, wukv_ref, gqn_ref, gkn_ref,
                    sgq_ref, sgk_ref, cm_ref, sm_ref, cs_ref, ss_ref,
                    qm_ref, km_ref, vm_ref, qs_ref, ks_ref, vs_ref, gate_ref, zp_a_ref, zp_b_ref):
    n = pl.program_id(0)

    @pl.when(n == 0)
    def _():
        zp_b_ref[...] = jnp.zeros_like(zp_b_ref)

    @pl.when(n % 2 == 0)
    def _():
        _even_in_step(h_ref, gn_ref, w_ref, gql_ref, gkvl_ref, wuq_ref, wukv_ref, gqn_ref, gkn_ref,
                      sgq_ref, sgk_ref, cm_ref, sm_ref, cs_ref, ss_ref,
                      qm_ref, km_ref, vm_ref, qs_ref, ks_ref, vs_ref, gate_ref, zp_a_ref, zp_b_ref)

    @pl.when(n % 2 == 1)
    def _():
        _even_in_step(h_ref, gn_ref, w_ref, gql_ref, gkvl_ref, wuq_ref, wukv_ref, gqn_ref, gkn_ref,
                      sgq_ref, sgk_ref, cm_ref, sm_ref, cs_ref, ss_ref,
                      qm_ref, km_ref, vm_ref, qs_ref, ks_ref, vs_ref, gate_ref, zp_b_ref, zp_a_ref)


def _even_in_step(h_ref, gn_ref, w_ref, gql_ref, gkvl_ref, wuq_ref, wukv_ref, gqn_ref, gkn_ref,
                  sgq_ref, sgk_ref, cm_ref, sm_ref, cs_ref, ss_ref,
                  qm_ref, km_ref, vm_ref, qs_ref, ks_ref, vs_ref, gate_ref, zp_new_ref, zp_old_ref):
    cq = zp_old_ref[:, 0:MLA_Q_RANK]
    ckv = zp_old_ref[:, MLA_Q_RANK:MLA_Q_RANK + MLA_KV_RANK]
    kpe = zp_old_ref[:, MLA_Q_RANK + MLA_KV_RANK:LAT_COLS]
    cqn = (cq * _rms_scale(cq, MLA_Q_RANK) * gql_ref[...]).astype(BF16)
    ckvn = (ckv * _rms_scale(ckv, MLA_KV_RANK) * gkvl_ref[...]).astype(BF16)
    q_all = jnp.dot(cqn, wuq_ref[...], preferred_element_type=F32)
    kv = jnp.dot(ckvn, wukv_ref[...], preferred_element_type=F32)

    x = h_ref[...]
    z = (x * _rms_scale(x, D_MODEL) * gn_ref[...]).astype(BF16)
    zp_new_ref[...] = jnp.dot(z, w_ref[:, 0:LAT_COLS + SWA_COLS], preferred_element_type=F32)
    gate_ref[0] = jnp.dot(z, w_ref[:, LAT_COLS + SWA_COLS:], preferred_element_type=F32).astype(BF16)


    cm = cm_ref[...]
    sm = sm_ref[...]

    def rope_mla(t):
        rot = pltpu.roll(t, MLA_ROPE // 2, 1) + pltpu.roll(t, LANE - MLA_ROPE // 2, 1)
        return t * cm + rot * sm

    q_scale = (MLA_QK ** -0.5) * LOG2E
    gqn = gqn_ref[0:1, :]
    q_extra = gqn_ref[1:2, :]
    for hd in range(MLA_HEADS):
        qh = q_all[:, hd * MLA_QK_PAD:(hd + 1) * MLA_QK_PAD]
        qn = qh * _rms_scale(qh, MLA_QK) * gqn
        q_out = jnp.concatenate([qn[:, :MLA_NOPE], rope_mla(qn[:, MLA_NOPE:])], axis=-1) * q_scale + q_extra
        qm_ref[0, hd] = q_out.astype(BF16)

    gkn = gkn_ref[0:1, :]
    k_extra = gkn_ref[1:2, :]
    ss_pe = jnp.sum(kpe * kpe, axis=-1, keepdims=True)
    kr = rope_mla(kpe * gkn[:, MLA_NOPE:])
    for hd in range(MLA_HEADS):
        kn = kv[:, hd * MLA_QK_PAD:hd * MLA_QK_PAD + MLA_NOPE]
        rs = lax.rsqrt((jnp.sum(kn * kn, axis=-1, keepdims=True) + ss_pe) * (1.0 / MLA_QK) + EPS)
        k_out = jnp.concatenate([kn * rs * gkn[:, :MLA_NOPE], kr * rs], axis=-1) + k_extra
        km_ref[0, hd] = k_out.astype(BF16)
        vm_ref[0, hd] = kv[:, hd * MLA_QK_PAD + MLA_NOPE:(hd + 1) * MLA_QK_PAD].astype(BF16)

    sw = zp_old_ref.at[:, LAT_COLS:LAT_COLS + SWA_COLS]
    cs = cs_ref[...]
    ss = ss_ref[...]

    def norm_rope_swa(t, g):
        n = t * _rms_scale(t, SWA_HEAD_DIM) * g
        return n * cs + pltpu.roll(n, SWA_HEAD_DIM // 2, 1) * ss

    s_scale = (SWA_HEAD_DIM ** -0.5) * LOG2E
    for hd in range(SWA_HEADS):
        t = sw[:, hd * SWA_HEAD_DIM:(hd + 1) * SWA_HEAD_DIM]
        qs_ref[0, hd] = (norm_rope_swa(t, sgq_ref[...]) * s_scale).astype(BF16)
    for hd in range(SWA_KV_HEADS):
        t = sw[:, SWA_WIDTH + hd * SWA_HEAD_DIM:SWA_WIDTH + (hd + 1) * SWA_HEAD_DIM]
        ks_ref[0, hd] = norm_rope_swa(t, sgk_ref[...]).astype(BF16)
        vs_ref[0, hd] = sw[:, SWA_WIDTH + SWA_KV_WIDTH + hd * SWA_HEAD_DIM:
                           SWA_WIDTH + SWA_KV_WIDTH + (hd + 1) * SWA_HEAD_DIM].astype(BF16)


def _h_tile_spec(h, tm, tile_of=lambda b, i: (b, i)):
    if h.ndim == 3:
        return pl.BlockSpec((None, tm, D_MODEL), lambda *g: (*tile_of(*g), 0))
    return pl.BlockSpec((tm, D_MODEL), lambda *g: tile_of(*g)[::-1])


def _even_in(h2d, gn, w_all, gql, gkvl, wuq, wukv, gqn, gkn, sgq, sgk, cm, sm, cs, ss, *, batch, tp, tm):
    nt = tp // tm
    n_tiles = batch * nt
    new_tile = lambda n: divmod(jnp.minimum(n, n_tiles - 1), nt)
    old_tile = lambda n: divmod(jnp.maximum(n - 1, 0), nt)
    row_spec = lambda cols: pl.BlockSpec((tm, cols), lambda n: (old_tile(n)[1], 0))
    head_out = lambda heads, cols: pl.BlockSpec(
        (1, heads, tm, cols), lambda n: (old_tile(n)[0], 0, old_tile(n)[1], 0))
    n_w = w_all.shape[1]
    est = (w_all.size * 2 + (wuq.size + wukv.size) * 2 + 2 * tm * D_MODEL * 4
           + 2 * tm * 2 * (2 * MLA_HEADS * MLA_QK_PAD + MLA_WIDTH + SWA_WIDTH + 2 * SWA_KV_WIDTH)
           + 2 * tm * D_INNER * 2 + 2 * tm * (LAT_COLS + SWA_COLS) * 4 + 3 * tm * n_w * 4)
    return pl.pallas_call(
        _even_in_kernel,
        name="even_in",
        grid=(n_tiles + 1,),
        in_specs=[
            _h_tile_spec(h2d, tm, new_tile),
            _const_spec((1, D_MODEL)),
            _const_spec(w_all.shape),
            _const_spec((1, MLA_Q_RANK)),
            _const_spec((1, MLA_KV_RANK)),
            _const_spec(wuq.shape),
            _const_spec(wukv.shape),
            _const_spec((2, MLA_QK_PAD)),
            _const_spec((2, MLA_QK_PAD)),
            _const_spec((1, SWA_HEAD_DIM)),
            _const_spec((1, SWA_HEAD_DIM)),
            row_spec(LANE), row_spec(LANE), row_spec(LANE), row_spec(LANE),
        ],
        out_specs=[
            head_out(MLA_HEADS, MLA_QK_PAD),
            head_out(MLA_HEADS, MLA_QK_PAD),
            head_out(MLA_HEADS, MLA_V),
            head_out(SWA_HEADS, SWA_HEAD_DIM),
            head_out(SWA_KV_HEADS, SWA_HEAD_DIM),
            head_out(SWA_KV_HEADS, SWA_HEAD_DIM),
            pl.BlockSpec((1, tm, D_INNER), lambda n: (*new_tile(n), 0)),
        ],
        out_shape=[
            jax.ShapeDtypeStruct((batch, MLA_HEADS, tp, MLA_QK_PAD), BF16),
            jax.ShapeDtypeStruct((batch, MLA_HEADS, tp, MLA_QK_PAD), BF16),
            jax.ShapeDtypeStruct((batch, MLA_HEADS, tp, MLA_V), BF16),
            jax.ShapeDtypeStruct((batch, SWA_HEADS, tp, SWA_HEAD_DIM), BF16),
            jax.ShapeDtypeStruct((batch, SWA_KV_HEADS, tp, SWA_HEAD_DIM), BF16),
            jax.ShapeDtypeStruct((batch, SWA_KV_HEADS, tp, SWA_HEAD_DIM), BF16),
            jax.ShapeDtypeStruct((batch, tp, D_INNER), BF16),
        ],
        scratch_shapes=[pltpu.VMEM((tm, LAT_COLS + SWA_COLS), F32),
                        pltpu.VMEM((tm, LAT_COLS + SWA_COLS), F32)],
        compiler_params=pltpu.CompilerParams(
            dimension_semantics=("arbitrary",), vmem_limit_bytes=_vmem_limit(est)),
    )(h2d, gn, w_all, gql, gkvl, wuq, wukv, gqn, gkn, sgq, sgk, cm, sm, cs, ss)


def _mla_attn_kernel(fixed_ref, q_ref, k_ref, v_ref, gate_ref, o_ref, vx_ref, *, tp, tkc):
    @pl.when(pl.program_id(2) == 0)
    def _():
        lane = lax.broadcasted_iota(jnp.int32, (tp, MXU_DIM - MLA_V), 1)
        vx_ref[:, 0:MLA_V] = v_ref[0, 0]
        vx_ref[:, MLA_V:] = jnp.where(lane == 0, 1.0, 0.0).astype(BF16)

    @pl.when(fixed_ref[0] == 1)
    def _():
        _mla_fixed_shift(q_ref, k_ref, gate_ref, o_ref, vx_ref, tp=tp, tkc=tkc)

    @pl.when(fixed_ref[0] != 1)
    def _():
        _mla_online(q_ref, k_ref, gate_ref, o_ref, vx_ref, tp=tp, tkc=tkc)


def _mla_finish(acc, gate_ref, o_ref):
    o = acc[:, 0:MLA_V] / acc[:, MLA_V:MLA_V + 1]
    o_ref[0] = (o * _silu(gate_ref[0].astype(F32))).astype(BF16)


def _mla_fixed_shift(q_ref, k_ref, gate_ref, o_ref, vx_ref, *, tp, tkc):
    q = q_ref[0, 0]
    tq = q.shape[0]
    n_chunks = (tp - BLOCK) // tkc

    def scores(c):
        return _nt_dot(q, k_ref[0, 0, BLOCK + c * tkc:BLOCK + (c + 1) * tkc, :])

    def pv(c, p, acc):
        return acc + jnp.dot(p, vx_ref[BLOCK + c * tkc:BLOCK + (c + 1) * tkc, :], preferred_element_type=F32)

    col = lax.broadcasted_iota(jnp.int32, (tq, BLOCK), 1)
    s = jnp.where(col >= PAD_FRONT, _nt_dot(q, k_ref[0, 0, 0:BLOCK, :]), NEG_INF)
    s_next = scores(0)
    acc = jnp.dot(jnp.exp2(s).astype(BF16), vx_ref[0:BLOCK, :], preferred_element_type=F32)
    p_prev = None
    for c in range(n_chunks):
        s = s_next
        if c + 1 < n_chunks:
            s_next = scores(c + 1)
        p = jnp.exp2(s).astype(BF16)
        if p_prev is not None:
            acc = pv(c - 1, p_prev, acc)
        p_prev = p
    acc = pv(n_chunks - 1, p_prev, acc)
    _mla_finish(acc, gate_ref, o_ref)


def _mla_online(q_ref, k_ref, gate_ref, o_ref, vx_ref, *, tp, tkc):
    q = q_ref[0, 0]
    tq = q.shape[0]

    s = _nt_dot(q, k_ref[0, 0, 0:BLOCK, :])
    col = lax.broadcasted_iota(jnp.int32, (tq, BLOCK), 1)
    s = jnp.where(col >= PAD_FRONT, s, NEG_INF)
    m = jnp.max(s, axis=-1, keepdims=True)
    p = jnp.exp2(s - m)
    acc = jnp.dot(p.astype(BF16), vx_ref[0:BLOCK, :], preferred_element_type=F32)

    n_chunks = (tp - BLOCK) // tkc

    def scores(c):
        start = pl.multiple_of(BLOCK + c * tkc, BLOCK)
        return _nt_dot(q, k_ref[0, 0, pl.ds(start, tkc), :])

    def softmax_step(s, m):
        m_new = jnp.maximum(m, jnp.max(s, axis=-1, keepdims=True))
        alpha = jnp.exp2(m - m_new)
        p = jnp.exp2(s - m_new)
        return p.astype(BF16), alpha, m_new

    def pv_step(c, p, alpha, acc):
        start = pl.multiple_of(BLOCK + c * tkc, BLOCK)
        return alpha * acc + jnp.dot(p, vx_ref[pl.ds(start, tkc), :], preferred_element_type=F32)

    def chunk(c, carry):
        s, p_prev, alpha_prev, m, acc = carry
        s_next = scores(c + 1)
        p, alpha, m = softmax_step(s, m)
        acc = pv_step(c - 1, p_prev, alpha_prev, acc)
        return s_next, p, alpha, m, acc

    s1 = scores(1)
    p0, alpha0, m = softmax_step(scores(0), m)
    carry = (s1, p0, alpha0, m, acc)
    for c in range(1, n_chunks - 1):
        carry = chunk(c, carry)
    s, p_prev, alpha_prev, m, acc = carry
    p, alpha, m = softmax_step(s, m)
    acc = pv_step(n_chunks - 2, p_prev, alpha_prev, acc)
    acc = pv_step(n_chunks - 1, p, alpha, acc)
    _mla_finish(acc, gate_ref, o_ref)


def _mla_attn(fixed_flag, qm, km, vm, gate, *, batch, tp, tq, tkc):
    nq = tp // tq
    est = (2 * tp * (MLA_QK_PAD + MLA_V) * 2 + tp * MXU_DIM * 2 + 6 * tq * tkc * 4
           + 4 * tq * MLA_QK_PAD * 2)
    return pl.pallas_call(
        functools.partial(_mla_attn_kernel, tp=tp, tkc=tkc),
        name="mla_attn",
        grid=(batch, MLA_HEADS, nq),
        in_specs=[
            pl.BlockSpec(memory_space=pltpu.SMEM),
            pl.BlockSpec((1, 1, tq, MLA_QK_PAD), lambda b, h, i: (b, h, i, 0)),
            pl.BlockSpec((1, 1, tp, MLA_QK_PAD), lambda b, h, i: (b, h, 0, 0)),
            pl.BlockSpec((1, 1, tp, MLA_V), lambda b, h, i: (b, h, 0, 0)),
            pl.BlockSpec((1, tq, MLA_V), lambda b, h, i: (b, i, h)),
        ],
        out_specs=pl.BlockSpec((1, tq, MLA_V), lambda b, h, i: (b, i, h)),
        out_shape=jax.ShapeDtypeStruct((batch, tp, MLA_WIDTH), BF16),
        scratch_shapes=[pltpu.VMEM((tp, MXU_DIM), BF16)],
        compiler_params=pltpu.CompilerParams(
            dimension_semantics=("arbitrary", "arbitrary", "arbitrary"), vmem_limit_bytes=_vmem_limit(est)),
    )(fixed_flag, qm, km, vm, gate)


def _swa_masks(tp):
    band = 3 * BLOCK
    nb = tp // BLOCK
    r = (jnp.arange(SWA_REP * BLOCK, dtype=jnp.int32) % BLOCK)[:, None]
    c = jnp.arange(band, dtype=jnp.int32)[None, :]

    def mask(n):
        start = min(max((n - 1) * BLOCK, 0), tp - band)
        rq, rk = n * BLOCK + r, start + c
        return jnp.where((jnp.abs(rq - rk) <= SWA_WINDOW) & (rk >= BLOCK), 0.0, NEG_INF).astype(F32)

    band_bias = jnp.stack([mask(2), mask(0), mask(1), mask(nb - 1)])
    meta_bias = jnp.where(jnp.arange(BLOCK) >= PAD_FRONT, 0.0, NEG_INF).astype(F32)
    return band_bias, jnp.broadcast_to(meta_bias[None, :], (SWA_REP * BLOCK, BLOCK))


def _swa_attn_kernel(fixed_ref, q_ref, k_ref, v_ref, sink_ref, gate_ref, bias_ref, mbias_ref, o_ref, vx_ref,
                     *, tp, nblk):
    step = pl.program_id(2)

    @pl.when(step == 0)
    def _():
        lane = lax.broadcasted_iota(jnp.int32, (tp, MXU_DIM - SWA_HEAD_DIM), 1)
        vx_ref[:, 0:SWA_HEAD_DIM] = v_ref[0, 0]
        vx_ref[:, SWA_HEAD_DIM:] = jnp.where(lane == 0, 1.0, 0.0).astype(BF16)

    @pl.when(fixed_ref[0] == 1)
    def _():
        _swa_blocks(q_ref, k_ref, sink_ref, gate_ref, bias_ref, mbias_ref, o_ref, vx_ref,
                    tp=tp, nblk=nblk, row_max=False)

    @pl.when(fixed_ref[0] != 1)
    def _():
        _swa_blocks(q_ref, k_ref, sink_ref, gate_ref, bias_ref, mbias_ref, o_ref, vx_ref,
                    tp=tp, nblk=nblk, row_max=True)


def _swa_blocks(q_ref, k_ref, sink_ref, gate_ref, bias_ref, mbias_ref, o_ref, vx_ref, *, tp, nblk, row_max):
    step = pl.program_id(2)
    rows = SWA_REP * BLOCK
    band = 3 * BLOCK
    sink = sink_ref[0]
    g = _silu(gate_ref[0].astype(F32))
    starts, scores = [], []
    for u in range(nblk):
        n = step * nblk + u
        q = q_ref[0, :, u * BLOCK:(u + 1) * BLOCK, :].reshape(rows, SWA_HEAD_DIM)
        start = pl.multiple_of(jnp.clip((n - 1) * BLOCK, 0, tp - band), BLOCK)
        kind = jnp.where(n == 0, 1, jnp.where(n == 1, 2, jnp.where(n == tp // BLOCK - 1, 3, 0)))
        s_b = _nt_dot(q, k_ref[0, 0, pl.ds(start, band), :]) + bias_ref[kind]
        s_m = _nt_dot(q, k_ref[0, 0, 0:BLOCK, :]) + mbias_ref[...]
        starts.append(start)
        scores.append((s_b, s_m))

    for u in range(nblk):
        s_b, s_m = scores[u]
        sink_u = sink
        if row_max:
            mx = jnp.maximum(jnp.maximum(s_b[:, 0:BLOCK], s_b[:, BLOCK:2 * BLOCK]),
                             jnp.maximum(s_b[:, 2 * BLOCK:], s_m))
            m = jnp.maximum(jnp.max(mx, axis=-1, keepdims=True), sink)
            s_b, s_m, sink_u = s_b - m, s_m - m, sink - m
        p_b = jnp.exp2(s_b).astype(BF16)
        p_m = jnp.exp2(s_m).astype(BF16)
        acc = (jnp.dot(p_b, vx_ref[pl.ds(starts[u], band), :], preferred_element_type=F32)
               + jnp.dot(p_m, vx_ref[0:BLOCK, :], preferred_element_type=F32))
        l = jnp.exp2(sink_u) + acc[:, SWA_HEAD_DIM:SWA_HEAD_DIM + 1]
        o = acc[:, 0:SWA_HEAD_DIM] / l
        for r in range(SWA_REP):
            o_ref[0, u * BLOCK:(u + 1) * BLOCK, r * SWA_HEAD_DIM:(r + 1) * SWA_HEAD_DIM] = (
                o[r * BLOCK:(r + 1) * BLOCK]
                * g[u * BLOCK:(u + 1) * BLOCK, r * SWA_HEAD_DIM:(r + 1) * SWA_HEAD_DIM]).astype(BF16)


def _swa_attn(qs, ks, vs, sink, g_qn, g_kn, gate, *, batch, tp):
    nb = tp // BLOCK
    assert nb >= 4, "the four band-mask kinds assume at least four token blocks"
    nblk = _tile(nb, (11, 3, 1))
    rows_q = nblk * BLOCK
    sink2 = sink.astype(F32) * LOG2E
    score_bound = 1.01 * LOG2E * math.sqrt(SWA_HEAD_DIM) * jnp.max(jnp.abs(g_qn)) * jnp.max(jnp.abs(g_kn))
    shift = jnp.maximum(score_bound, jnp.max(sink2))
    fixed_flag = (score_bound + shift <= 2.0 * MLA_FIXED_SHIFT_MAX).astype(jnp.int32).reshape(1)
    band_bias, meta_bias = _swa_masks(tp)
    band_bias, meta_bias = band_bias - shift, meta_bias - shift
    sink_rows = jnp.repeat(sink2 - shift, BLOCK).reshape(SWA_KV_HEADS, SWA_REP * BLOCK, 1)
    gcols = SWA_REP * SWA_HEAD_DIM
    est = 4 * tp * SWA_HEAD_DIM * 2 + tp * MXU_DIM * 2 + nblk * 12 * SWA_REP * BLOCK * 4 * BLOCK * 4
    return pl.pallas_call(
        functools.partial(_swa_attn_kernel, tp=tp, nblk=nblk),
        name="swa_attn",
        grid=(batch, SWA_KV_HEADS, nb // nblk),
        in_specs=[
            pl.BlockSpec(memory_space=pltpu.SMEM),
            pl.BlockSpec((1, SWA_REP, rows_q, SWA_HEAD_DIM), lambda b, g, n: (b, g, n, 0)),
            pl.BlockSpec((1, 1, tp, SWA_HEAD_DIM), lambda b, g, n: (b, g, 0, 0)),
            pl.BlockSpec((1, 1, tp, SWA_HEAD_DIM), lambda b, g, n: (b, g, 0, 0)),
            pl.BlockSpec((1, SWA_REP * BLOCK, 1), lambda b, g, n: (g, 0, 0)),
            pl.BlockSpec((1, rows_q, gcols), lambda b, g, n: (b, n, MLA_WIDTH // gcols + g)),
            _const_spec(band_bias.shape),
            _const_spec(meta_bias.shape),
        ],
        out_specs=pl.BlockSpec((1, rows_q, gcols), lambda b, g, n: (b, n, g)),
        out_shape=jax.ShapeDtypeStruct((batch, tp, SWA_WIDTH), BF16),
        scratch_shapes=[pltpu.VMEM((tp, MXU_DIM), BF16)],
        compiler_params=pltpu.CompilerParams(
            dimension_semantics=("arbitrary", "arbitrary", "arbitrary"), vmem_limit_bytes=_vmem_limit(est)),
    )(fixed_flag, qs, ks, vs, sink_rows, gate, band_bias, meta_bias)


def _even_out_kernel(ya_ref, yb_ref, w_ref, h_ref, o_ref, *, tm):
    i = pl.program_id(1)
    d = (jnp.dot(ya_ref[0], w_ref[0:MLA_WIDTH, :], preferred_element_type=F32)
         + jnp.dot(yb_ref[0], w_ref[MLA_WIDTH:, :], preferred_element_type=F32))
    row = i * tm + lax.broadcasted_iota(jnp.int32, (tm, D_MODEL), 0)
    o_ref[...] = h_ref[...] + jnp.where(row >= PAD_FRONT, d, 0.0)


def _even_out(ya, yb, w_out, h2d, *, batch, tp, tm):
    nt = tp // tm
    est = w_out.size * 2 + 4 * tm * D_MODEL * 2 + 6 * tm * D_MODEL * 4
    return pl.pallas_call(
        functools.partial(_even_out_kernel, tm=tm),
        name="even_out",
        grid=(batch, nt),
        in_specs=[
            pl.BlockSpec((1, tm, MLA_WIDTH), lambda b, i: (b, i, 0)),
            pl.BlockSpec((1, tm, SWA_WIDTH), lambda b, i: (b, i, 0)),
            _const_spec(w_out.shape),
            _h_tile_spec(h2d, tm),
        ],
        out_specs=pl.BlockSpec((tm, D_MODEL), lambda b, i: (i, b)),
        out_shape=jax.ShapeDtypeStruct((tp, batch * D_MODEL), F32),
        compiler_params=pltpu.CompilerParams(
            dimension_semantics=("arbitrary", "arbitrary"), vmem_limit_bytes=_vmem_limit(est)),
    )(ya, yb, w_out, h2d)


def _lru_coeffs(xc, xcb, wg_ref, bg_ref, lam_ref, a_ref, b_ref, t0=None):
    tmr = xc.shape[0]
    if t0 is not None:
        t_idx = t0 + (lax.broadcasted_iota(jnp.int32, (tmr, LRU_BLOCK_DIM), 0) // SUBLANE)
        live = t_idx >= PAD_FRONT
    for n in range(LRU_BLOCKS):
        sl = slice(n * LRU_BLOCK_DIM, (n + 1) * LRU_BLOCK_DIM)
        xn = xc[:, sl]
        gh = jnp.dot(xcb[:, sl], wg_ref[n], preferred_element_type=F32) + bg_ref[n]
        tr = jnp.tanh(gh[:, :LRU_BLOCK_DIM])
        ti = jnp.tanh(gh[:, LRU_BLOCK_DIM:])
        ch = (-0.5 * LRU_C) * jax.nn.softplus(-lam_ref[:, sl])
        log_a = tr * ch + ch
        a = jnp.exp(log_a)
        a_ref[:, sl] = a
        one_m_a2 = jnp.tanh(log_a) * (-1.0 - a * a)
        root = one_m_a2 * lax.rsqrt(jnp.maximum(one_m_a2, F32_MIN_NORMAL))
        b = root * (0.5 * ti + 0.5) * xn
        b_ref[:, sl] = b if t0 is None else jnp.where(live, b, 0.0)


def _odd_fwd_kernel(h_ref, halo_ref, gn_ref, w_ref, cw_ref, cb_ref, wg_ref, bg_ref, lam_ref,
                    xc_ref, gate_ref, hf_ref,
                    uprev_ref, hst_ref, a_ref, b_ref, *, tc, nt):
    j = pl.program_id(0)
    tmr = tc * SUBLANE
    halo_rows = SUBLANE
    prev_rows = 2 * SUBLANE

    @pl.when(j == 0)
    def _():
        uprev_ref[...] = jnp.zeros_like(uprev_ref)
        hst_ref[...] = jnp.zeros_like(hst_ref)

    x = jnp.concatenate([h_ref[...].reshape(tmr, D_MODEL), halo_ref[...].reshape(halo_rows, D_MODEL)], axis=0)
    z = (x * _rms_scale(x, D_MODEL) * gn_ref[...]).astype(BF16)
    u = jnp.dot(z, w_ref[:, 0:LRU_WIDTH], preferred_element_type=F32)
    gate_ref[...] = jnp.dot(z[0:tmr], w_ref[:, LRU_WIDTH:], preferred_element_type=F32).astype(BF16)

    u_next = jnp.where(j == nt - 1, 0.0, u[tmr:])
    ue = jnp.concatenate([uprev_ref[...], u[0:tmr], u_next], axis=0)
    uprev_ref[...] = u[tmr - prev_rows:tmr]
    xc = cb_ref[...] + ue[0:tmr] * cw_ref[0:1, :]
    for tap in range(1, CONV_WIDTH):
        xc = xc + ue[tap * SUBLANE:tap * SUBLANE + tmr] * cw_ref[tap:tap + 1, :]
    xcb = xc.astype(BF16)
    xc_ref[...] = xcb

    _lru_coeffs(xc, xcb, wg_ref, bg_ref, lam_ref, a_ref, b_ref, j * tc)

    def step(t, hs):
        r = pl.multiple_of(t * SUBLANE, SUBLANE)
        hs = a_ref[pl.ds(r, SUBLANE), :] * hs + b_ref[pl.ds(r, SUBLANE), :]
        b_ref[pl.ds(r, SUBLANE), :] = hs
        return hs

    hst_ref[...] = lax.fori_loop(0, tc, step, hst_ref[...], unroll=8)
    hf_ref[...] = b_ref[...].astype(BF16)


def _odd_fwd(h, gn, w_in, cw, cb, wg, bg, lam, *, batch, tp, tc):
    nt = tp // tc
    tmr = tc * batch
    blk = lambda cols: pl.BlockSpec((tc, batch, cols), lambda j: (j, 0, 0))
    row_blk = pl.BlockSpec((tmr, LRU_WIDTH), lambda j: (j, 0))
    est = (w_in.size * 2 + 2 * tmr * D_MODEL * 4 + 3 * 2 * tmr * LRU_WIDTH * 2
           + 2 * tmr * LRU_WIDTH * 4 + 5 * tmr * LRU_WIDTH * 4)
    return pl.pallas_call(
        functools.partial(_odd_fwd_kernel, tc=tc, nt=nt),
        name="odd_fwd",
        grid=(nt,),
        in_specs=[
            blk(D_MODEL),
            pl.BlockSpec((1, batch, D_MODEL), lambda j: (jnp.minimum((j + 1) * tc, tp - 1), 0, 0)),
            _const_spec((1, D_MODEL)),
            _const_spec(w_in.shape),
            _const_spec(cw.shape),
            _const_spec(cb.shape),
            _const_spec(wg.shape),
            _const_spec(bg.shape),
            _const_spec(lam.shape),
        ],
        out_specs=[row_blk, row_blk, row_blk],
        out_shape=[jax.ShapeDtypeStruct((tp * batch, LRU_WIDTH), BF16)] * 3,
        scratch_shapes=[pltpu.VMEM((2 * SUBLANE, LRU_WIDTH), F32),
                        pltpu.VMEM((SUBLANE, LRU_WIDTH), F32),
                        pltpu.VMEM((tmr, LRU_WIDTH), F32),
                        pltpu.VMEM((tmr, LRU_WIDTH), F32)],
        compiler_params=pltpu.CompilerParams(
            dimension_semantics=("arbitrary",), vmem_limit_bytes=_vmem_limit(est)),
    )(h, h, gn, w_in, cw, cb, wg, bg, lam)


def _odd_bwd_kernel(xc_ref, gate_ref, hf_ref, h_ref, wg_ref, bg_ref, lam_ref, w_ref, o_ref,
                    hst_ref, a_ref, b_ref, st_ref, *, tc, nt, final):
    j = pl.program_id(0)
    jj = nt - 1 - j
    tmr = tc * SUBLANE

    @pl.when(j == 0)
    def _():
        hst_ref[...] = jnp.zeros_like(hst_ref)

    xcb = xc_ref[...]
    _lru_coeffs(xcb.astype(F32), xcb, wg_ref, bg_ref, lam_ref, a_ref, b_ref)

    def step(s, hs):
        t = tc - 1 - s
        r = pl.multiple_of(t * SUBLANE, SUBLANE)
        hs = a_ref[pl.ds(r, SUBLANE), :] * hs + b_ref[pl.ds(r, SUBLANE), :]
        b_ref[pl.ds(r, SUBLANE), :] = hs
        return hs

    hst_ref[...] = lax.fori_loop(0, tc, step, hst_ref[...], unroll=8)

    y = ((hf_ref[...].astype(F32) + b_ref[...]) * _silu(gate_ref[...].astype(F32))).astype(BF16)
    d = jnp.dot(y, w_ref[...], preferred_element_type=F32)
    if not final:
        t_idx = jj * tc + (lax.broadcasted_iota(jnp.int32, (tmr, D_MODEL), 0) // SUBLANE)
        o_ref[...] = h_ref[...] + jnp.where(t_idx >= PAD_FRONT, d, 0.0).reshape(tc, SUBLANE, D_MODEL)
    else:
        @pl.when(jj >= BLOCK // tc)
        def _():
            res = h_ref[...].reshape(tmr, D_MODEL) + d
            for c in range(D_MODEL // LANE):
                st_ref[c] = res[:, c * LANE:(c + 1) * LANE]
                for b in range(SUBLANE):
                    o_ref[b, :, c * LANE:(c + 1) * LANE] = st_ref[c, pl.ds(b, tc, stride=SUBLANE), :]


def _odd_bwd(xc, gate, hf, h, wg, bg, lam, w_out, *, batch, tp, tc, final):
    nt = tp // tc
    tmr = tc * batch
    blk = lambda cols: pl.BlockSpec((tc, batch, cols), lambda j: (nt - 1 - j, 0, 0))
    if final:
        first = BLOCK // tc
        out_spec = pl.BlockSpec((batch, tc, D_MODEL), lambda j: (0, jnp.maximum(nt - 1 - j - first, 0), 0))
        out_shape = jax.ShapeDtypeStruct((batch, tp - BLOCK, D_MODEL), F32)
    else:
        out_spec = blk(D_MODEL)
        out_shape = jax.ShapeDtypeStruct((tp, batch, D_MODEL), F32)
    row_blk = pl.BlockSpec((tmr, LRU_WIDTH), lambda j: (nt - 1 - j, 0))
    est = (w_out.size * 2 + 3 * 2 * tmr * LRU_WIDTH * 2 + 4 * tmr * D_MODEL * 4
           + 2 * tmr * LRU_WIDTH * 4 + 5 * tmr * LRU_WIDTH * 4)
    return pl.pallas_call(
        functools.partial(_odd_bwd_kernel, tc=tc, nt=nt, final=final),
        name="odd_bwd",
        grid=(nt,),
        in_specs=[
            row_blk, row_blk, row_blk, blk(D_MODEL),
            _const_spec(wg.shape),
            _const_spec(bg.shape),
            _const_spec(lam.shape),
            _const_spec(w_out.shape),
        ],
        out_specs=out_spec,
        out_shape=out_shape,
        scratch_shapes=[pltpu.VMEM((SUBLANE, LRU_WIDTH), F32),
                        pltpu.VMEM((tmr, LRU_WIDTH), F32),
                        pltpu.VMEM((tmr, LRU_WIDTH), F32),
                        pltpu.VMEM((D_MODEL // LANE, tmr, LANE), F32)],
        compiler_params=pltpu.CompilerParams(
            dimension_semantics=("arbitrary",), vmem_limit_bytes=_vmem_limit(est)),
    )(xc, gate, hf, h, wg, bg, lam, w_out)


def _rope_tables(tp):
    pos = (jnp.arange(tp, dtype=F32) - PAD_FRONT)[:, None]

    def table(dim):
        inv = ROPE_THETA ** (-jnp.arange(0, dim, 2, dtype=F32) / dim)
        ang = pos * inv[None, :]
        return jnp.cos(ang), jnp.sin(ang)

    c, s = table(MLA_ROPE)
    zeros = jnp.zeros((tp, LANE - MLA_ROPE), F32)
    cm = jnp.concatenate([c, c, zeros], axis=1)
    sm = jnp.concatenate([-s, s, zeros], axis=1)
    c, s = table(SWA_HEAD_DIM)
    cs = jnp.concatenate([c, c], axis=1)
    ss = jnp.concatenate([-s, s], axis=1)
    return cm, sm, cs, ss


def _prep_even(w_in, w_uq, w_ukv, g_qn, g_kn):
    lat_w = MLA_Q_RANK + MLA_KV_RANK + MLA_ROPE
    w_bf = w_in.astype(BF16)
    w_all = jnp.concatenate(
        [w_bf[:, :lat_w], jnp.zeros((D_MODEL, LAT_COLS - lat_w), BF16), w_bf[:, lat_w:]], axis=1)
    wuq = jnp.pad(w_uq, ((0, 0), (0, 0), (0, MLA_QK_PAD - MLA_QK))).reshape(
        MLA_Q_RANK, MLA_HEADS * MLA_QK_PAD).astype(BF16)
    wukv = w_ukv.reshape(MLA_KV_RANK, MLA_HEADS * (MLA_NOPE + MLA_V)).astype(BF16)
    shift = 1.01 * LOG2E * math.sqrt(MLA_QK) * jnp.max(jnp.abs(g_qn)) * jnp.max(jnp.abs(g_kn))
    lane = jnp.arange(MLA_QK_PAD) == MLA_QK
    pad_g = lambda g: jnp.pad(g, (0, MLA_QK_PAD - MLA_QK))
    gq2 = jnp.stack([pad_g(g_qn), jnp.where(lane, -shift, 0.0)]).astype(F32)
    gk2 = jnp.stack([pad_g(g_kn), jnp.where(lane, 1.0, 0.0)]).astype(F32)
    fixed_flag = (shift <= MLA_FIXED_SHIFT_MAX).astype(jnp.int32).reshape(1)
    return w_all, wuq, wukv, gq2, gk2, fixed_flag


def _prep_gates(w_a, b_a, w_x, b_x):
    wg = (0.5 * jnp.concatenate([w_a, w_x], axis=-1)).astype(BF16)
    bg = 0.5 * jnp.concatenate([b_a.reshape(LRU_BLOCKS, 1, LRU_BLOCK_DIM),
                                b_x.reshape(LRU_BLOCKS, 1, LRU_BLOCK_DIM)], axis=-1)
    return wg, bg


def _tile(tp, prefs):
    for t in prefs:
        if tp % t == 0:
            return t
    raise ValueError(f"no tile of {prefs} divides {tp}")


def kernel(x, meta_tokens, norm_g, even_w_in, mla_g_q_lat, mla_g_kv_lat, mla_w_uq, mla_w_ukv, mla_g_qn, mla_g_kn, swa_g_qn, swa_g_kn, swa_sink, even_w_out, odd_w_in, lru_conv_w, lru_conv_b, lru_w_a, lru_b_a, lru_w_x, lru_b_x, lru_lambda, odd_w_out):
    batch, seq, d = x.shape
    assert d == D_MODEL and batch == SUBLANE and seq % BLOCK == 0
    tp = BLOCK + seq
    tm = _tile(tp, (384, 128))
    tq = _tile(tp, (2112, 1056, 384, 128))
    tkc = _tile(seq, (512, 256, 128))
    assert seq // tkc >= 2, "the MLA key-chunk pipeline needs at least two chunks"
    tc = _tile(tp, (64, 32, 16))

    meta = jnp.broadcast_to(meta_tokens.astype(x.dtype)[None], (batch, N_META, D_MODEL))
    h = jnp.concatenate([jnp.zeros((batch, PAD_FRONT, D_MODEL), x.dtype), meta, x], axis=1)
    cm, sm, cs, ss = _rope_tables(tp)

    even_w_in_bf = even_w_in.astype(BF16)
    for l in range(DEPTH):
        j = l // 2
        gn = norm_g[l].reshape(1, D_MODEL)
        if l % 2 == 0:
            w_all, wuq, wukv, gqn, gkn, fixed_flag = _prep_even(even_w_in_bf[j], mla_w_uq[j], mla_w_ukv[j],
                                                                mla_g_qn[j], mla_g_kn[j])
            h2d = h if l == 0 else h.reshape(tp, batch * D_MODEL)
            qm, km, vm, qs, ks, vs, gate = _even_in(
                h2d, gn, w_all, mla_g_q_lat[j].reshape(1, -1), mla_g_kv_lat[j].reshape(1, -1), wuq, wukv,
                gqn, gkn, swa_g_qn[j].reshape(1, -1), swa_g_kn[j].reshape(1, -1), cm, sm, cs, ss,
                batch=batch, tp=tp, tm=tm)
            ya = _mla_attn(fixed_flag, qm, km, vm, gate, batch=batch, tp=tp, tq=tq, tkc=tkc)
            yb = _swa_attn(qs, ks, vs, swa_sink[j], swa_g_qn[j], swa_g_kn[j], gate, batch=batch, tp=tp)
            h2d = _even_out(ya, yb, even_w_out[j].astype(BF16), h2d, batch=batch, tp=tp, tm=tm)
            h = h2d.reshape(tp, batch, D_MODEL)
        else:
            wgf, bgf = _prep_gates(lru_w_a[j, 0], lru_b_a[j, 0], lru_w_x[j, 0], lru_b_x[j, 0])
            wgb, bgb = _prep_gates(lru_w_a[j, 1], lru_b_a[j, 1], lru_w_x[j, 1], lru_b_x[j, 1])
            xc, gate, hf = _odd_fwd(h, gn, odd_w_in[j].astype(BF16), lru_conv_w[j], lru_conv_b[j].reshape(1, -1),
                                    wgf, bgf, lru_lambda[j, 0].reshape(1, -1), batch=batch, tp=tp, tc=tc)
            h = _odd_bwd(xc, gate, hf, h, wgb, bgb, lru_lambda[j, 1].reshape(1, -1), odd_w_out[j].astype(BF16),
                         batch=batch, tp=tp, tc=tc, final=(l == DEPTH - 1))
    return h
```

```python
import functools
import math

import jax
import jax.numpy as jnp
from jax import lax
from jax.experimental import pallas as pl
from jax.experimental.pallas import tpu as pltpu

F32 = jnp.float32
BF16 = jnp.bfloat16

D_MODEL = 1024
DEPTH = 4
N_META = 16
D_INNER = 2 * D_MODEL
ROPE_THETA = 10000.0
EPS = 1e-6
NEG_INF = -1e30

MLA_HEADS = 8
MLA_Q_RANK = 384
MLA_KV_RANK = 256
MLA_NOPE = 128
MLA_ROPE = 64
MLA_QK = MLA_NOPE + MLA_ROPE
MLA_V = 128
MLA_WIDTH = MLA_HEADS * MLA_V

SWA_HEADS = 8
SWA_KV_HEADS = 2
SWA_REP = SWA_HEADS // SWA_KV_HEADS
SWA_HEAD_DIM = 128
SWA_WINDOW = 128
SWA_WIDTH = SWA_HEADS * SWA_HEAD_DIM
SWA_KV_WIDTH = SWA_KV_HEADS * SWA_HEAD_DIM

LRU_WIDTH = D_INNER
LRU_BLOCKS = 16
LRU_BLOCK_DIM = LRU_WIDTH // LRU_BLOCKS
LRU_C = 8.0
CONV_WIDTH = 4
ODD_CHUNK_BLOCKS = 2

LANE = 128
SUBLANE = 8
MXU_DIM = 256
V7X_VMEM_BYTES = 64 * 1024 * 1024

BLOCK = 128
PAD_FRONT = BLOCK - N_META
MLA_QK_PAD = MXU_DIM
LAT_COLS = 768
SWA_COLS = SWA_WIDTH + 2 * SWA_KV_WIDTH
LOG2E = 1.4426950408889634
F32_MIN_NORMAL = 1.1754943508222875e-38
MLA_FIXED_SHIFT_MAX = 40.0


def _vmem_limit(nbytes):
    return int(min(V7X_VMEM_BYTES - (4 << 20), max(nbytes, 16 << 20)))


def _const_spec(shape):
    nd = len(shape)
    return pl.BlockSpec(shape, lambda *_: (0,) * nd, pipeline_mode=pl.Buffered(1))


def _rms_scale(x, n):
    return lax.rsqrt(jnp.sum(x * x, axis=-1, keepdims=True) * (1.0 / n) + EPS)


def _silu(g):
    gh = 0.5 * g
    return gh * (1.0 + jnp.tanh(gh))


def _nt_dot(a, b):
    return lax.dot_general(a, b, (((1,), (1,)), ((), ())), preferred_element_type=F32)


def _even_in_kernel(h_ref, gn_ref, w_ref, gql_ref, gkvl_ref, wuq_ref, wukv_ref, gqn_ref, gkn_ref,
                    sgq_ref, sgk_ref, cm_ref, sm_ref, cs_ref, ss_ref,
                    qm_ref, km_ref, vm_ref, qs_ref, ks_ref, vs_ref, gate_ref, zp_a_ref, zp_b_ref):
    n = pl.program_id(0)

    @pl.when(n == 0)
    def _():
        zp_b_ref[...] = jnp.zeros_like(zp_b_ref)

    @pl.when(n % 2 == 0)
    def _():
        _even_in_step(h_ref, gn_ref, w_ref, gql_ref, gkvl_ref, wuq_ref, wukv_ref, gqn_ref, gkn_ref,
                      sgq_ref, sgk_ref, cm_ref, sm_ref, cs_ref, ss_ref,
                      qm_ref, km_ref, vm_ref, qs_ref, ks_ref, vs_ref, gate_ref, zp_a_ref, zp_b_ref)

    @pl.when(n % 2 == 1)
    def _():
        _even_in_step(h_ref, gn_ref, w_ref, gql_ref, gkvl_ref, wuq_ref, wukv_ref, gqn_ref, gkn_ref,
                      sgq_ref, sgk_ref, cm_ref, sm_ref, cs_ref, ss_ref,
                      qm_ref, km_ref, vm_ref, qs_ref, ks_ref, vs_ref, gate_ref, zp_b_ref, zp_a_ref)


def _even_in_step(h_ref, gn_ref, w_ref, gql_ref, gkvl_ref, wuq_ref, wukv_ref, gqn_ref, gkn_ref,
                  sgq_ref, sgk_ref, cm_ref, sm_ref, cs_ref, ss_ref,
                  qm_ref, km_ref, vm_ref, qs_ref, ks_ref, vs_ref, gate_ref, zp_new_ref, zp_old_ref):
    cq = zp_old_ref[:, 0:MLA_Q_RANK]
    ckv = zp_old_ref[:, MLA_Q_RANK:MLA_Q_RANK + MLA_KV_RANK]
    kpe = zp_old_ref[:, MLA_Q_RANK + MLA_KV_RANK:LAT_COLS]
    cqn = (cq * _rms_scale(cq, MLA_Q_RANK) * gql_ref[...]).astype(BF16)
    ckvn = (ckv * _rms_scale(ckv, MLA_KV_RANK) * gkvl_ref[...]).astype(BF16)
    q_all = jnp.dot(cqn, wuq_ref[...], preferred_element_type=F32)
    kv = jnp.dot(ckvn, wukv_ref[...], preferred_element_type=F32)

    x = h_ref[...]
    z = (x * _rms_scale(x, D_MODEL) * gn_ref[...]).astype(BF16)
    zp_new_ref[...] = jnp.dot(z, w_ref[:, 0:LAT_COLS + SWA_COLS], preferred_element_type=F32)
    gate_ref[0] = jnp.dot(z, w_ref[:, LAT_COLS + SWA_COLS:], preferred_element_type=F32).astype(BF16)


    cm = cm_ref[...]
    sm = sm_ref[...]

    def rope_mla(t):
        rot = pltpu.roll(t, MLA_ROPE // 2, 1) + pltpu.roll(t, LANE - MLA_ROPE // 2, 1)
        return t * cm + rot * sm

    q_scale = (MLA_QK ** -0.5) * LOG2E
    gqn = gqn_ref[0:1, :]
    q_extra = gqn_ref[1:2, :]
    for hd in range(MLA_HEADS):
        qh = q_all[:, hd * MLA_QK_PAD:(hd + 1) * MLA_QK_PAD]
        qn = qh * _rms_scale(qh, MLA_QK) * gqn
        q_out = jnp.concatenate([qn[:, :MLA_NOPE], rope_mla(qn[:, MLA_NOPE:])], axis=-1) * q_scale + q_extra
        qm_ref[0, hd] = q_out.astype(BF16)

    gkn = gkn_ref[0:1, :]
    k_extra = gkn_ref[1:2, :]
    ss_pe = jnp.sum(kpe * kpe, axis=-1, keepdims=True)
    kr = rope_mla(kpe * gkn[:, MLA_NOPE:])
    for hd in range(MLA_HEADS):
        kn = kv[:, hd * MLA_QK_PAD:hd * MLA_QK_PAD + MLA_NOPE]
        rs = lax.rsqrt((jnp.sum(kn * kn, axis=-1, keepdims=True) + ss_pe) * (1.0 / MLA_QK) + EPS)
        k_out = jnp.concatenate([kn * rs * gkn[:, :MLA_NOPE], kr * rs], axis=-1) + k_extra
        km_ref[0, hd] = k_out.astype(BF16)
        vm_ref[0, hd] = kv[:, hd * MLA_QK_PAD + MLA_NOPE:(hd + 1) * MLA_QK_PAD].astype(BF16)

    sw = zp_old_ref.at[:, LAT_COLS:LAT_COLS + SWA_COLS]
    cs = cs_ref[...]
    ss = ss_ref[...]

    def norm_rope_swa(t, g):
        n = t * _rms_scale(t, SWA_HEAD_DIM) * g
        return n * cs + pltpu.roll(n, SWA_HEAD_DIM // 2, 1) * ss

    s_scale = (SWA_HEAD_DIM ** -0.5) * LOG2E
    for hd in range(SWA_HEADS):
        t = sw[:, hd * SWA_HEAD_DIM:(hd + 1) * SWA_HEAD_DIM]
        qs_ref[0, hd] = (norm_rope_swa(t, sgq_ref[...]) * s_scale).astype(BF16)
    for hd in range(SWA_KV_HEADS):
        t = sw[:, SWA_WIDTH + hd * SWA_HEAD_DIM:SWA_WIDTH + (hd + 1) * SWA_HEAD_DIM]
        ks_ref[0, hd] = norm_rope_swa(t, sgk_ref[...]).astype(BF16)
        vs_ref[0, hd] = sw[:, SWA_WIDTH + SWA_KV_WIDTH + hd * SWA_HEAD_DIM:
                           SWA_WIDTH + SWA_KV_WIDTH + (hd + 1) * SWA_HEAD_DIM].astype(BF16)


def _h_tile_spec(h, tm, tile_of=lambda b, i: (b, i)):
    if h.ndim == 3:
        return pl.BlockSpec((None, tm, D_MODEL), lambda *g: (*tile_of(*g), 0))
    return pl.BlockSpec((tm, D_MODEL), lambda *g: tile_of(*g)[::-1])


def _even_in(h2d, gn, w_all, gql, gkvl, wuq, wukv, gqn, gkn, sgq, sgk, cm, sm, cs, ss, *, batch, tp, tm):
    nt = tp // tm
    n_tiles = batch * nt
    new_tile = lambda n: divmod(jnp.minimum(n, n_tiles - 1), nt)
    old_tile = lambda n: divmod(jnp.maximum(n - 1, 0), nt)
    row_spec = lambda cols: pl.BlockSpec((tm, cols), lambda n: (old_tile(n)[1], 0))
    head_out = lambda heads, cols: pl.BlockSpec(
        (1, heads, tm, cols), lambda n: (old_tile(n)[0], 0, old_tile(n)[1], 0))
    n_w = w_all.shape[1]
    est = (w_all.size * 2 + (wuq.size + wukv.size) * 2 + 2 * tm * D_MODEL * 4
           + 2 * tm * 2 * (2 * MLA_HEADS * MLA_QK_PAD + MLA_WIDTH + SWA_WIDTH + 2 * SWA_KV_WIDTH)
           + 2 * tm * D_INNER * 2 + 2 * tm * (LAT_COLS + SWA_COLS) * 4 + 3 * tm * n_w * 4)
    return pl.pallas_call(
        _even_in_kernel,
        name="even_in",
        grid=(n_tiles + 1,),
        in_specs=[
            _h_tile_spec(h2d, tm, new_tile),
            _const_spec((1, D_MODEL)),
            _const_spec(w_all.shape),
            _const_spec((1, MLA_Q_RANK)),
            _const_spec((1, MLA_KV_RANK)),
            _const_spec(wuq.shape),
            _const_spec(wukv.shape),
            _const_spec((2, MLA_QK_PAD)),
            _const_spec((2, MLA_QK_PAD)),
            _const_spec((1, SWA_HEAD_DIM)),
            _const_spec((1, SWA_HEAD_DIM)),
            row_spec(LANE), row_spec(LANE), row_spec(LANE), row_spec(LANE),
        ],
        out_specs=[
            head_out(MLA_HEADS, MLA_QK_PAD),
            head_out(MLA_HEADS, MLA_QK_PAD),
            head_out(MLA_HEADS, MLA_V),
            head_out(SWA_HEADS, SWA_HEAD_DIM),
            head_out(SWA_KV_HEADS, SWA_HEAD_DIM),
            head_out(SWA_KV_HEADS, SWA_HEAD_DIM),
            pl.BlockSpec((1, tm, D_INNER), lambda n: (*new_tile(n), 0)),
        ],
        out_shape=[
            jax.ShapeDtypeStruct((batch, MLA_HEADS, tp, MLA_QK_PAD), BF16),
            jax.ShapeDtypeStruct((batch, MLA_HEADS, tp, MLA_QK_PAD), BF16),
            jax.ShapeDtypeStruct((batch, MLA_HEADS, tp, MLA_V), BF16),
            jax.ShapeDtypeStruct((batch, SWA_HEADS, tp, SWA_HEAD_DIM), BF16),
            jax.ShapeDtypeStruct((batch, SWA_KV_HEADS, tp, SWA_HEAD_DIM), BF16),
            jax.ShapeDtypeStruct((batch, SWA_KV_HEADS, tp, SWA_HEAD_DIM), BF16),
            jax.ShapeDtypeStruct((batch, tp, D_INNER), BF16),
        ],
        scratch_shapes=[pltpu.VMEM((tm, LAT_COLS + SWA_COLS), F32),
                        pltpu.VMEM((tm, LAT_COLS + SWA_COLS), F32)],
        compiler_params=pltpu.CompilerParams(
            dimension_semantics=("arbitrary",), vmem_limit_bytes=_vmem_limit(est)),
    )(h2d, gn, w_all, gql, gkvl, wuq, wukv, gqn, gkn, sgq, sgk, cm, sm, cs, ss)


def _mla_attn_kernel(fixed_ref, q_ref, k_ref, v_ref, gate_ref, o_ref, vx_ref, *, tp, tkc):
    @pl.when(pl.program_id(2) == 0)
    def _():
        lane = lax.broadcasted_iota(jnp.int32, (tp, MXU_DIM - MLA_V), 1)
        vx_ref[:, 0:MLA_V] = v_ref[0, 0]
        vx_ref[:, MLA_V:] = jnp.where(lane == 0, 1.0, 0.0).astype(BF16)

    @pl.when(fixed_ref[0] == 1)
    def _():
        _mla_fixed_shift(q_ref, k_ref, gate_ref, o_ref, vx_ref, tp=tp, tkc=tkc)

    @pl.when(fixed_ref[0] != 1)
    def _():
        _mla_online(q_ref, k_ref, gate_ref, o_ref, vx_ref, tp=tp, tkc=tkc)


def _mla_finish(acc, gate_ref, o_ref):
    o = acc[:, 0:MLA_V] / acc[:, MLA_V:MLA_V + 1]
    o_ref[0] = (o * _silu(gate_ref[0].astype(F32))).astype(BF16)


def _mla_fixed_shift(q_ref, k_ref, gate_ref, o_ref, vx_ref, *, tp, tkc):
    q = q_ref[0, 0]
    tq = q.shape[0]
    n_chunks = (tp - BLOCK) // tkc

    def scores(c):
        return _nt_dot(q, k_ref[0, 0, BLOCK + c * tkc:BLOCK + (c + 1) * tkc, :])

    def pv(c, p, acc):
        return acc + jnp.dot(p, vx_ref[BLOCK + c * tkc:BLOCK + (c + 1) * tkc, :], preferred_element_type=F32)

    col = lax.broadcasted_iota(jnp.int32, (tq, BLOCK), 1)
    s = jnp.where(col >= PAD_FRONT, _nt_dot(q, k_ref[0, 0, 0:BLOCK, :]), NEG_INF)
    s_next = scores(0)
    acc = jnp.dot(jnp.exp2(s).astype(BF16), vx_ref[0:BLOCK, :], preferred_element_type=F32)
    p_prev = None
    for c in range(n_chunks):
        s = s_next
        if c + 1 < n_chunks:
            s_next = scores(c + 1)
        p = jnp.exp2(s).astype(BF16)
        if p_prev is not None:
            acc = pv(c - 1, p_prev, acc)
        p_prev = p
    acc = pv(n_chunks - 1, p_prev, acc)
    _mla_finish(acc, gate_ref, o_ref)


def _mla_online(q_ref, k_ref, gate_ref, o_ref, vx_ref, *, tp, tkc):
    q = q_ref[0, 0]
    tq = q.shape[0]

    s = _nt_dot(q, k_ref[0, 0, 0:BLOCK, :])
    col = lax.broadcasted_iota(jnp.int32, (tq, BLOCK), 1)
    s = jnp.where(col >= PAD_FRONT, s, NEG_INF)
    m = jnp.max(s, axis=-1, keepdims=True)
    p = jnp.exp2(s - m)
    acc = jnp.dot(p.astype(BF16), vx_ref[0:BLOCK, :], preferred_element_type=F32)

    n_chunks = (tp - BLOCK) // tkc

    def scores(c):
        start = pl.multiple_of(BLOCK + c * tkc, BLOCK)
        return _nt_dot(q, k_ref[0, 0, pl.ds(start, tkc), :])

    def softmax_step(s, m):
        m_new = jnp.maximum(m, jnp.max(s, axis=-1, keepdims=True))
        alpha = jnp.exp2(m - m_new)
        p = jnp.exp2(s - m_new)
        return p.astype(BF16), alpha, m_new

    def pv_step(c, p, alpha, acc):
        start = pl.multiple_of(BLOCK + c * tkc, BLOCK)
        return alpha * acc + jnp.dot(p, vx_ref[pl.ds(start, tkc), :], preferred_element_type=F32)

    def chunk(c, carry):
        s, p_prev, alpha_prev, m, acc = carry
        s_next = scores(c + 1)
        p, alpha, m = softmax_step(s, m)
        acc = pv_step(c - 1, p_prev, alpha_prev, acc)
        return s_next, p, alpha, m, acc

    s1 = scores(1)
    p0, alpha0, m = softmax_step(scores(0), m)
    carry = (s1, p0, alpha0, m, acc)
    for c in range(1, n_chunks - 1):
        carry = chunk(c, carry)
    s, p_prev, alpha_prev, m, acc = carry
    p, alpha, m = softmax_step(s, m)
    acc = pv_step(n_chunks - 2, p_prev, alpha_prev, acc)
    acc = pv_step(n_chunks - 1, p, alpha, acc)
    _mla_finish(acc, gate_ref, o_ref)


def _mla_attn(fixed_flag, qm, km, vm, gate, *, batch, tp, tq, tkc):
    nq = tp // tq
    est = (2 * tp * (MLA_QK_PAD + MLA_V) * 2 + tp * MXU_DIM * 2 + 6 * tq * tkc * 4
           + 4 * tq * MLA_QK_PAD * 2)
    return pl.pallas_call(
        functools.partial(_mla_attn_kernel, tp=tp, tkc=tkc),
        name="mla_attn",
        grid=(batch, MLA_HEADS, nq),
        in_specs=[
            pl.BlockSpec(memory_space=pltpu.SMEM),
            pl.BlockSpec((1, 1, tq, MLA_QK_PAD), lambda b, h, i: (b, h, i, 0)),
            pl.BlockSpec((1, 1, tp, MLA_QK_PAD), lambda b, h, i: (b, h, 0, 0)),
            pl.BlockSpec((1, 1, tp, MLA_V), lambda b, h, i: (b, h, 0, 0)),
            pl.BlockSpec((1, tq, MLA_V), lambda b, h, i: (b, i, h)),
        ],
        out_specs=pl.BlockSpec((1, tq, MLA_V), lambda b, h, i: (b, i, h)),
        out_shape=jax.ShapeDtypeStruct((batch, tp, MLA_WIDTH), BF16),
        scratch_shapes=[pltpu.VMEM((tp, MXU_DIM), BF16)],
        compiler_params=pltpu.CompilerParams(
            dimension_semantics=("arbitrary", "arbitrary", "arbitrary"), vmem_limit_bytes=_vmem_limit(est)),
    )(fixed_flag, qm, km, vm, gate)


def _swa_masks(tp):
    band = 3 * BLOCK
    nb = tp // BLOCK
    r = (jnp.arange(SWA_REP * BLOCK, dtype=jnp.int32) % BLOCK)[:, None]
    c = jnp.arange(band, dtype=jnp.int32)[None, :]

    def mask(n):
        start = min(max((n - 1) * BLOCK, 0), tp - band)
        rq, rk = n * BLOCK + r, start + c
        return jnp.where((jnp.abs(rq - rk) <= SWA_WINDOW) & (rk >= BLOCK), 0.0, NEG_INF).astype(F32)

    band_bias = jnp.stack([mask(2), mask(0), mask(1), mask(nb - 1)])
    meta_bias = jnp.where(jnp.arange(BLOCK) >= PAD_FRONT, 0.0, NEG_INF).astype(F32)
    return band_bias, jnp.broadcast_to(meta_bias[None, :], (SWA_REP * BLOCK, BLOCK))


def _swa_attn_kernel(fixed_ref, q_ref, k_ref, v_ref, sink_ref, gate_ref, bias_ref, mbias_ref, o_ref, vx_ref,
                     *, tp, nblk):
    step = pl.program_id(2)

    @pl.when(step == 0)
    def _():
        lane = lax.broadcasted_iota(jnp.int32, (tp, MXU_DIM - SWA_HEAD_DIM), 1)
        vx_ref[:, 0:SWA_HEAD_DIM] = v_ref[0, 0]
        vx_ref[:, SWA_HEAD_DIM:] = jnp.where(lane == 0, 1.0, 0.0).astype(BF16)

    @pl.when(fixed_ref[0] == 1)
    def _():
        _swa_blocks(q_ref, k_ref, sink_ref, gate_ref, bias_ref, mbias_ref, o_ref, vx_ref,
                    tp=tp, nblk=nblk, row_max=False)

    @pl.when(fixed_ref[0] != 1)
    def _():
        _swa_blocks(q_ref, k_ref, sink_ref, gate_ref, bias_ref, mbias_ref, o_ref, vx_ref,
                    tp=tp, nblk=nblk, row_max=True)


def _swa_blocks(q_ref, k_ref, sink_ref, gate_ref, bias_ref, mbias_ref, o_ref, vx_ref, *, tp, nblk, row_max):
    step = pl.program_id(2)
    rows = SWA_REP * BLOCK
    band = 3 * BLOCK
    sink = sink_ref[0]
    g = _silu(gate_ref[0].astype(F32))
    starts, scores = [], []
    for u in range(nblk):
        n = step * nblk + u
        q = q_ref[0, :, u * BLOCK:(u + 1) * BLOCK, :].reshape(rows, SWA_HEAD_DIM)
        start = pl.multiple_of(jnp.clip((n - 1) * BLOCK, 0, tp - band), BLOCK)
        kind = jnp.where(n == 0, 1, jnp.where(n == 1, 2, jnp.where(n == tp // BLOCK - 1, 3, 0)))
        s_b = _nt_dot(q, k_ref[0, 0, pl.ds(start, band), :]) + bias_ref[kind]
        s_m = _nt_dot(q, k_ref[0, 0, 0:BLOCK, :]) + mbias_ref[...]
        starts.append(start)
        scores.append((s_b, s_m))

    for u in range(nblk):
        s_b, s_m = scores[u]
        sink_u = sink
        if row_max:
            mx = jnp.maximum(jnp.maximum(s_b[:, 0:BLOCK], s_b[:, BLOCK:2 * BLOCK]),
                             jnp.maximum(s_b[:, 2 * BLOCK:], s_m))
            m = jnp.maximum(jnp.max(mx, axis=-1, keepdims=True), sink)
            s_b, s_m, sink_u = s_b - m, s_m - m, sink - m
        p_b = jnp.exp2(s_b).astype(BF16)
        p_m = jnp.exp2(s_m).astype(BF16)
        acc = (jnp.dot(p_b, vx_ref[pl.ds(starts[u], band), :], preferred_element_type=F32)
               + jnp.dot(p_m, vx_ref[0:BLOCK, :], preferred_element_type=F32))
        l = jnp.exp2(sink_u) + acc[:, SWA_HEAD_DIM:SWA_HEAD_DIM + 1]
        o = acc[:, 0:SWA_HEAD_DIM] / l
        for r in range(SWA_REP):
            o_ref[0, u * BLOCK:(u + 1) * BLOCK, r * SWA_HEAD_DIM:(r + 1) * SWA_HEAD_DIM] = (
                o[r * BLOCK:(r + 1) * BLOCK]
                * g[u * BLOCK:(u + 1) * BLOCK, r * SWA_HEAD_DIM:(r + 1) * SWA_HEAD_DIM]).astype(BF16)


def _swa_attn(qs, ks, vs, sink, g_qn, g_kn, gate, *, batch, tp):
    nb = tp // BLOCK
    assert nb >= 4, "the four band-mask kinds assume at least four token blocks"
    nblk = _tile(nb, (11, 3, 1))
    rows_q = nblk * BLOCK
    sink2 = sink.astype(F32) * LOG2E
    score_bound = 1.01 * LOG2E * math.sqrt(SWA_HEAD_DIM) * jnp.max(jnp.abs(g_qn)) * jnp.max(jnp.abs(g_kn))
    shift = jnp.maximum(score_bound, jnp.max(sink2))
    fixed_flag = (score_bound + shift <= 2.0 * MLA_FIXED_SHIFT_MAX).astype(jnp.int32).reshape(1)
    band_bias, meta_bias = _swa_masks(tp)
    band_bias, meta_bias = band_bias - shift, meta_bias - shift
    sink_rows = jnp.repeat(sink2 - shift, BLOCK).reshape(SWA_KV_HEADS, SWA_REP * BLOCK, 1)
    gcols = SWA_REP * SWA_HEAD_DIM
    est = 4 * tp * SWA_HEAD_DIM * 2 + tp * MXU_DIM * 2 + nblk * 12 * SWA_REP * BLOCK * 4 * BLOCK * 4
    return pl.pallas_call(
        functools.partial(_swa_attn_kernel, tp=tp, nblk=nblk),
        name="swa_attn",
        grid=(batch, SWA_KV_HEADS, nb // nblk),
        in_specs=[
            pl.BlockSpec(memory_space=pltpu.SMEM),
            pl.BlockSpec((1, SWA_REP, rows_q, SWA_HEAD_DIM), lambda b, g, n: (b, g, n, 0)),
            pl.BlockSpec((1, 1, tp, SWA_HEAD_DIM), lambda b, g, n: (b, g, 0, 0)),
            pl.BlockSpec((1, 1, tp, SWA_HEAD_DIM), lambda b, g, n: (b, g, 0, 0)),
            pl.BlockSpec((1, SWA_REP * BLOCK, 1), lambda b, g, n: (g, 0, 0)),
            pl.BlockSpec((1, rows_q, gcols), lambda b, g, n: (b, n, MLA_WIDTH // gcols + g)),
            _const_spec(band_bias.shape),
            _const_spec(meta_bias.shape),
        ],
        out_specs=pl.BlockSpec((1, rows_q, gcols), lambda b, g, n: (b, n, g)),
        out_shape=jax.ShapeDtypeStruct((batch, tp, SWA_WIDTH), BF16),
        scratch_shapes=[pltpu.VMEM((tp, MXU_DIM), BF16)],
        compiler_params=pltpu.CompilerParams(
            dimension_semantics=("arbitrary", "arbitrary", "arbitrary"), vmem_limit_bytes=_vmem_limit(est)),
    )(fixed_flag, qs, ks, vs, sink_rows, gate, band_bias, meta_bias)


def _even_out_kernel(ya_ref, yb_ref, w_ref, h_ref, o_ref, *, tm):
    i = pl.program_id(1)
    d = (jnp.dot(ya_ref[0], w_ref[0:MLA_WIDTH, :], preferred_element_type=F32)
         + jnp.dot(yb_ref[0], w_ref[MLA_WIDTH:, :], preferred_element_type=F32))
    row = i * tm + lax.broadcasted_iota(jnp.int32, (tm, D_MODEL), 0)
    o_ref[...] = h_ref[...] + jnp.where(row >= PAD_FRONT, d, 0.0)


def _even_out(ya, yb, w_out, h2d, *, batch, tp, tm):
    nt = tp // tm
    est = w_out.size * 2 + 4 * tm * D_MODEL * 2 + 6 * tm * D_MODEL * 4
    return pl.pallas_call(
        functools.partial(_even_out_kernel, tm=tm),
        name="even_out",
        grid=(batch, nt),
        in_specs=[
            pl.BlockSpec((1, tm, MLA_WIDTH), lambda b, i: (b, i, 0)),
            pl.BlockSpec((1, tm, SWA_WIDTH), lambda b, i: (b, i, 0)),
            _const_spec(w_out.shape),
            _h_tile_spec(h2d, tm),
        ],
        out_specs=pl.BlockSpec((tm, D_MODEL), lambda b, i: (i, b)),
        out_shape=jax.ShapeDtypeStruct((tp, batch * D_MODEL), F32),
        compiler_params=pltpu.CompilerParams(
            dimension_semantics=("arbitrary", "arbitrary"), vmem_limit_bytes=_vmem_limit(est)),
    )(ya, yb, w_out, h2d)


def _lru_coeffs(xc, xcb, wg_ref, bg_ref, lam_ref, a_ref, b_ref, t0=None):
    live = _lru_live_mask(xc.shape[0], t0)
    for n in range(LRU_BLOCKS):
        sl = slice(n * LRU_BLOCK_DIM, (n + 1) * LRU_BLOCK_DIM)
        _lru_block(n, _lru_gate_dot(n, xcb[:, sl], wg_ref, bg_ref), xc[:, sl], lam_ref, a_ref, b_ref, live)


def _lru_live_mask(tmr, t0):
    if t0 is None:
        return None
    t_idx = t0 + (lax.broadcasted_iota(jnp.int32, (tmr, LRU_BLOCK_DIM), 0) // SUBLANE)
    return t_idx >= PAD_FRONT


def _lru_gate_dot(n, xnb, wg_ref, bg_ref):
    return jnp.dot(xnb, wg_ref[n], preferred_element_type=F32) + bg_ref[n]


def _lru_block(n, gh, xn, lam_ref, a_ref, b_ref, live):
    sl = slice(n * LRU_BLOCK_DIM, (n + 1) * LRU_BLOCK_DIM)
    tr = jnp.tanh(gh[:, :LRU_BLOCK_DIM])
    ti = jnp.tanh(gh[:, LRU_BLOCK_DIM:])
    ch = (-0.5 * LRU_C) * jax.nn.softplus(-lam_ref[:, sl])
    log_a = tr * ch + ch
    a = jnp.exp(log_a)
    a_ref[:, sl] = a
    one_m_a2 = jnp.tanh(log_a) * (-1.0 - a * a)
    root = one_m_a2 * lax.rsqrt(jnp.maximum(one_m_a2, F32_MIN_NORMAL))
    b = root * (0.5 * ti + 0.5) * xn
    b_ref[:, sl] = b if live is None else jnp.where(live, b, 0.0)


def _odd_fwd_kernel(h_ref, halo_ref, gn_ref, w_ref, cw_ref, cb_ref, wg_ref, bg_ref, lam_ref,
                    xc_ref, gate_ref, hf_ref,
                    uprev_ref, hst_ref, a_ref, b_ref, *, tc, nt):
    j = pl.program_id(0)
    tmr = tc * SUBLANE
    halo_rows = SUBLANE
    prev_rows = 2 * SUBLANE

    @pl.when(j == 0)
    def _():
        uprev_ref[...] = jnp.zeros_like(uprev_ref)
        hst_ref[...] = jnp.zeros_like(hst_ref)

    x = jnp.concatenate([h_ref[...].reshape(tmr, D_MODEL), halo_ref[...].reshape(halo_rows, D_MODEL)], axis=0)
    z = (x * _rms_scale(x, D_MODEL) * gn_ref[...]).astype(BF16)
    live = _lru_live_mask(tmr, j * tc)
    cols = ODD_CHUNK_BLOCKS * LRU_BLOCK_DIM
    n_chunks = LRU_WIDTH // cols

    def project(c):
        return jnp.dot(z, w_ref[:, c * cols:(c + 1) * cols], preferred_element_type=F32)

    pending = [project(c) for c in range(min(2, n_chunks))]
    for c in range(n_chunks):
        cs = slice(c * cols, (c + 1) * cols)
        u = pending.pop(0)
        u_next = jnp.where(j == nt - 1, 0.0, u[tmr:])
        ue = jnp.concatenate([uprev_ref[:, cs], u[0:tmr], u_next], axis=0)
        uprev_ref[:, cs] = u[tmr - prev_rows:tmr]
        xc = cb_ref[:, cs] + ue[0:tmr] * cw_ref[0:1, cs]
        for tap in range(1, CONV_WIDTH):
            xc = xc + ue[tap * SUBLANE:tap * SUBLANE + tmr] * cw_ref[tap:tap + 1, cs]
        xcb = xc.astype(BF16)
        xc_ref[:, cs] = xcb
        blocks = [c * ODD_CHUNK_BLOCKS + k for k in range(ODD_CHUNK_BLOCKS)]
        ghs = [_lru_gate_dot(n, xcb[:, k * LRU_BLOCK_DIM:(k + 1) * LRU_BLOCK_DIM], wg_ref, bg_ref)
               for k, n in enumerate(blocks)]
        if c + 2 < n_chunks:
            pending.append(project(c + 2))
        gate_ref[:, cs] = jnp.dot(z[0:tmr], w_ref[:, LRU_WIDTH + c * cols:LRU_WIDTH + (c + 1) * cols],
                                  preferred_element_type=F32).astype(BF16)
        for k, n in enumerate(blocks):
            _lru_block(n, ghs[k], xc[:, k * LRU_BLOCK_DIM:(k + 1) * LRU_BLOCK_DIM], lam_ref, a_ref, b_ref, live)

    def step(t, hs):
        r = pl.multiple_of(t * SUBLANE, SUBLANE)
        hs = a_ref[pl.ds(r, SUBLANE), :] * hs + b_ref[pl.ds(r, SUBLANE), :]
        b_ref[pl.ds(r, SUBLANE), :] = hs
        return hs

    hst_ref[...] = lax.fori_loop(0, tc, step, hst_ref[...], unroll=8)
    hf_ref[...] = b_ref[...].astype(BF16)


def _odd_fwd(h, gn, w_in, cw, cb, wg, bg, lam, *, batch, tp, tc):
    nt = tp // tc
    tmr = tc * batch
    blk = lambda cols: pl.BlockSpec((tc, batch, cols), lambda j: (j, 0, 0))
    row_blk = pl.BlockSpec((tmr, LRU_WIDTH), lambda j: (j, 0))
    est = (w_in.size * 2 + 2 * tmr * D_MODEL * 4 + 3 * 2 * tmr * LRU_WIDTH * 2
           + 2 * tmr * LRU_WIDTH * 4 + 5 * tmr * LRU_WIDTH * 4)
    return pl.pallas_call(
        functools.partial(_odd_fwd_kernel, tc=tc, nt=nt),
        name="odd_fwd",
        grid=(nt,),
        in_specs=[
            blk(D_MODEL),
            pl.BlockSpec((1, batch, D_MODEL), lambda j: (jnp.minimum((j + 1) * tc, tp - 1), 0, 0)),
            _const_spec((1, D_MODEL)),
            _const_spec(w_in.shape),
            _const_spec(cw.shape),
            _const_spec(cb.shape),
            _const_spec(wg.shape),
            _const_spec(bg.shape),
            _const_spec(lam.shape),
        ],
        out_specs=[row_blk, row_blk, row_blk],
        out_shape=[jax.ShapeDtypeStruct((tp * batch, LRU_WIDTH), BF16)] * 3,
        scratch_shapes=[pltpu.VMEM((2 * SUBLANE, LRU_WIDTH), F32),
                        pltpu.VMEM((SUBLANE, LRU_WIDTH), F32),
                        pltpu.VMEM((tmr, LRU_WIDTH), F32),
                        pltpu.VMEM((tmr, LRU_WIDTH), F32)],
        compiler_params=pltpu.CompilerParams(
            dimension_semantics=("arbitrary",), vmem_limit_bytes=_vmem_limit(est)),
    )(h, h, gn, w_in, cw, cb, wg, bg, lam)


def _odd_bwd_kernel(xc_ref, gate_ref, hf_ref, h_ref, wg_ref, bg_ref, lam_ref, w_ref, o_ref,
                    hst_ref, a_ref, b_ref, st_ref, *, tc, nt, final):
    j = pl.program_id(0)
    jj = nt - 1 - j
    tmr = tc * SUBLANE

    @pl.when(j == 0)
    def _():
        hst_ref[...] = jnp.zeros_like(hst_ref)

    xcb = xc_ref[...]
    _lru_coeffs(xcb.astype(F32), xcb, wg_ref, bg_ref, lam_ref, a_ref, b_ref)

    def step(s, hs):
        t = tc - 1 - s
        r = pl.multiple_of(t * SUBLANE, SUBLANE)
        hs = a_ref[pl.ds(r, SUBLANE), :] * hs + b_ref[pl.ds(r, SUBLANE), :]
        b_ref[pl.ds(r, SUBLANE), :] = hs
        return hs

    hst_ref[...] = lax.fori_loop(0, tc, step, hst_ref[...], unroll=8)

    y = ((hf_ref[...].astype(F32) + b_ref[...]) * _silu(gate_ref[...].astype(F32))).astype(BF16)
    d = jnp.dot(y, w_ref[...], preferred_element_type=F32)
    if not final:
        t_idx = jj * tc + (lax.broadcasted_iota(jnp.int32, (tmr, D_MODEL), 0) // SUBLANE)
        o_ref[...] = h_ref[...] + jnp.where(t_idx >= PAD_FRONT, d, 0.0).reshape(tc, SUBLANE, D_MODEL)
    else:
        @pl.when(jj >= BLOCK // tc)
        def _():
            res = h_ref[...].reshape(tmr, D_MODEL) + d
            for c in range(D_MODEL // LANE):
                st_ref[c] = res[:, c * LANE:(c + 1) * LANE]
                for b in range(SUBLANE):
                    o_ref[b, :, c * LANE:(c + 1) * LANE] = st_ref[c, pl.ds(b, tc, stride=SUBLANE), :]


def _odd_bwd(xc, gate, hf, h, wg, bg, lam, w_out, *, batch, tp, tc, final):
    nt = tp // tc
    tmr = tc * batch
    blk = lambda cols: pl.BlockSpec((tc, batch, cols), lambda j: (nt - 1 - j, 0, 0))
    if final:
        first = BLOCK // tc
        out_spec = pl.BlockSpec((batch, tc, D_MODEL), lambda j: (0, jnp.maximum(nt - 1 - j - first, 0), 0))
        out_shape = jax.ShapeDtypeStruct((batch, tp - BLOCK, D_MODEL), F32)
    else:
        out_spec = blk(D_MODEL)
        out_shape = jax.ShapeDtypeStruct((tp, batch, D_MODEL), F32)
    row_blk = pl.BlockSpec((tmr, LRU_WIDTH), lambda j: (nt - 1 - j, 0))
    est = (w_out.size * 2 + 3 * 2 * tmr * LRU_WIDTH * 2 + 4 * tmr * D_MODEL * 4
           + 2 * tmr * LRU_WIDTH * 4 + 5 * tmr * LRU_WIDTH * 4)
    return pl.pallas_call(
        functools.partial(_odd_bwd_kernel, tc=tc, nt=nt, final=final),
        name="odd_bwd",
        grid=(nt,),
        in_specs=[
            row_blk, row_blk, row_blk, blk(D_MODEL),
            _const_spec(wg.shape),
            _const_spec(bg.shape),
            _const_spec(lam.shape),
            _const_spec(w_out.shape),
        ],
        out_specs=out_spec,
        out_shape=out_shape,
        scratch_shapes=[pltpu.VMEM((SUBLANE, LRU_WIDTH), F32),
                        pltpu.VMEM((tmr, LRU_WIDTH), F32),
                        pltpu.VMEM((tmr, LRU_WIDTH), F32),
                        pltpu.VMEM((D_MODEL // LANE, tmr, LANE), F32)],
        compiler_params=pltpu.CompilerParams(
            dimension_semantics=("arbitrary",), vmem_limit_bytes=_vmem_limit(est)),
    )(xc, gate, hf, h, wg, bg, lam, w_out)


def _rope_tables(tp):
    pos = (jnp.arange(tp, dtype=F32) - PAD_FRONT)[:, None]

    def table(dim):
        inv = ROPE_THETA ** (-jnp.arange(0, dim, 2, dtype=F32) / dim)
        ang = pos * inv[None, :]
        return jnp.cos(ang), jnp.sin(ang)

    c, s = table(MLA_ROPE)
    zeros = jnp.zeros((tp, LANE - MLA_ROPE), F32)
    cm = jnp.concatenate([c, c, zeros], axis=1)
    sm = jnp.concatenate([-s, s, zeros], axis=1)
    c, s = table(SWA_HEAD_DIM)
    cs = jnp.concatenate([c, c], axis=1)
    ss = jnp.concatenate([-s, s], axis=1)
    return cm, sm, cs, ss


def _prep_even(w_in, w_uq, w_ukv, g_qn, g_kn):
    lat_w = MLA_Q_RANK + MLA_KV_RANK + MLA_ROPE
    w_bf = w_in.astype(BF16)
    w_all = jnp.concatenate(
        [w_bf[:, :lat_w], jnp.zeros((D_MODEL, LAT_COLS - lat_w), BF16), w_bf[:, lat_w:]], axis=1)
    wuq = jnp.pad(w_uq, ((0, 0), (0, 0), (0, MLA_QK_PAD - MLA_QK))).reshape(
        MLA_Q_RANK, MLA_HEADS * MLA_QK_PAD).astype(BF16)
    wukv = w_ukv.reshape(MLA_KV_RANK, MLA_HEADS * (MLA_NOPE + MLA_V)).astype(BF16)
    shift = 1.01 * LOG2E * math.sqrt(MLA_QK) * jnp.max(jnp.abs(g_qn)) * jnp.max(jnp.abs(g_kn))
    lane = jnp.arange(MLA_QK_PAD) == MLA_QK
    pad_g = lambda g: jnp.pad(g, (0, MLA_QK_PAD - MLA_QK))
    gq2 = jnp.stack([pad_g(g_qn), jnp.where(lane, -shift, 0.0)]).astype(F32)
    gk2 = jnp.stack([pad_g(g_kn), jnp.where(lane, 1.0, 0.0)]).astype(F32)
    fixed_flag = (shift <= MLA_FIXED_SHIFT_MAX).astype(jnp.int32).reshape(1)
    return w_all, wuq, wukv, gq2, gk2, fixed_flag


def _prep_gates(w_a, b_a, w_x, b_x):
    wg = (0.5 * jnp.concatenate([w_a, w_x], axis=-1)).astype(BF16)
    bg = 0.5 * jnp.concatenate([b_a.reshape(LRU_BLOCKS, 1, LRU_BLOCK_DIM),
                                b_x.reshape(LRU_BLOCKS, 1, LRU_BLOCK_DIM)], axis=-1)
    return wg, bg


def _tile(tp, prefs):
    for t in prefs:
        if tp % t == 0:
            return t
    raise ValueError(f"no tile of {prefs} divides {tp}")


def kernel(x, meta_tokens, norm_g, even_w_in, mla_g_q_lat, mla_g_kv_lat, mla_w_uq, mla_w_ukv, mla_g_qn, mla_g_kn, swa_g_qn, swa_g_kn, swa_sink, even_w_out, odd_w_in, lru_conv_w, lru_conv_b, lru_w_a, lru_b_a, lru_w_x, lru_b_x, lru_lambda, odd_w_out):
    batch, seq, d = x.shape
    assert d == D_MODEL and batch == SUBLANE and seq % BLOCK == 0
    tp = BLOCK + seq
    tm = _tile(tp, (384, 128))
    tq = _tile(tp, (2112, 1056, 384, 128))
    tkc = _tile(seq, (512, 256, 128))
    assert seq // tkc >= 2, "the MLA key-chunk pipeline needs at least two chunks"
    tc = _tile(tp, (64, 32, 16))

    meta = jnp.broadcast_to(meta_tokens.astype(x.dtype)[None], (batch, N_META, D_MODEL))
    h = jnp.concatenate([jnp.zeros((batch, PAD_FRONT, D_MODEL), x.dtype), meta, x], axis=1)
    cm, sm, cs, ss = _rope_tables(tp)

    even_w_in_bf = even_w_in.astype(BF16)
    for l in range(DEPTH):
        j = l // 2
        gn = norm_g[l].reshape(1, D_MODEL)
        if l % 2 == 0:
            w_all, wuq, wukv, gqn, gkn, fixed_flag = _prep_even(even_w_in_bf[j], mla_w_uq[j], mla_w_ukv[j],
                                                                mla_g_qn[j], mla_g_kn[j])
            h2d = h if l == 0 else h.reshape(tp, batch * D_MODEL)
            qm, km, vm, qs, ks, vs, gate = _even_in(
                h2d, gn, w_all, mla_g_q_lat[j].reshape(1, -1), mla_g_kv_lat[j].reshape(1, -1), wuq, wukv,
                gqn, gkn, swa_g_qn[j].reshape(1, -1), swa_g_kn[j].reshape(1, -1), cm, sm, cs, ss,
                batch=batch, tp=tp, tm=tm)
            ya = _mla_attn(fixed_flag, qm, km, vm, gate, batch=batch, tp=tp, tq=tq, tkc=tkc)
            yb = _swa_attn(qs, ks, vs, swa_sink[j], swa_g_qn[j], swa_g_kn[j], gate, batch=batch, tp=tp)
            h2d = _even_out(ya, yb, even_w_out[j].astype(BF16), h2d, batch=batch, tp=tp, tm=tm)
            h = h2d.reshape(tp, batch, D_MODEL)
        else:
            wgf, bgf = _prep_gates(lru_w_a[j, 0], lru_b_a[j, 0], lru_w_x[j, 0], lru_b_x[j, 0])
            wgb, bgb = _prep_gates(lru_w_a[j, 1], lru_b_a[j, 1], lru_w_x[j, 1], lru_b_x[j, 1])
            xc, gate, hf = _odd_fwd(h, gn, odd_w_in[j].astype(BF16), lru_conv_w[j], lru_conv_b[j].reshape(1, -1),
                                    wgf, bgf, lru_lambda[j, 0].reshape(1, -1), batch=batch, tp=tp, tc=tc)
            h = _odd_bwd(xc, gate, hf, h, wgb, bgb, lru_lambda[j, 1].reshape(1, -1), odd_w_out[j].astype(BF16),
                         batch=batch, tp=tp, tc=tc, final=(l == DEPTH - 1))
    return h
```

```python
import functools
import math

import jax
import jax.numpy as jnp
from jax import lax
from jax.experimental import pallas as pl
from jax.experimental.pallas import tpu as pltpu

F32 = jnp.float32
BF16 = jnp.bfloat16

D_MODEL = 1024
DEPTH = 4
N_META = 16
D_INNER = 2 * D_MODEL
ROPE_THETA = 10000.0
EPS = 1e-6
NEG_INF = -1e30

MLA_HEADS = 8
MLA_Q_RANK = 384
MLA_KV_RANK = 256
MLA_NOPE = 128
MLA_ROPE = 64
MLA_QK = MLA_NOPE + MLA_ROPE
MLA_V = 128
MLA_WIDTH = MLA_HEADS * MLA_V

SWA_HEADS = 8
SWA_KV_HEADS = 2
SWA_REP = SWA_HEADS // SWA_KV_HEADS
SWA_HEAD_DIM = 128
SWA_WINDOW = 128
SWA_WIDTH = SWA_HEADS * SWA_HEAD_DIM
SWA_KV_WIDTH = SWA_KV_HEADS * SWA_HEAD_DIM

LRU_WIDTH = D_INNER
LRU_BLOCKS = 16
LRU_BLOCK_DIM = LRU_WIDTH // LRU_BLOCKS
LRU_C = 8.0
CONV_WIDTH = 4
ODD_CHUNK_BLOCKS = 2
OUT_PROJ_CHUNK = 512

LANE = 128
SUBLANE = 8
MXU_DIM = 256
V7X_VMEM_BYTES = 64 * 1024 * 1024

BLOCK = 128
PAD_FRONT = BLOCK - N_META
MLA_QK_PAD = MXU_DIM
LAT_COLS = 768
SWA_COLS = SWA_WIDTH + 2 * SWA_KV_WIDTH
LOG2E = 1.4426950408889634
F32_MIN_NORMAL = 1.1754943508222875e-38
MLA_FIXED_SHIFT_MAX = 40.0


def _vmem_limit(nbytes):
    return int(min(V7X_VMEM_BYTES - (4 << 20), max(nbytes, 16 << 20)))


def _const_spec(shape):
    nd = len(shape)
    return pl.BlockSpec(shape, lambda *_: (0,) * nd, pipeline_mode=pl.Buffered(1))


def _rms_scale(x, n):
    return lax.rsqrt(jnp.sum(x * x, axis=-1, keepdims=True) * (1.0 / n) + EPS)


def _silu(g):
    gh = 0.5 * g
    return gh * (1.0 + jnp.tanh(gh))


def _nt_dot(a, b):
    return lax.dot_general(a, b, (((1,), (1,)), ((), ())), preferred_element_type=F32)


def _even_in_kernel(h_ref, gn_ref, w_ref, gql_ref, gkvl_ref, wuq_ref, wukv_ref, gqn_ref, gkn_ref,
                    sgq_ref, sgk_ref, cm_ref, sm_ref, cs_ref, ss_ref,
                    qm_ref, km_ref, vm_ref, qs_ref, ks_ref, vs_ref, gate_ref, zp_a_ref, zp_b_ref):
    n = pl.program_id(0)

    @pl.when(n == 0)
    def _():
        zp_b_ref[...] = jnp.zeros_like(zp_b_ref)

    @pl.when(n % 2 == 0)
    def _():
        _even_in_step(h_ref, gn_ref, w_ref, gql_ref, gkvl_ref, wuq_ref, wukv_ref, gqn_ref, gkn_ref,
                      sgq_ref, sgk_ref, cm_ref, sm_ref, cs_ref, ss_ref,
                      qm_ref, km_ref, vm_ref, qs_ref, ks_ref, vs_ref, gate_ref, zp_a_ref, zp_b_ref)

    @pl.when(n % 2 == 1)
    def _():
        _even_in_step(h_ref, gn_ref, w_ref, gql_ref, gkvl_ref, wuq_ref, wukv_ref, gqn_ref, gkn_ref,
                      sgq_ref, sgk_ref, cm_ref, sm_ref, cs_ref, ss_ref,
                      qm_ref, km_ref, vm_ref, qs_ref, ks_ref, vs_ref, gate_ref, zp_b_ref, zp_a_ref)


def _even_in_step(h_ref, gn_ref, w_ref, gql_ref, gkvl_ref, wuq_ref, wukv_ref, gqn_ref, gkn_ref,
                  sgq_ref, sgk_ref, cm_ref, sm_ref, cs_ref, ss_ref,
                  qm_ref, km_ref, vm_ref, qs_ref, ks_ref, vs_ref, gate_ref, zp_new_ref, zp_old_ref):
    cq = zp_old_ref[:, 0:MLA_Q_RANK]
    ckv = zp_old_ref[:, MLA_Q_RANK:MLA_Q_RANK + MLA_KV_RANK]
    kpe = zp_old_ref[:, MLA_Q_RANK + MLA_KV_RANK:LAT_COLS]
    cqn = (cq * _rms_scale(cq, MLA_Q_RANK) * gql_ref[...]).astype(BF16)
    ckvn = (ckv * _rms_scale(ckv, MLA_KV_RANK) * gkvl_ref[...]).astype(BF16)
    q_all = jnp.dot(cqn, wuq_ref[...], preferred_element_type=F32)
    kv = jnp.dot(ckvn, wukv_ref[...], preferred_element_type=F32)

    x = h_ref[...]
    z = (x * _rms_scale(x, D_MODEL) * gn_ref[...]).astype(BF16)
    zp_new_ref[...] = jnp.dot(z, w_ref[:, 0:LAT_COLS + SWA_COLS], preferred_element_type=F32)
    gate_ref[0] = jnp.dot(z, w_ref[:, LAT_COLS + SWA_COLS:], preferred_element_type=F32).astype(BF16)


    cm = cm_ref[...]
    sm = sm_ref[...]

    def rope_mla(t):
        rot = pltpu.roll(t, MLA_ROPE // 2, 1) + pltpu.roll(t, LANE - MLA_ROPE // 2, 1)
        return t * cm + rot * sm

    q_scale = (MLA_QK ** -0.5) * LOG2E
    gqn = gqn_ref[0:1, :]
    q_extra = gqn_ref[1:2, :]
    for hd in range(MLA_HEADS):
        qh = q_all[:, hd * MLA_QK_PAD:(hd + 1) * MLA_QK_PAD]
        qn = qh * _rms_scale(qh, MLA_QK) * gqn
        q_out = jnp.concatenate([qn[:, :MLA_NOPE], rope_mla(qn[:, MLA_NOPE:])], axis=-1) * q_scale + q_extra
        qm_ref[0, hd] = q_out.astype(BF16)

    gkn = gkn_ref[0:1, :]
    k_extra = gkn_ref[1:2, :]
    ss_pe = jnp.sum(kpe * kpe, axis=-1, keepdims=True)
    kr = rope_mla(kpe * gkn[:, MLA_NOPE:])
    for hd in range(MLA_HEADS):
        kn = kv[:, hd * MLA_QK_PAD:hd * MLA_QK_PAD + MLA_NOPE]
        rs = lax.rsqrt((jnp.sum(kn * kn, axis=-1, keepdims=True) + ss_pe) * (1.0 / MLA_QK) + EPS)
        k_out = jnp.concatenate([kn * rs * gkn[:, :MLA_NOPE], kr * rs], axis=-1) + k_extra
        km_ref[0, hd] = k_out.astype(BF16)
        vm_ref[0, hd] = kv[:, hd * MLA_QK_PAD + MLA_NOPE:(hd + 1) * MLA_QK_PAD].astype(BF16)

    sw = zp_old_ref.at[:, LAT_COLS:LAT_COLS + SWA_COLS]
    cs = cs_ref[...]
    ss = ss_ref[...]

    def norm_rope_swa(t, g):
        n = t * _rms_scale(t, SWA_HEAD_DIM) * g
        return n * cs + pltpu.roll(n, SWA_HEAD_DIM // 2, 1) * ss

    s_scale = (SWA_HEAD_DIM ** -0.5) * LOG2E
    for hd in range(SWA_HEADS):
        t = sw[:, hd * SWA_HEAD_DIM:(hd + 1) * SWA_HEAD_DIM]
        qs_ref[0, hd] = (norm_rope_swa(t, sgq_ref[...]) * s_scale).astype(BF16)
    for hd in range(SWA_KV_HEADS):
        t = sw[:, SWA_WIDTH + hd * SWA_HEAD_DIM:SWA_WIDTH + (hd + 1) * SWA_HEAD_DIM]
        ks_ref[0, hd] = norm_rope_swa(t, sgk_ref[...]).astype(BF16)
        vs_ref[0, hd] = sw[:, SWA_WIDTH + SWA_KV_WIDTH + hd * SWA_HEAD_DIM:
                           SWA_WIDTH + SWA_KV_WIDTH + (hd + 1) * SWA_HEAD_DIM].astype(BF16)


def _h_tile_spec(h, tm, tile_of=lambda b, i: (b, i)):
    if h.ndim == 3:
        return pl.BlockSpec((None, tm, D_MODEL), lambda *g: (*tile_of(*g), 0))
    return pl.BlockSpec((tm, D_MODEL), lambda *g: tile_of(*g)[::-1])


def _even_in(h2d, gn, w_all, gql, gkvl, wuq, wukv, gqn, gkn, sgq, sgk, cm, sm, cs, ss, *, batch, tp, tm):
    nt = tp // tm
    n_tiles = batch * nt
    new_tile = lambda n: divmod(jnp.minimum(n, n_tiles - 1), nt)
    old_tile = lambda n: divmod(jnp.maximum(n - 1, 0), nt)
    row_spec = lambda cols: pl.BlockSpec((tm, cols), lambda n: (old_tile(n)[1], 0))
    head_out = lambda heads, cols: pl.BlockSpec(
        (1, heads, tm, cols), lambda n: (old_tile(n)[0], 0, old_tile(n)[1], 0))
    n_w = w_all.shape[1]
    est = (w_all.size * 2 + (wuq.size + wukv.size) * 2 + 2 * tm * D_MODEL * 4
           + 2 * tm * 2 * (2 * MLA_HEADS * MLA_QK_PAD + MLA_WIDTH + SWA_WIDTH + 2 * SWA_KV_WIDTH)
           + 2 * tm * D_INNER * 2 + 2 * tm * (LAT_COLS + SWA_COLS) * 4 + 3 * tm * n_w * 4)
    return pl.pallas_call(
        _even_in_kernel,
        name="even_in",
        grid=(n_tiles + 1,),
        in_specs=[
            _h_tile_spec(h2d, tm, new_tile),
            _const_spec((1, D_MODEL)),
            _const_spec(w_all.shape),
            _const_spec((1, MLA_Q_RANK)),
            _const_spec((1, MLA_KV_RANK)),
            _const_spec(wuq.shape),
            _const_spec(wukv.shape),
            _const_spec((2, MLA_QK_PAD)),
            _const_spec((2, MLA_QK_PAD)),
            _const_spec((1, SWA_HEAD_DIM)),
            _const_spec((1, SWA_HEAD_DIM)),
            row_spec(LANE), row_spec(LANE), row_spec(LANE), row_spec(LANE),
        ],
        out_specs=[
            head_out(MLA_HEADS, MLA_QK_PAD),
            head_out(MLA_HEADS, MLA_QK_PAD),
            head_out(MLA_HEADS, MLA_V),
            head_out(SWA_HEADS, SWA_HEAD_DIM),
            head_out(SWA_KV_HEADS, SWA_HEAD_DIM),
            head_out(SWA_KV_HEADS, SWA_HEAD_DIM),
            pl.BlockSpec((1, tm, D_INNER), lambda n: (*new_tile(n), 0)),
        ],
        out_shape=[
            jax.ShapeDtypeStruct((batch, MLA_HEADS, tp, MLA_QK_PAD), BF16),
            jax.ShapeDtypeStruct((batch, MLA_HEADS, tp, MLA_QK_PAD), BF16),
            jax.ShapeDtypeStruct((batch, MLA_HEADS, tp, MLA_V), BF16),
            jax.ShapeDtypeStruct((batch, SWA_HEADS, tp, SWA_HEAD_DIM), BF16),
            jax.ShapeDtypeStruct((batch, SWA_KV_HEADS, tp, SWA_HEAD_DIM), BF16),
            jax.ShapeDtypeStruct((batch, SWA_KV_HEADS, tp, SWA_HEAD_DIM), BF16),
            jax.ShapeDtypeStruct((batch, tp, D_INNER), BF16),
        ],
        scratch_shapes=[pltpu.VMEM((tm, LAT_COLS + SWA_COLS), F32),
                        pltpu.VMEM((tm, LAT_COLS + SWA_COLS), F32)],
        compiler_params=pltpu.CompilerParams(
            dimension_semantics=("arbitrary",), vmem_limit_bytes=_vmem_limit(est)),
    )(h2d, gn, w_all, gql, gkvl, wuq, wukv, gqn, gkn, sgq, sgk, cm, sm, cs, ss)


def _mla_attn_kernel(fixed_ref, q_ref, k_ref, v_ref, gate_ref, o_ref, vx_ref, *, tp, tkc):
    @pl.when(pl.program_id(2) == 0)
    def _():
        lane = lax.broadcasted_iota(jnp.int32, (tp, MXU_DIM - MLA_V), 1)
        vx_ref[:, 0:MLA_V] = v_ref[0, 0]
        vx_ref[:, MLA_V:] = jnp.where(lane == 0, 1.0, 0.0).astype(BF16)

    @pl.when(fixed_ref[0] == 1)
    def _():
        _mla_fixed_shift(q_ref, k_ref, gate_ref, o_ref, vx_ref, tp=tp, tkc=tkc)

    @pl.when(fixed_ref[0] != 1)
    def _():
        _mla_online(q_ref, k_ref, gate_ref, o_ref, vx_ref, tp=tp, tkc=tkc)


def _mla_finish(acc, gate_ref, o_ref):
    o = acc[:, 0:MLA_V] / acc[:, MLA_V:MLA_V + 1]
    o_ref[0] = (o * _silu(gate_ref[0].astype(F32))).astype(BF16)


def _mla_fixed_shift(q_ref, k_ref, gate_ref, o_ref, vx_ref, *, tp, tkc):
    q = q_ref[0, 0]
    tq = q.shape[0]
    n_chunks = (tp - BLOCK) // tkc

    def scores(c):
        return _nt_dot(q, k_ref[0, 0, BLOCK + c * tkc:BLOCK + (c + 1) * tkc, :])

    def pv(c, p, acc):
        return acc + jnp.dot(p, vx_ref[BLOCK + c * tkc:BLOCK + (c + 1) * tkc, :], preferred_element_type=F32)

    col = lax.broadcasted_iota(jnp.int32, (tq, BLOCK), 1)
    s = jnp.where(col >= PAD_FRONT, _nt_dot(q, k_ref[0, 0, 0:BLOCK, :]), NEG_INF)
    s_next = scores(0)
    acc = jnp.dot(jnp.exp2(s).astype(BF16), vx_ref[0:BLOCK, :], preferred_element_type=F32)
    p_prev = None
    for c in range(n_chunks):
        s = s_next
        if c + 1 < n_chunks:
            s_next = scores(c + 1)
        p = jnp.exp2(s).astype(BF16)
        if p_prev is not None:
            acc = pv(c - 1, p_prev, acc)
        p_prev = p
    acc = pv(n_chunks - 1, p_prev, acc)
    _mla_finish(acc, gate_ref, o_ref)


def _mla_online(q_ref, k_ref, gate_ref, o_ref, vx_ref, *, tp, tkc):
    q = q_ref[0, 0]
    tq = q.shape[0]

    s = _nt_dot(q, k_ref[0, 0, 0:BLOCK, :])
    col = lax.broadcasted_iota(jnp.int32, (tq, BLOCK), 1)
    s = jnp.where(col >= PAD_FRONT, s, NEG_INF)
    m = jnp.max(s, axis=-1, keepdims=True)
    p = jnp.exp2(s - m)
    acc = jnp.dot(p.astype(BF16), vx_ref[0:BLOCK, :], preferred_element_type=F32)

    n_chunks = (tp - BLOCK) // tkc

    def scores(c):
        start = pl.multiple_of(BLOCK + c * tkc, BLOCK)
        return _nt_dot(q, k_ref[0, 0, pl.ds(start, tkc), :])

    def softmax_step(s, m):
        m_new = jnp.maximum(m, jnp.max(s, axis=-1, keepdims=True))
        alpha = jnp.exp2(m - m_new)
        p = jnp.exp2(s - m_new)
        return p.astype(BF16), alpha, m_new

    def pv_step(c, p, alpha, acc):
        start = pl.multiple_of(BLOCK + c * tkc, BLOCK)
        return alpha * acc + jnp.dot(p, vx_ref[pl.ds(start, tkc), :], preferred_element_type=F32)

    def chunk(c, carry):
        s, p_prev, alpha_prev, m, acc = carry
        s_next = scores(c + 1)
        p, alpha, m = softmax_step(s, m)
        acc = pv_step(c - 1, p_prev, alpha_prev, acc)
        return s_next, p, alpha, m, acc

    s1 = scores(1)
    p0, alpha0, m = softmax_step(scores(0), m)
    carry = (s1, p0, alpha0, m, acc)
    for c in range(1, n_chunks - 1):
        carry = chunk(c, carry)
    s, p_prev, alpha_prev, m, acc = carry
    p, alpha, m = softmax_step(s, m)
    acc = pv_step(n_chunks - 2, p_prev, alpha_prev, acc)
    acc = pv_step(n_chunks - 1, p, alpha, acc)
    _mla_finish(acc, gate_ref, o_ref)


def _mla_attn(fixed_flag, qm, km, vm, gate, *, batch, tp, tq, tkc):
    nq = tp // tq
    est = (2 * tp * (MLA_QK_PAD + MLA_V) * 2 + tp * MXU_DIM * 2 + 6 * tq * tkc * 4
           + 4 * tq * MLA_QK_PAD * 2)
    return pl.pallas_call(
        functools.partial(_mla_attn_kernel, tp=tp, tkc=tkc),
        name="mla_attn",
        grid=(batch, MLA_HEADS, nq),
        in_specs=[
            pl.BlockSpec(memory_space=pltpu.SMEM),
            pl.BlockSpec((1, 1, tq, MLA_QK_PAD), lambda b, h, i: (b, h, i, 0)),
            pl.BlockSpec((1, 1, tp, MLA_QK_PAD), lambda b, h, i: (b, h, 0, 0)),
            pl.BlockSpec((1, 1, tp, MLA_V), lambda b, h, i: (b, h, 0, 0)),
            pl.BlockSpec((1, tq, MLA_V), lambda b, h, i: (b, i, h)),
        ],
        out_specs=pl.BlockSpec((1, tq, MLA_V), lambda b, h, i: (b, i, h)),
        out_shape=jax.ShapeDtypeStruct((batch, tp, MLA_WIDTH), BF16),
        scratch_shapes=[pltpu.VMEM((tp, MXU_DIM), BF16)],
        compiler_params=pltpu.CompilerParams(
            dimension_semantics=("arbitrary", "arbitrary", "arbitrary"), vmem_limit_bytes=_vmem_limit(est)),
    )(fixed_flag, qm, km, vm, gate)


def _swa_masks(tp):
    band = 3 * BLOCK
    nb = tp // BLOCK
    r = (jnp.arange(SWA_REP * BLOCK, dtype=jnp.int32) % BLOCK)[:, None]
    c = jnp.arange(band, dtype=jnp.int32)[None, :]

    def mask(n):
        start = min(max((n - 1) * BLOCK, 0), tp - band)
        rq, rk = n * BLOCK + r, start + c
        return jnp.where((jnp.abs(rq - rk) <= SWA_WINDOW) & (rk >= BLOCK), 0.0, NEG_INF).astype(F32)

    band_bias = jnp.stack([mask(2), mask(0), mask(1), mask(nb - 1)])
    meta_bias = jnp.where(jnp.arange(BLOCK) >= PAD_FRONT, 0.0, NEG_INF).astype(F32)
    return band_bias, jnp.broadcast_to(meta_bias[None, :], (SWA_REP * BLOCK, BLOCK))


def _swa_attn_kernel(fixed_ref, q_ref, k_ref, v_ref, sink_ref, gate_ref, bias_ref, mbias_ref, o_ref, vx_ref,
                     *, tp, nblk):
    step = pl.program_id(2)

    @pl.when(step == 0)
    def _():
        lane = lax.broadcasted_iota(jnp.int32, (tp, MXU_DIM - SWA_HEAD_DIM), 1)
        vx_ref[:, 0:SWA_HEAD_DIM] = v_ref[0, 0]
        vx_ref[:, SWA_HEAD_DIM:] = jnp.where(lane == 0, 1.0, 0.0).astype(BF16)

    @pl.when(fixed_ref[0] == 1)
    def _():
        _swa_blocks(q_ref, k_ref, sink_ref, gate_ref, bias_ref, mbias_ref, o_ref, vx_ref,
                    tp=tp, nblk=nblk, row_max=False)

    @pl.when(fixed_ref[0] != 1)
    def _():
        _swa_blocks(q_ref, k_ref, sink_ref, gate_ref, bias_ref, mbias_ref, o_ref, vx_ref,
                    tp=tp, nblk=nblk, row_max=True)


def _swa_blocks(q_ref, k_ref, sink_ref, gate_ref, bias_ref, mbias_ref, o_ref, vx_ref, *, tp, nblk, row_max):
    step = pl.program_id(2)
    rows = SWA_REP * BLOCK
    band = 3 * BLOCK
    sink = sink_ref[0]
    g = _silu(gate_ref[0].astype(F32))
    starts, scores = [], []
    for u in range(nblk):
        n = step * nblk + u
        q = q_ref[0, :, u * BLOCK:(u + 1) * BLOCK, :].reshape(rows, SWA_HEAD_DIM)
        start = pl.multiple_of(jnp.clip((n - 1) * BLOCK, 0, tp - band), BLOCK)
        kind = jnp.where(n == 0, 1, jnp.where(n == 1, 2, jnp.where(n == tp // BLOCK - 1, 3, 0)))
        s_b = _nt_dot(q, k_ref[0, 0, pl.ds(start, band), :]) + bias_ref[kind]
        s_m = _nt_dot(q, k_ref[0, 0, 0:BLOCK, :]) + mbias_ref[...]
        starts.append(start)
        scores.append((s_b, s_m))

    for u in range(nblk):
        s_b, s_m = scores[u]
        sink_u = sink
        if row_max:
            mx = jnp.maximum(jnp.maximum(s_b[:, 0:BLOCK], s_b[:, BLOCK:2 * BLOCK]),
                             jnp.maximum(s_b[:, 2 * BLOCK:], s_m))
            m = jnp.maximum(jnp.max(mx, axis=-1, keepdims=True), sink)
            s_b, s_m, sink_u = s_b - m, s_m - m, sink - m
        p_b = jnp.exp2(s_b).astype(BF16)
        p_m = jnp.exp2(s_m).astype(BF16)
        acc = (jnp.dot(p_b, vx_ref[pl.ds(starts[u], band), :], preferred_element_type=F32)
               + jnp.dot(p_m, vx_ref[0:BLOCK, :], preferred_element_type=F32))
        l = jnp.exp2(sink_u) + acc[:, SWA_HEAD_DIM:SWA_HEAD_DIM + 1]
        o = acc[:, 0:SWA_HEAD_DIM] / l
        for r in range(SWA_REP):
            o_ref[0, u * BLOCK:(u + 1) * BLOCK, r * SWA_HEAD_DIM:(r + 1) * SWA_HEAD_DIM] = (
                o[r * BLOCK:(r + 1) * BLOCK]
                * g[u * BLOCK:(u + 1) * BLOCK, r * SWA_HEAD_DIM:(r + 1) * SWA_HEAD_DIM]).astype(BF16)


def _swa_attn(qs, ks, vs, sink, g_qn, g_kn, gate, *, batch, tp):
    nb = tp // BLOCK
    assert nb >= 4, "the four band-mask kinds assume at least four token blocks"
    nblk = _tile(nb, (11, 3, 1))
    rows_q = nblk * BLOCK
    sink2 = sink.astype(F32) * LOG2E
    score_bound = 1.01 * LOG2E * math.sqrt(SWA_HEAD_DIM) * jnp.max(jnp.abs(g_qn)) * jnp.max(jnp.abs(g_kn))
    shift = jnp.maximum(score_bound, jnp.max(sink2))
    fixed_flag = (score_bound + shift <= 2.0 * MLA_FIXED_SHIFT_MAX).astype(jnp.int32).reshape(1)
    band_bias, meta_bias = _swa_masks(tp)
    band_bias, meta_bias = band_bias - shift, meta_bias - shift
    sink_rows = jnp.repeat(sink2 - shift, BLOCK).reshape(SWA_KV_HEADS, SWA_REP * BLOCK, 1)
    gcols = SWA_REP * SWA_HEAD_DIM
    est = 4 * tp * SWA_HEAD_DIM * 2 + tp * MXU_DIM * 2 + nblk * 12 * SWA_REP * BLOCK * 4 * BLOCK * 4
    return pl.pallas_call(
        functools.partial(_swa_attn_kernel, tp=tp, nblk=nblk),
        name="swa_attn",
        grid=(batch, SWA_KV_HEADS, nb // nblk),
        in_specs=[
            pl.BlockSpec(memory_space=pltpu.SMEM),
            pl.BlockSpec((1, SWA_REP, rows_q, SWA_HEAD_DIM), lambda b, g, n: (b, g, n, 0)),
            pl.BlockSpec((1, 1, tp, SWA_HEAD_DIM), lambda b, g, n: (b, g, 0, 0)),
            pl.BlockSpec((1, 1, tp, SWA_HEAD_DIM), lambda b, g, n: (b, g, 0, 0)),
            pl.BlockSpec((1, SWA_REP * BLOCK, 1), lambda b, g, n: (g, 0, 0)),
            pl.BlockSpec((1, rows_q, gcols), lambda b, g, n: (b, n, MLA_WIDTH // gcols + g)),
            _const_spec(band_bias.shape),
            _const_spec(meta_bias.shape),
        ],
        out_specs=pl.BlockSpec((1, rows_q, gcols), lambda b, g, n: (b, n, g)),
        out_shape=jax.ShapeDtypeStruct((batch, tp, SWA_WIDTH), BF16),
        scratch_shapes=[pltpu.VMEM((tp, MXU_DIM), BF16)],
        compiler_params=pltpu.CompilerParams(
            dimension_semantics=("arbitrary", "arbitrary", "arbitrary"), vmem_limit_bytes=_vmem_limit(est)),
    )(fixed_flag, qs, ks, vs, sink_rows, gate, band_bias, meta_bias)


def _even_out_kernel(ya_ref, yb_ref, w_ref, h_ref, o_ref, *, tm):
    i = pl.program_id(1)
    d = (jnp.dot(ya_ref[0], w_ref[0:MLA_WIDTH, :], preferred_element_type=F32)
         + jnp.dot(yb_ref[0], w_ref[MLA_WIDTH:, :], preferred_element_type=F32))
    row = i * tm + lax.broadcasted_iota(jnp.int32, (tm, D_MODEL), 0)
    o_ref[...] = h_ref[...] + jnp.where(row >= PAD_FRONT, d, 0.0)


def _even_out(ya, yb, w_out, h2d, *, batch, tp, tm):
    nt = tp // tm
    est = w_out.size * 2 + 4 * tm * D_MODEL * 2 + 6 * tm * D_MODEL * 4
    return pl.pallas_call(
        functools.partial(_even_out_kernel, tm=tm),
        name="even_out",
        grid=(batch, nt),
        in_specs=[
            pl.BlockSpec((1, tm, MLA_WIDTH), lambda b, i: (b, i, 0)),
            pl.BlockSpec((1, tm, SWA_WIDTH), lambda b, i: (b, i, 0)),
            _const_spec(w_out.shape),
            _h_tile_spec(h2d, tm),
        ],
        out_specs=pl.BlockSpec((tm, D_MODEL), lambda b, i: (i, b)),
        out_shape=jax.ShapeDtypeStruct((tp, batch * D_MODEL), F32),
        compiler_params=pltpu.CompilerParams(
            dimension_semantics=("arbitrary", "arbitrary"), vmem_limit_bytes=_vmem_limit(est)),
    )(ya, yb, w_out, h2d)


def _lru_coeffs(xc, xcb, wg_ref, bg_ref, lam_ref, a_ref, b_ref, t0=None):
    live = _lru_live_mask(xc.shape[0], t0)
    for n in range(LRU_BLOCKS):
        sl = slice(n * LRU_BLOCK_DIM, (n + 1) * LRU_BLOCK_DIM)
        _lru_block(n, _lru_gate_dot(n, xcb[:, sl], wg_ref, bg_ref), xc[:, sl], lam_ref, a_ref, b_ref, live)


def _lru_live_mask(tmr, t0):
    if t0 is None:
        return None
    t_idx = t0 + (lax.broadcasted_iota(jnp.int32, (tmr, LRU_BLOCK_DIM), 0) // SUBLANE)
    return t_idx >= PAD_FRONT


def _lru_gate_dot(n, xnb, wg_ref, bg_ref):
    return jnp.dot(xnb, wg_ref[n], preferred_element_type=F32) + bg_ref[n]


def _lru_block(n, gh, xn, lam_ref, a_ref, b_ref, live):
    sl = slice(n * LRU_BLOCK_DIM, (n + 1) * LRU_BLOCK_DIM)
    tr = jnp.tanh(gh[:, :LRU_BLOCK_DIM])
    ti = jnp.tanh(gh[:, LRU_BLOCK_DIM:])
    ch = (-0.5 * LRU_C) * jax.nn.softplus(-lam_ref[:, sl])
    log_a = tr * ch + ch
    a = jnp.exp(log_a)
    a_ref[:, sl] = a
    one_m_a2 = jnp.tanh(log_a) * (-1.0 - a * a)
    root = one_m_a2 * lax.rsqrt(jnp.maximum(one_m_a2, F32_MIN_NORMAL))
    b = root * (0.5 * ti + 0.5) * xn
    b_ref[:, sl] = b if live is None else jnp.where(live, b, 0.0)


def _odd_fwd_kernel(h_ref, halo_ref, gn_ref, w_ref, cw_ref, cb_ref, wg_ref, bg_ref, lam_ref,
                    xc_ref, gate_ref, hf_ref,
                    uprev_ref, hst_ref, a_ref, b_ref, *, tc, nt):
    j = pl.program_id(0)
    tmr = tc * SUBLANE
    halo_rows = SUBLANE
    prev_rows = 2 * SUBLANE

    @pl.when(j == 0)
    def _():
        uprev_ref[...] = jnp.zeros_like(uprev_ref)
        hst_ref[...] = jnp.zeros_like(hst_ref)

    x = jnp.concatenate([h_ref[...].reshape(tmr, D_MODEL), halo_ref[...].reshape(halo_rows, D_MODEL)], axis=0)
    z = (x * _rms_scale(x, D_MODEL) * gn_ref[...]).astype(BF16)
    live = _lru_live_mask(tmr, j * tc)
    cols = ODD_CHUNK_BLOCKS * LRU_BLOCK_DIM
    n_chunks = LRU_WIDTH // cols

    def project(c):
        return jnp.dot(z, w_ref[:, c * cols:(c + 1) * cols], preferred_element_type=F32)

    pending = [project(c) for c in range(min(2, n_chunks))]
    for c in range(n_chunks):
        cs = slice(c * cols, (c + 1) * cols)
        u = pending.pop(0)
        u_next = jnp.where(j == nt - 1, 0.0, u[tmr:])
        ue = jnp.concatenate([uprev_ref[:, cs], u[0:tmr], u_next], axis=0)
        uprev_ref[:, cs] = u[tmr - prev_rows:tmr]
        xc = cb_ref[:, cs] + ue[0:tmr] * cw_ref[0:1, cs]
        for tap in range(1, CONV_WIDTH):
            xc = xc + ue[tap * SUBLANE:tap * SUBLANE + tmr] * cw_ref[tap:tap + 1, cs]
        xcb = xc.astype(BF16)
        xc_ref[:, cs] = xcb
        blocks = [c * ODD_CHUNK_BLOCKS + k for k in range(ODD_CHUNK_BLOCKS)]
        ghs = [_lru_gate_dot(n, xcb[:, k * LRU_BLOCK_DIM:(k + 1) * LRU_BLOCK_DIM], wg_ref, bg_ref)
               for k, n in enumerate(blocks)]
        if c + 2 < n_chunks:
            pending.append(project(c + 2))
        gate_ref[:, cs] = jnp.dot(z[0:tmr], w_ref[:, LRU_WIDTH + c * cols:LRU_WIDTH + (c + 1) * cols],
                                  preferred_element_type=F32).astype(BF16)
        for k, n in enumerate(blocks):
            _lru_block(n, ghs[k], xc[:, k * LRU_BLOCK_DIM:(k + 1) * LRU_BLOCK_DIM], lam_ref, a_ref, b_ref, live)

    def step(t, hs):
        r = pl.multiple_of(t * SUBLANE, SUBLANE)
        hs = a_ref[pl.ds(r, SUBLANE), :] * hs + b_ref[pl.ds(r, SUBLANE), :]
        b_ref[pl.ds(r, SUBLANE), :] = hs
        return hs

    hst_ref[...] = lax.fori_loop(0, tc, step, hst_ref[...], unroll=8)
    hf_ref[...] = b_ref[...].astype(BF16)


def _odd_fwd(h, gn, w_in, cw, cb, wg, bg, lam, *, batch, tp, tc):
    nt = tp // tc
    tmr = tc * batch
    blk = lambda cols: pl.BlockSpec((tc, batch, cols), lambda j: (j, 0, 0))
    row_blk = pl.BlockSpec((tmr, LRU_WIDTH), lambda j: (j, 0))
    est = (w_in.size * 2 + 2 * tmr * D_MODEL * 4 + 3 * 2 * tmr * LRU_WIDTH * 2
           + 2 * tmr * LRU_WIDTH * 4 + 5 * tmr * LRU_WIDTH * 4)
    return pl.pallas_call(
        functools.partial(_odd_fwd_kernel, tc=tc, nt=nt),
        name="odd_fwd",
        grid=(nt,),
        in_specs=[
            blk(D_MODEL),
            pl.BlockSpec((1, batch, D_MODEL), lambda j: (jnp.minimum((j + 1) * tc, tp - 1), 0, 0)),
            _const_spec((1, D_MODEL)),
            _const_spec(w_in.shape),
            _const_spec(cw.shape),
            _const_spec(cb.shape),
            _const_spec(wg.shape),
            _const_spec(bg.shape),
            _const_spec(lam.shape),
        ],
        out_specs=[row_blk, row_blk, row_blk],
        out_shape=[jax.ShapeDtypeStruct((tp * batch, LRU_WIDTH), BF16)] * 3,
        scratch_shapes=[pltpu.VMEM((2 * SUBLANE, LRU_WIDTH), F32),
                        pltpu.VMEM((SUBLANE, LRU_WIDTH), F32),
                        pltpu.VMEM((tmr, LRU_WIDTH), F32),
                        pltpu.VMEM((tmr, LRU_WIDTH), F32)],
        compiler_params=pltpu.CompilerParams(
            dimension_semantics=("arbitrary",), vmem_limit_bytes=_vmem_limit(est)),
    )(h, h, gn, w_in, cw, cb, wg, bg, lam)


def _odd_bwd_kernel(xc_ref, gate_ref, hf_ref, h_ref, wg_ref, bg_ref, lam_ref, w_ref, o_ref,
                    hst_ref, a_ref, b_ref, st_ref, *, tc, nt, final):
    j = pl.program_id(0)
    jj = nt - 1 - j
    tmr = tc * SUBLANE

    @pl.when(j == 0)
    def _():
        hst_ref[...] = jnp.zeros_like(hst_ref)

    xcb = xc_ref[...]
    _lru_coeffs(xcb.astype(F32), xcb, wg_ref, bg_ref, lam_ref, a_ref, b_ref)

    def step(s, hs):
        t = tc - 1 - s
        r = pl.multiple_of(t * SUBLANE, SUBLANE)
        hs = a_ref[pl.ds(r, SUBLANE), :] * hs + b_ref[pl.ds(r, SUBLANE), :]
        b_ref[pl.ds(r, SUBLANE), :] = hs
        return hs

    hst_ref[...] = lax.fori_loop(0, tc, step, hst_ref[...], unroll=8)

    d = None
    for c in range(LRU_WIDTH // OUT_PROJ_CHUNK):
        cs = slice(c * OUT_PROJ_CHUNK, (c + 1) * OUT_PROJ_CHUNK)
        y = ((hf_ref[:, cs].astype(F32) + b_ref[:, cs]) * _silu(gate_ref[:, cs].astype(F32))).astype(BF16)
        part = jnp.dot(y, w_ref[cs, :], preferred_element_type=F32)
        d = part if d is None else d + part
    if not final:
        t_idx = jj * tc + (lax.broadcasted_iota(jnp.int32, (tmr, D_MODEL), 0) // SUBLANE)
        o_ref[...] = h_ref[...] + jnp.where(t_idx >= PAD_FRONT, d, 0.0).reshape(tc, SUBLANE, D_MODEL)
    else:
        @pl.when(jj >= BLOCK // tc)
        def _():
            res = h_ref[...].reshape(tmr, D_MODEL) + d
            for c in range(D_MODEL // LANE):
                st_ref[c] = res[:, c * LANE:(c + 1) * LANE]
                for b in range(SUBLANE):
                    o_ref[b, :, c * LANE:(c + 1) * LANE] = st_ref[c, pl.ds(b, tc, stride=SUBLANE), :]


def _odd_bwd(xc, gate, hf, h, wg, bg, lam, w_out, *, batch, tp, tc, final):
    nt = tp // tc
    tmr = tc * batch
    blk = lambda cols: pl.BlockSpec((tc, batch, cols), lambda j: (nt - 1 - j, 0, 0))
    if final:
        first = BLOCK // tc
        out_spec = pl.BlockSpec((batch, tc, D_MODEL), lambda j: (0, jnp.maximum(nt - 1 - j - first, 0), 0))
        out_shape = jax.ShapeDtypeStruct((batch, tp - BLOCK, D_MODEL), F32)
    else:
        out_spec = blk(D_MODEL)
        out_shape = jax.ShapeDtypeStruct((tp, batch, D_MODEL), F32)
    row_blk = pl.BlockSpec((tmr, LRU_WIDTH), lambda j: (nt - 1 - j, 0))
    est = (w_out.size * 2 + 3 * 2 * tmr * LRU_WIDTH * 2 + 4 * tmr * D_MODEL * 4
           + 2 * tmr * LRU_WIDTH * 4 + 5 * tmr * LRU_WIDTH * 4)
    return pl.pallas_call(
        functools.partial(_odd_bwd_kernel, tc=tc, nt=nt, final=final),
        name="odd_bwd",
        grid=(nt,),
        in_specs=[
            row_blk, row_blk, row_blk, blk(D_MODEL),
            _const_spec(wg.shape),
            _const_spec(bg.shape),
            _const_spec(lam.shape),
            _const_spec(w_out.shape),
        ],
        out_specs=out_spec,
        out_shape=out_shape,
        scratch_shapes=[pltpu.VMEM((SUBLANE, LRU_WIDTH), F32),
                        pltpu.VMEM((tmr, LRU_WIDTH), F32),
                        pltpu.VMEM((tmr, LRU_WIDTH), F32),
                        pltpu.VMEM((D_MODEL // LANE, tmr, LANE), F32)],
        compiler_params=pltpu.CompilerParams(
            dimension_semantics=("arbitrary",), vmem_limit_bytes=_vmem_limit(est)),
    )(xc, gate, hf, h, wg, bg, lam, w_out)


def _rope_tables(tp):
    pos = (jnp.arange(tp, dtype=F32) - PAD_FRONT)[:, None]

    def table(dim):
        inv = ROPE_THETA ** (-jnp.arange(0, dim, 2, dtype=F32) / dim)
        ang = pos * inv[None, :]
        return jnp.cos(ang), jnp.sin(ang)

    c, s = table(MLA_ROPE)
    zeros = jnp.zeros((tp, LANE - MLA_ROPE), F32)
    cm = jnp.concatenate([c, c, zeros], axis=1)
    sm = jnp.concatenate([-s, s, zeros], axis=1)
    c, s = table(SWA_HEAD_DIM)
    cs = jnp.concatenate([c, c], axis=1)
    ss = jnp.concatenate([-s, s], axis=1)
    return cm, sm, cs, ss


def _prep_even(w_in, w_uq, w_ukv, g_qn, g_kn):
    lat_w = MLA_Q_RANK + MLA_KV_RANK + MLA_ROPE
    w_bf = w_in.astype(BF16)
    w_all = jnp.concatenate(
        [w_bf[:, :lat_w], jnp.zeros((D_MODEL, LAT_COLS - lat_w), BF16), w_bf[:, lat_w:]], axis=1)
    wuq = jnp.pad(w_uq, ((0, 0), (0, 0), (0, MLA_QK_PAD - MLA_QK))).reshape(
        MLA_Q_RANK, MLA_HEADS * MLA_QK_PAD).astype(BF16)
    wukv = w_ukv.reshape(MLA_KV_RANK, MLA_HEADS * (MLA_NOPE + MLA_V)).astype(BF16)
    shift = 1.01 * LOG2E * math.sqrt(MLA_QK) * jnp.max(jnp.abs(g_qn)) * jnp.max(jnp.abs(g_kn))
    lane = jnp.arange(MLA_QK_PAD) == MLA_QK
    pad_g = lambda g: jnp.pad(g, (0, MLA_QK_PAD - MLA_QK))
    gq2 = jnp.stack([pad_g(g_qn), jnp.where(lane, -shift, 0.0)]).astype(F32)
    gk2 = jnp.stack([pad_g(g_kn), jnp.where(lane, 1.0, 0.0)]).astype(F32)
    fixed_flag = (shift <= MLA_FIXED_SHIFT_MAX).astype(jnp.int32).reshape(1)
    return w_all, wuq, wukv, gq2, gk2, fixed_flag


def _prep_gates(w_a, b_a, w_x, b_x):
    wg = (0.5 * jnp.concatenate([w_a, w_x], axis=-1)).astype(BF16)
    bg = 0.5 * jnp.concatenate([b_a.reshape(LRU_BLOCKS, 1, LRU_BLOCK_DIM),
                                b_x.reshape(LRU_BLOCKS, 1, LRU_BLOCK_DIM)], axis=-1)
    return wg, bg


def _tile(tp, prefs):
    for t in prefs:
        if tp % t == 0:
            return t
    raise ValueError(f"no tile of {prefs} divides {tp}")


def kernel(x, meta_tokens, norm_g, even_w_in, mla_g_q_lat, mla_g_kv_lat, mla_w_uq, mla_w_ukv, mla_g_qn, mla_g_kn, swa_g_qn, swa_g_kn, swa_sink, even_w_out, odd_w_in, lru_conv_w, lru_conv_b, lru_w_a, lru_b_a, lru_w_x, lru_b_x, lru_lambda, odd_w_out):
    batch, seq, d = x.shape
    assert d == D_MODEL and batch == SUBLANE and seq % BLOCK == 0
    tp = BLOCK + seq
    tm = _tile(tp, (384, 128))
    tq = _tile(tp, (2112, 1056, 384, 128))
    tkc = _tile(seq, (512, 256, 128))
    assert seq // tkc >= 2, "the MLA key-chunk pipeline needs at least two chunks"
    tc = _tile(tp, (64, 32, 16))

    meta = jnp.broadcast_to(meta_tokens.astype(x.dtype)[None], (batch, N_META, D_MODEL))
    h = jnp.concatenate([jnp.zeros((batch, PAD_FRONT, D_MODEL), x.dtype), meta, x], axis=1)
    cm, sm, cs, ss = _rope_tables(tp)

    even_w_in_bf = even_w_in.astype(BF16)
    for l in range(DEPTH):
        j = l // 2
        gn = norm_g[l].reshape(1, D_MODEL)
        if l % 2 == 0:
            w_all, wuq, wukv, gqn, gkn, fixed_flag = _prep_even(even_w_in_bf[j], mla_w_uq[j], mla_w_ukv[j],
                                                                mla_g_qn[j], mla_g_kn[j])
            h2d = h if l == 0 else h.reshape(tp, batch * D_MODEL)
            qm, km, vm, qs, ks, vs, gate = _even_in(
                h2d, gn, w_all, mla_g_q_lat[j].reshape(1, -1), mla_g_kv_lat[j].reshape(1, -1), wuq, wukv,
                gqn, gkn, swa_g_qn[j].reshape(1, -1), swa_g_kn[j].reshape(1, -1), cm, sm, cs, ss,
                batch=batch, tp=tp, tm=tm)
            ya = _mla_attn(fixed_flag, qm, km, vm, gate, batch=batch, tp=tp, tq=tq, tkc=tkc)
            yb = _swa_attn(qs, ks, vs, swa_sink[j], swa_g_qn[j], swa_g_kn[j], gate, batch=batch, tp=tp)
            h2d = _even_out(ya, yb, even_w_out[j].astype(BF16), h2d, batch=batch, tp=tp, tm=tm)
            h = h2d.reshape(tp, batch, D_MODEL)
        else:
            wgf, bgf = _prep_gates(lru_w_a[j, 0], lru_b_a[j, 0], lru_w_x[j, 0], lru_b_x[j, 0])
            wgb, bgb = _prep_gates(lru_w_a[j, 1], lru_b_a[j, 1], lru_w_x[j, 1], lru_b_x[j, 1])
            xc, gate, hf = _odd_fwd(h, gn, odd_w_in[j].astype(BF16), lru_conv_w[j], lru_conv_b[j].reshape(1, -1),
                                    wgf, bgf, lru_lambda[j, 0].reshape(1, -1), batch=batch, tp=tp, tc=tc)
            h = _odd_bwd(xc, gate, hf, h, wgb, bgb, lru_lambda[j, 1].reshape(1, -1), odd_w_out[j].astype(BF16),
                         batch=batch, tp=tp, tc=tc, final=(l == DEPTH - 1))
    return h
```
